```python
import jax
import jax.numpy as jnp
from jax import lax
import numpy as np

D_MODEL = 1024
BATCH = 8
SEQ = 4096
DEPTH = 4
DEC_BATCH = 8
DEC_SEQ = 2048
PAST_LEN = 128

HEAD_DIM = 64
A_WIDTH = D_MODEL // 2
A_HEADS = A_WIDTH // HEAD_DIM
B_WIDTH = D_MODEL - A_WIDTH
IN_AB = 3 * A_WIDTH + 3 * B_WIDTH
DILATED_BRANCHES = ((128, 1), (512, 4), (2048, 16))
CONV_WIDTH = 3
ROPE_THETA = 500000.0
ROPE_DIM = HEAD_DIM // 4
RWKV_HEAD = 64
RWKV_HEADS = D_MODEL // RWKV_HEAD
DECAY_LORA = 64
AAA_LORA = 64
GATE_LORA = 128
GN_EPS = 64e-5
N_MEM = 256
CA_HEADS = 4
CA_HEAD_DIM = D_MODEL // CA_HEADS
FF_RAW = -(-8 * D_MODEL // 3)
D_FF = -(-FF_RAW // 256) * 256
N_EVEN = (DEPTH + 1) // 2
N_ODD = DEPTH // 2
RMS_EPS = 1e-6
NEG_INF = -1e30

kernel_name = 'hybrid_dilated_conv_rwkv7_encoder'


def rmsnorm(x, g):
    x32 = x.astype(jnp.float32)
    y = x32 * lax.rsqrt(jnp.mean(x32 * x32, axis=-1, keepdims=True) + RMS_EPS)
    return y.astype(x.dtype) * g


def partial_rotary(x, pos):
    half = ROPE_DIM // 2
    inv = ROPE_THETA ** (-2.0 * jnp.arange(half, dtype=jnp.float32) / ROPE_DIM)
    ang = pos.astype(jnp.float32)[:, None] * inv[None, :]
    cos = jnp.cos(ang)[None, :, None, :]
    sin = jnp.sin(ang)[None, :, None, :]
    x1 = x[..., :half].astype(jnp.float32)
    x2 = x[..., half:ROPE_DIM].astype(jnp.float32)
    rot = jnp.concatenate([x1 * cos - x2 * sin, x2 * cos + x1 * sin], axis=-1).astype(x.dtype)
    return jnp.concatenate([rot, x[..., ROPE_DIM:]], axis=-1)


def dilated_window_branch(q, k, v, window, dilation):
    B, S, H, hd = q.shape
    half = window // (2 * dilation)
    blk = half
    L = S // dilation
    nb = -(-L // blk)
    Lp = nb * blk

    def strided(t):
        return t.reshape(B, L, dilation, H, hd)

    qb = jnp.pad(strided(q), ((0, 0), (0, Lp - L), (0, 0), (0, 0), (0, 0)))
    qb = qb.reshape(B, nb, blk, dilation, H, hd)

    def neighbourhood(t):
        tb = jnp.pad(strided(t), ((0, 0), (blk, Lp - L + blk), (0, 0), (0, 0), (0, 0)))
        tb = tb.reshape(B, nb + 2, blk, dilation, H, hd)
        return jnp.concatenate([tb[:, :-2], tb[:, 1:-1], tb[:, 2:]], axis=2)

    kb = neighbourhood(k)
    vb = neighbourhood(v)
    s = jnp.einsum('bnqrhd,bnkrhd->bnrhqk', qb, kb, preferred_element_type=jnp.float32)
    qi = jnp.arange(blk)[:, None]
    kj = jnp.arange(3 * blk)[None, :]
    band = jnp.abs(kj - blk - qi) <= half
    tk = (jnp.arange(nb)[:, None] - 1) * blk + jnp.arange(3 * blk)[None, :]
    inside = (tk >= 0) & (tk < L)
    mask = band[None, :, :] & inside[:, None, :]
    s = jnp.where(mask[None, :, None, None], s, NEG_INF)
    m = jnp.max(s, axis=-1, keepdims=True)
    p = jnp.exp(s - m)
    den = jnp.sum(p, axis=-1, keepdims=True)
    o = jnp.einsum('bnrhqk,bnkrhd->bnqrhd', (p / den).astype(v.dtype), vb)
    lse = (m + jnp.log(den))[..., 0]
    o = o.reshape(B, Lp, dilation, H, hd)[:, :L].reshape(B, S, H, hd)
    lse = jnp.transpose(lse, (0, 1, 4, 2, 3)).reshape(B, Lp, dilation, H)[:, :L].reshape(B, S, H)
    return o, lse


def dilated_attention(q, k, v):
    outs, lses = [], []
    for window, dilation in DILATED_BRANCHES:
        o, l = dilated_window_branch(q, k, v, window, dilation)
        outs.append(o)
        lses.append(l)
    wts = jax.nn.softmax(jnp.stack(lses, axis=0), axis=0).astype(v.dtype)
    return jnp.einsum('gbsh,gbshd->bshd', wts, jnp.stack(outs, axis=0))


def short_gated_conv(bg, cg, h, conv_w):
    S = h.shape[1]
    u = cg * h
    pad = CONV_WIDTH // 2
    up = jnp.pad(u, ((0, 0), (pad, pad), (0, 0)))
    y = sum(conv_w[j] * up[:, j:j + S] for j in range(CONV_WIDTH))
    return bg * y


def window_conv_mixer(xn, pos, w_in, w_out, conv_w):
    B, S, _ = xn.shape
    z = xn @ w_in
    cuts = [A_WIDTH, 2 * A_WIDTH, 3 * A_WIDTH, 3 * A_WIDTH + B_WIDTH, 3 * A_WIDTH + 2 * B_WIDTH]
    q, k, v, bg, cg, h = jnp.split(z, cuts, axis=-1)

    def heads(t):
        return t.reshape(B, S, A_HEADS, HEAD_DIM)

    q = partial_rotary(heads(q), pos) * (HEAD_DIM ** -0.5)
    k = partial_rotary(heads(k), pos)
    ya = dilated_attention(q, k, heads(v)).reshape(B, S, A_WIDTH)
    yb = short_gated_conv(bg, cg, h, conv_w)
    return jnp.concatenate([ya, yb], axis=-1) @ w_out


def wkv7_scan(r, w, k, v, a, b, reverse):
    B, S, H, N = r.shape

    def step(state, inp):
        rt, wt, kt, vt, at, bt = inp
        sa = jnp.einsum('bhij,bhj->bhi', state, at)
        state = state * wt[:, :, None, :] + sa[..., None] * bt[:, :, None, :] + vt[..., None] * kt[:, :, None, :]
        return state, jnp.einsum('bhij,bhj->bhi', state, rt)

    xs = tuple(jnp.moveaxis(t, 1, 0) for t in (r, w, k, v, a, b))
    s0 = jnp.zeros((B, H, N, N), jnp.float32)
    _, out = lax.scan(step, s0, xs, reverse=reverse)
    return jnp.moveaxis(out, 0, 1)


def rwkv7_bidir_time_mix(xn, mu, w_r, w_k, w_v, w_o, w0, w1, w2, a0, a1, a2, g1, g2, k_k, k_a, r_k, lnx_w, lnx_b):
    B, S, D = xn.shape
    f32 = jnp.float32

    def heads(t):
        return t.astype(f32).reshape(B, S, RWKV_HEADS, RWKV_HEAD)

    xp = jnp.pad(xn, ((0, 0), (1, 1), (0, 0)))
    xx = 0.5 * (xp[:, :-2] + xp[:, 2:]) - xn
    xr, xw, xk, xv, xa, xg = [xn + xx * mu[i] for i in range(6)]
    r = heads(xr @ w_r)
    k_lin = xk @ w_k
    k = heads(k_lin)
    v = heads(xv @ w_v)
    g = jax.nn.sigmoid(xg @ g1) @ g2
    kk = heads(k_lin * k_k)
    kk = kk * lax.rsqrt(jnp.maximum(jnp.sum(kk * kk, axis=-1, keepdims=True), 1e-24))
    k_a_h = k_a.astype(f32).reshape(RWKV_HEADS, RWKV_HEAD)
    r_k32 = r_k.astype(f32)
    outs, bonus = [], []
    for d in range(2):
        w_log = -jax.nn.softplus(-(w0[d] + jnp.tanh(xw @ w1[d]) @ w2[d]).astype(f32)) - 0.5
        decay = heads(jnp.exp(-jnp.exp(w_log)))
        a = heads(jax.nn.sigmoid((a0[d] + (xa @ a1[d]) @ a2[d]).astype(f32)))
        k_d = k * (1.0 + (a - 1.0) * k_a_h)
        outs.append(wkv7_scan(r, decay, k_d, v, -kk, kk * a, reverse=(d == 1)))
        bonus.append(jnp.sum(r * k_d * r_k32, axis=-1, keepdims=True) * v)
    y = outs[0] + outs[1]
    mean = jnp.mean(y, axis=-1, keepdims=True)
    var = jnp.mean(jnp.square(y - mean), axis=-1, keepdims=True)
    y = (y - mean) * lax.rsqrt(var + GN_EPS)
    y = y.reshape(B, S, D) * lnx_w.astype(f32) + lnx_b.astype(f32) + (bonus[0] + bonus[1]).reshape(B, S, D)
    return (y.astype(xn.dtype) * g) @ w_o


def memory_cross_attention(xn, mem_n, w_q, w_kv, w_o):
    B, S, D = xn.shape
    M = mem_n.shape[1]
    q = (xn @ w_q).reshape(B, S, CA_HEADS, CA_HEAD_DIM)
    k, v = jnp.split(mem_n @ w_kv, 2, axis=-1)
    k = k.reshape(B, M, CA_HEADS, CA_HEAD_DIM)
    v = v.reshape(B, M, CA_HEADS, CA_HEAD_DIM)
    s = jnp.einsum('bqhd,bkhd->bhqk', q, k, preferred_element_type=jnp.float32) * (CA_HEAD_DIM ** -0.5)
    p = jax.nn.softmax(s, axis=-1).astype(v.dtype)
    o = jnp.einsum('bhqk,bkhd->bqhd', p, v).reshape(B, S, D)
    return o @ w_o


def swiglu(xn, w_gu, w_down):
    gate, up = jnp.split(xn @ w_gu, 2, axis=-1)
    return (jax.nn.silu(gate) * up) @ w_down


def encoder_trunk(x, mem, p):
    pos = jnp.arange(x.shape[1])
    for l in range(DEPTH):
        xn = rmsnorm(x, p['norm_mix'][l])
        if l % 2 == 0:
            e = l // 2
            x = x + window_conv_mixer(xn, pos, p['ab_w_in'][e], p['ab_w_out'][e], p['ab_conv'][e])
        else:
            o = l // 2
            x = x + rwkv7_bidir_time_mix(
                xn, p['rw_mu'][o], p['rw_wr'][o], p['rw_wk'][o], p['rw_wv'][o], p['rw_wo'][o],
                p['rw_w0'][o], p['rw_w1'][o], p['rw_w2'][o], p['rw_a0'][o], p['rw_a1'][o], p['rw_a2'][o],
                p['rw_g1'][o], p['rw_g2'][o], p['rw_kk'][o], p['rw_ka'][o], p['rw_rk'][o],
                p['rw_lnx_w'][o], p['rw_lnx_b'][o])
        mem_n = rmsnorm(mem, p['norm_mem'][l])
        x = x + memory_cross_attention(rmsnorm(x, p['norm_cross'][l]), mem_n,
                                       p['ca_wq'][l], p['ca_wkv'][l], p['ca_wo'][l])
        x = x + swiglu(rmsnorm(x, p['norm_ffn'][l]), p['ffn_wgu'][l], p['ffn_wdown'][l])
    return rmsnorm(x, p['norm_final'])


def setup_inputs(seed: int = 0) -> dict:
    key = jax.random.key(seed)
    keys = jax.random.split(key, 64)
    counter = iter(range(64))
    f32 = jnp.float32

    def nk():
        return keys[next(counter)]

    def nrm(shape, scale):
        return jax.random.normal(nk(), shape, f32) * scale

    def gain(shape, base=1.0):
        return base + 0.05 * jax.random.normal(nk(), shape, f32)

    def unif(shape, lo, hi):
        return jax.random.uniform(nk(), shape, f32, lo, hi)

    D = D_MODEL
    return {
        'x_prompt': nrm((BATCH, SEQ, D), 1.0),
        'x_sample': nrm((DEC_BATCH, DEC_SEQ, D), 1.0),
        'mem_prompt': nrm((BATCH, N_MEM, D), 1.0),
        'mem_sample': nrm((DEC_BATCH, N_MEM, D), 1.0),
        'norm_mix': gain((DEPTH, D)),
        'norm_cross': gain((DEPTH, D)),
        'norm_mem': gain((DEPTH, D)),
        'norm_ffn': gain((DEPTH, D)),
        'norm_final': gain((D,)),
        'ab_w_in': nrm((N_EVEN, D, IN_AB), D ** -0.5),
        'ab_w_out': nrm((N_EVEN, A_WIDTH + B_WIDTH, D), (A_WIDTH + B_WIDTH) ** -0.5),
        'ab_conv': nrm((N_EVEN, CONV_WIDTH, B_WIDTH), CONV_WIDTH ** -0.5),
        'rw_mu': unif((N_ODD, 6, D), 0.0, 1.0),
        'rw_wr': nrm((N_ODD, D, D), D ** -0.5),
        'rw_wk': nrm((N_ODD, D, D), D ** -0.5),
        'rw_wv': nrm((N_ODD, D, D), D ** -0.5),
        'rw_wo': nrm((N_ODD, D, D), D ** -0.5),
        'rw_w0': unif((N_ODD, 2, D), -5.0, 0.0),
        'rw_w1': nrm((N_ODD, 2, D, DECAY_LORA), D ** -0.5),
        'rw_w2': nrm((N_ODD, 2, DECAY_LORA, D), 0.1 * DECAY_LORA ** -0.5),
        'rw_a0': nrm((N_ODD, 2, D), 0.5),
        'rw_a1': nrm((N_ODD, 2, D, AAA_LORA), D ** -0.5),
        'rw_a2': nrm((N_ODD, 2, AAA_LORA, D), 0.5 * AAA_LORA ** -0.5),
        'rw_g1': nrm((N_ODD, D, GATE_LORA), D ** -0.5),
        'rw_g2': nrm((N_ODD, GATE_LORA, D), GATE_LORA ** -0.5),
        'rw_kk': gain((N_ODD, D), 0.85),
        'rw_ka': gain((N_ODD, D)),
        'rw_rk': nrm((N_ODD, RWKV_HEADS, RWKV_HEAD), 0.1),
        'rw_lnx_w': gain((N_ODD, D)),
        'rw_lnx_b': nrm((N_ODD, D), 0.02),
        'ca_wq': nrm((DEPTH, D, D), D ** -0.5),
        'ca_wkv': nrm((DEPTH, D, 2 * D), D ** -0.5),
        'ca_wo': nrm((DEPTH, D, D), D ** -0.5),
        'ffn_wgu': nrm((DEPTH, D, 2 * D_FF), D ** -0.5),
        'ffn_wdown': nrm((DEPTH, D_FF, D), D_FF ** -0.5),
    }


def reference(x_prompt, x_sample, mem_prompt, mem_sample, norm_mix, norm_cross, norm_mem, norm_ffn,
              norm_final, ab_w_in, ab_w_out, ab_conv, rw_mu, rw_wr, rw_wk, rw_wv, rw_wo, rw_w0, rw_w1,
              rw_w2, rw_a0, rw_a1, rw_a2, rw_g1, rw_g2, rw_kk, rw_ka, rw_rk, rw_lnx_w, rw_lnx_b,
              ca_wq, ca_wkv, ca_wo, ffn_wgu, ffn_wdown):
    p = dict(norm_mix=norm_mix, norm_cross=norm_cross, norm_mem=norm_mem, norm_ffn=norm_ffn,
             norm_final=norm_final, ab_w_in=ab_w_in, ab_w_out=ab_w_out, ab_conv=ab_conv,
             rw_mu=rw_mu, rw_wr=rw_wr, rw_wk=rw_wk, rw_wv=rw_wv, rw_wo=rw_wo,
             rw_w0=rw_w0, rw_w1=rw_w1, rw_w2=rw_w2, rw_a0=rw_a0, rw_a1=rw_a1, rw_a2=rw_a2,
             rw_g1=rw_g1, rw_g2=rw_g2, rw_kk=rw_kk, rw_ka=rw_ka, rw_rk=rw_rk,
             rw_lnx_w=rw_lnx_w, rw_lnx_b=rw_lnx_b, ca_wq=ca_wq, ca_wkv=ca_wkv, ca_wo=ca_wo,
             ffn_wgu=ffn_wgu, ffn_wdown=ffn_wdown)
    y_prompt = encoder_trunk(x_prompt, mem_prompt, p)
    y_sample = encoder_trunk(x_sample, mem_sample, p)
    return (y_prompt, y_sample)
```

```python
import functools

import jax
import jax.numpy as jnp
from jax import lax
from jax.experimental import pallas as pl
from jax.experimental.pallas import tpu as pltpu

F32 = jnp.float32
BF16 = jnp.bfloat16

D_MODEL = 1024
DEPTH = 4
HEAD_DIM = 64
A_WIDTH = 512
B_WIDTH = 512
DILATIONS = (1, 4, 16)
BAND = 64
ROPE_THETA = 500000.0
ROPE_DIM = 16
RWKV_HEAD = 64
GN_EPS = 64e-5
CA_HEADS = 4
CA_HEAD_DIM = 256
D_FF = 2816
RMS_EPS = 1e-6
NEG_INF = -1e30
CHUNK = 64
LANES = 128

VMEM_LIMIT = 48 * 1024 * 1024

TM = 512
TM_RW = 256
TQ_DIL = 128
TF_FFN = 1408
TS_LOCAL = 128
TS_SEQ = 128
PAIRS_SEQ = 4


def _cparams(sem):
    return pltpu.CompilerParams(dimension_semantics=sem, vmem_limit_bytes=VMEM_LIMIT)


def _full(shape):
    n = len(shape)
    return pl.BlockSpec(shape, lambda *_: (0,) * n)


def _rms(x, g):
    ms = jnp.mean(x * x, axis=-1, keepdims=True)
    return x * lax.rsqrt(ms + RMS_EPS) * g


def _dot(a, b):
    return jnp.dot(a, b, preferred_element_type=F32)


def _dot_nt(a, b):
    return lax.dot_general(a, b, (((1,), (1,)), ((), ())), preferred_element_type=F32)


def _dot_tn(a, b):
    return lax.dot_general(a, b, (((0,), (0,)), ((), ())), preferred_element_type=F32)


def _split(x):
    hi = x.astype(BF16)
    lo = (x - hi.astype(F32)).astype(BF16)
    return hi, lo


def _norm_matmul_kernel(x_ref, g_ref, w_ref, o_ref, xn_ref):
    @pl.when(pl.program_id(1) == 0)
    def _():
        xn_ref[...] = _rms(x_ref[...], g_ref[...]).astype(BF16)

    o_ref[...] = _dot(xn_ref[...], w_ref[...]).astype(o_ref.dtype)


def _norm_matmul(x, g, w, tm, tn, out_dtype):
    m, k = x.shape
    n = w.shape[1]
    return pl.pallas_call(
        _norm_matmul_kernel,
        grid=(m // tm, n // tn),
        in_specs=[pl.BlockSpec((tm, k), lambda i, j: (i, 0)),
                  pl.BlockSpec((1, k), lambda i, j: (0, 0)),
                  pl.BlockSpec((k, tn), lambda i, j: (0, j))],
        out_specs=pl.BlockSpec((tm, tn), lambda i, j: (i, j)),
        out_shape=jax.ShapeDtypeStruct((m, n), out_dtype),
        scratch_shapes=[pltpu.VMEM((tm, k), BF16)],
        compiler_params=_cparams(("parallel", "arbitrary")),
        name="norm_matmul",
    )(x, g.reshape(1, k), w)


def _qkv_kernel(x_ref, g_ref, w_ref, cos_ref, sin_ref, o1_ref, o4_ref, o16_ref, xn_ref, acc_ref, slab_ref, *, tm):
    j = pl.program_id(1)

    @pl.when(j == 0)
    def _():
        xn_ref[...] = _rms(x_ref[...], g_ref[...]).astype(BF16)

    acc_ref[...] = _dot(xn_ref[...], w_ref[...])

    @pl.when(j < 2)
    def _():
        a = acc_ref[...]
        lane = lax.broadcasted_iota(jnp.int32, a.shape, 1) & (HEAD_DIM - 1)
        half = ROPE_DIM // 2
        partner = jnp.where(lane < half, pltpu.roll(a, A_WIDTH - half, 1), pltpu.roll(a, half, 1))
        scale = jnp.where(j == 0, HEAD_DIM ** -0.5, 1.0).astype(F32)
        acc_ref[...] = (a * cos_ref[...] + partner * sin_ref[...]) * scale

    o1_ref[...] = acc_ref[...].astype(BF16)
    for c in range(A_WIDTH // LANES):
        sl = slice(c * LANES, (c + 1) * LANES)
        slab_ref[c] = acc_ref[:, sl]
        for r in range(4):
            o4_ref[r, :, sl] = slab_ref[c, pl.ds(r, tm // 4, stride=4), :].astype(BF16)
        for r in range(16):
            o16_ref[r, :, sl] = slab_ref[c, pl.ds(r, tm // 16, stride=16), :].astype(BF16)


def _qkv_proj(x, g, w_qkv, cos, sin, batch, seq):
    m = x.shape[0]
    tm = TM
    nts = seq // tm
    kern = functools.partial(_qkv_kernel, tm=tm)
    out_shape = (jax.ShapeDtypeStruct((3, m, A_WIDTH), BF16),
                 jax.ShapeDtypeStruct((3, batch, 4, seq // 4, A_WIDTH), BF16),
                 jax.ShapeDtypeStruct((3, batch, 16, seq // 16, A_WIDTH), BF16))
    return pl.pallas_call(
        kern,
        grid=(m // tm, 3),
        in_specs=[pl.BlockSpec((tm, D_MODEL), lambda i, j: (i, 0)),
                  pl.BlockSpec((1, D_MODEL), lambda i, j: (0, 0)),
                  pl.BlockSpec((D_MODEL, A_WIDTH), lambda i, j: (0, j)),
                  pl.BlockSpec((tm, A_WIDTH), lambda i, j: (i % nts, 0)),
                  pl.BlockSpec((tm, A_WIDTH), lambda i, j: (i % nts, 0))],
        out_specs=(pl.BlockSpec((None, tm, A_WIDTH), lambda i, j: (j, i, 0)),
                   pl.BlockSpec((None, None, 4, tm // 4, A_WIDTH), lambda i, j: (j, i // nts, 0, i % nts, 0)),
                   pl.BlockSpec((None, None, 16, tm // 16, A_WIDTH), lambda i, j: (j, i // nts, 0, i % nts, 0))),
        out_shape=out_shape,
        scratch_shapes=[pltpu.VMEM((tm, D_MODEL), BF16), pltpu.VMEM((tm, A_WIDTH), F32),
                        pltpu.VMEM((A_WIDTH // LANES, tm, LANES), F32)],
        compiler_params=_cparams(("parallel", "arbitrary")),
        name="qkv_proj",
    )(x, g.reshape(1, D_MODEL), w_qkv, cos, sin)


def _rope_tables(seq):
    half = ROPE_DIM // 2
    inv = ROPE_THETA ** (-2.0 * jnp.arange(half, dtype=F32) / ROPE_DIM)
    ang = jnp.arange(seq, dtype=F32)[:, None] * inv[None, :]
    cos, sin = jnp.cos(ang), jnp.sin(ang)
    rest = HEAD_DIM - ROPE_DIM
    cos_h = jnp.concatenate([cos, cos, jnp.ones((seq, rest), F32)], axis=1)
    sin_h = jnp.concatenate([-sin, sin, jnp.zeros((seq, rest), F32)], axis=1)
    reps = A_WIDTH // HEAD_DIM
    return jnp.tile(cos_h, (1, reps)), jnp.tile(sin_h, (1, reps))


def _dil_kernel(q_ref, kp_ref, kc_ref, kn_ref, vp_ref, vc_ref, vn_ref, o_ref, lse_ref, *, tq, length):
    qi = pl.program_id(1)
    span = tq + 2 * BAND
    q = q_ref[...]
    k = jnp.concatenate([kp_ref[...], kc_ref[...], kn_ref[...]], axis=0)
    v = jnp.concatenate([vp_ref[...], vc_ref[...], vn_ref[...]], axis=0)
    row = lax.broadcasted_iota(jnp.int32, (tq, span), 0)
    col = lax.broadcasted_iota(jnp.int32, (tq, span), 1)
    key_pos = qi * tq - BAND + col
    rel = col - BAND - row
    valid = (rel <= BAND) & (rel >= -BAND) & (key_pos >= 0) & (key_pos < length)
    first = lax.broadcasted_iota(jnp.int32, (tq, LANES), 1) < HEAD_DIM
    for p in range(A_WIDTH // LANES):
        sl = slice(p * LANES, (p + 1) * LANES)
        qp, kp, vp = q[:, sl], k[:, sl], v[:, sl]
        outs, lses = [], []
        for sel in (first, jnp.logical_not(first)):
            s = _dot_nt(jnp.where(sel, qp, jnp.zeros_like(qp)), kp)
            s = jnp.where(valid, s, NEG_INF)
            mx = jnp.max(s, axis=-1, keepdims=True)
            e = jnp.exp(s - mx)
            den = jnp.sum(e, axis=-1, keepdims=True)
            outs.append(_dot(e.astype(BF16), vp) * (1.0 / den))
            lses.append(mx + jnp.log(den))
        o_ref[:, sl] = jnp.where(first, outs[0], outs[1]).astype(BF16)
        lse_ref[:, sl] = jnp.where(first, lses[0], lses[1])


def _dilated_branch(zd, length):
    g = zd.shape[1]
    tq = min(TQ_DIL, length)
    nb = length // BAND
    r = tq // BAND
    kern = functools.partial(_dil_kernel, tq=tq, length=length)

    def cur(which):
        return pl.BlockSpec((None, None, tq, A_WIDTH), lambda b, i: (which, b, i, 0))

    def prev(which):
        return pl.BlockSpec((None, None, BAND, A_WIDTH), lambda b, i: (which, b, jnp.maximum(i * r - 1, 0), 0))

    def nxt(which):
        return pl.BlockSpec((None, None, BAND, A_WIDTH), lambda b, i: (which, b, jnp.minimum((i + 1) * r, nb - 1), 0))

    return pl.pallas_call(
        kern,
        grid=(g, length // tq),
        in_specs=[cur(0), prev(1), cur(1), nxt(1), prev(2), cur(2), nxt(2)],
        out_specs=(pl.BlockSpec((None, tq, A_WIDTH), lambda b, i: (b, i, 0)),
                   pl.BlockSpec((None, tq, A_WIDTH), lambda b, i: (b, i, 0))),
        out_shape=(jax.ShapeDtypeStruct((g, length, A_WIDTH), BF16),
                   jax.ShapeDtypeStruct((g, length, A_WIDTH), F32)),
        compiler_params=_cparams(("parallel", "parallel")),
        name="dilated_attn",
    )(zd, zd, zd, zd, zd, zd, zd)


def _even_out_kernel(o1_ref, l1_ref, o4_ref, l4_ref, o16_ref, l16_ref, bg_ref, cg_ref, h_ref,
                     cgp_ref, hp_ref, cgn_ref, hn_ref, cw_ref, wa_ref, wb_ref, x_ref, out_ref,
                     s4o, s4l, s16o, s16l, *, tm, nts):
    it = pl.program_id(0) % nts
    nslab = A_WIDTH // LANES
    for c in range(nslab):
        sl = slice(c * LANES, (c + 1) * LANES)
        for r in range(4):
            s4o[c, pl.ds(r, tm // 4, stride=4), :] = o4_ref[r, :, sl].astype(F32)
            s4l[c, pl.ds(r, tm // 4, stride=4), :] = l4_ref[r, :, sl]
        for r in range(16):
            s16o[c, pl.ds(r, tm // 16, stride=16), :] = o16_ref[r, :, sl].astype(F32)
            s16l[c, pl.ds(r, tm // 16, stride=16), :] = l16_ref[r, :, sl]
    wide = lambda ref: jnp.concatenate([ref[c] for c in range(nslab)], axis=1)
    l1, l4, l16 = l1_ref[...], wide(s4l), wide(s16l)
    mx = jnp.maximum(jnp.maximum(l1, l4), l16)
    e1, e4, e16 = jnp.exp(l1 - mx), jnp.exp(l4 - mx), jnp.exp(l16 - mx)
    ya = (e1 * o1_ref[...].astype(F32) + e4 * wide(s4o) + e16 * wide(s16o)) * (1.0 / (e1 + e4 + e16))

    u = cg_ref[...].astype(F32) * h_ref[...].astype(F32)
    last = cgp_ref.shape[0] - 1
    u_before = (cgp_ref[...].astype(F32) * hp_ref[...].astype(F32))[last:last + 1]
    u_after = (cgn_ref[...].astype(F32) * hn_ref[...].astype(F32))[0:1]
    u_before = u_before * (it > 0).astype(F32)
    u_after = u_after * (it < nts - 1).astype(F32)
    row = lax.broadcasted_iota(jnp.int32, u.shape, 0)
    u_prev = jnp.where(row == 0, u_before, pltpu.roll(u, 1, 0))
    u_next = jnp.where(row == tm - 1, u_after, pltpu.roll(u, tm - 1, 0))
    cw = cw_ref[...]
    yb = bg_ref[...].astype(F32) * (cw[0:1] * u_prev + cw[1:2] * u + cw[2:3] * u_next)

    out_ref[...] = x_ref[...] + _dot(ya.astype(BF16), wa_ref[...]) + _dot(yb.astype(BF16), wb_ref[...])


def _even_out(x, o1, l1, o4, l4, o16, l16, zc, conv_w, w_out, batch, seq):
    m = x.shape[0]
    tm = TM
    nts = seq // tm
    hb = 16
    nhb = m // hb
    kern = functools.partial(_even_out_kernel, tm=tm, nts=nts)
    nat = lambda: pl.BlockSpec((tm, A_WIDTH), lambda i: (i, 0))
    dil = lambda d: pl.BlockSpec((None, d, tm // d, A_WIDTH), lambda i: (i // nts, 0, i % nts, 0))
    col = lambda c: pl.BlockSpec((tm, B_WIDTH), lambda i: (i, c))
    before = lambda c: pl.BlockSpec((hb, B_WIDTH), lambda i: (jnp.maximum(i * (tm // hb) - 1, 0), c))
    after = lambda c: pl.BlockSpec((hb, B_WIDTH), lambda i: (jnp.minimum((i + 1) * (tm // hb), nhb - 1), c))
    return pl.pallas_call(
        kern,
        grid=(m // tm,),
        in_specs=[nat(), nat(), dil(4), dil(4), dil(16), dil(16),
                  col(0), col(1), col(2), before(1), before(2), after(1), after(2),
                  _full((3, B_WIDTH)),
                  pl.BlockSpec((A_WIDTH, D_MODEL), lambda i: (0, 0)),
                  pl.BlockSpec((B_WIDTH, D_MODEL), lambda i: (1, 0)),
                  pl.BlockSpec((tm, D_MODEL), lambda i: (i, 0))],
        out_specs=pl.BlockSpec((tm, D_MODEL), lambda i: (i, 0)),
        out_shape=jax.ShapeDtypeStruct((m, D_MODEL), F32),
        scratch_shapes=[pltpu.VMEM((A_WIDTH // LANES, tm, LANES), F32)] * 4,
        compiler_params=_cparams(("parallel",)),
        name="even_out",
    )(o1, l1, o4, l4, o16, l16, zc, zc, zc, zc, zc, zc, zc, conv_w, w_out, w_out, x)


def _cross_kernel(x_ref, g_ref, k_ref, v_ref, wq_ref, wo_ref, out_ref):
    x = x_ref[...]
    xn = _rms(x, g_ref[...]).astype(BF16)
    q = (_dot(xn, wq_ref[...]) * (CA_HEAD_DIM ** -0.5)).astype(BF16)
    k = k_ref[...]
    v = v_ref[...]
    outs = []
    for h in range(CA_HEADS):
        sl = slice(h * CA_HEAD_DIM, (h + 1) * CA_HEAD_DIM)
        s = _dot_nt(q[:, sl], k[:, sl])
        mx = jnp.max(s, axis=-1, keepdims=True)
        e = jnp.exp(s - mx)
        den = jnp.sum(e, axis=-1, keepdims=True)
        outs.append(_dot((e * (1.0 / den)).astype(BF16), v[:, sl]).astype(BF16))
    o = jnp.concatenate(outs, axis=1)
    out_ref[...] = x + _dot(o, wo_ref[...])


def _cross_attn(x, g, kv, wq, wo, batch, seq, n_mem):
    m = x.shape[0]
    tq = TM
    nts = seq // tq
    return pl.pallas_call(
        _cross_kernel,
        grid=(m // tq,),
        in_specs=[pl.BlockSpec((tq, D_MODEL), lambda i: (i, 0)),
                  _full((1, D_MODEL)),
                  pl.BlockSpec((n_mem, D_MODEL), lambda i: (i // nts, 0)),
                  pl.BlockSpec((n_mem, D_MODEL), lambda i: (i // nts, 1)),
                  _full((D_MODEL, D_MODEL)),
                  _full((D_MODEL, D_MODEL))],
        out_specs=pl.BlockSpec((tq, D_MODEL), lambda i: (i, 0)),
        out_shape=jax.ShapeDtypeStruct((m, D_MODEL), F32),
        compiler_params=_cparams(("parallel",)),
        name="cross_attn",
    )(x, g.reshape(1, D_MODEL), kv, kv, wq, wo)


def _ffn_kernel(x_ref, g_ref, wg_ref, wu_ref, wd_ref, gf_ref, out_ref, xn_ref, acc_ref, *, final_norm):
    f = pl.program_id(1)

    @pl.when(f == 0)
    def _():
        xn_ref[...] = _rms(x_ref[...], g_ref[...]).astype(BF16)
        acc_ref[...] = x_ref[...]

    xn = xn_ref[...]
    gate = _dot(xn, wg_ref[...])
    up = _dot(xn, wu_ref[...])
    act = (gate * (1.0 / (1.0 + jnp.exp(-gate))) * up).astype(BF16)
    acc_ref[...] += _dot(act, wd_ref[...])

    @pl.when(f == pl.num_programs(1) - 1)
    def _():
        y = acc_ref[...]
        if final_norm:
            y = _rms(y, gf_ref[...])
        out_ref[...] = y


def _ffn(x, g, w_gu, w_down, g_final, final_norm):
    m = x.shape[0]
    tm, tf = TM, TF_FFN
    nf = D_FF // tf
    kern = functools.partial(_ffn_kernel, final_norm=final_norm)
    return pl.pallas_call(
        kern,
        grid=(m // tm, nf),
        in_specs=[pl.BlockSpec((tm, D_MODEL), lambda i, f: (i, 0)),
                  pl.BlockSpec((1, D_MODEL), lambda i, f: (0, 0)),
                  pl.BlockSpec((D_MODEL, tf), lambda i, f: (0, f)),
                  pl.BlockSpec((D_MODEL, tf), lambda i, f: (0, nf + f)),
                  pl.BlockSpec((tf, D_MODEL), lambda i, f: (f, 0)),
                  pl.BlockSpec((1, D_MODEL), lambda i, f: (0, 0))],
        out_specs=pl.BlockSpec((tm, D_MODEL), lambda i, f: (i, 0)),
        out_shape=jax.ShapeDtypeStruct((m, D_MODEL), F32),
        scratch_shapes=[pltpu.VMEM((tm, D_MODEL), BF16), pltpu.VMEM((tm, D_MODEL), F32)],
        compiler_params=_cparams(("parallel", "arbitrary")),
        name="swiglu",
    )(x, g.reshape(1, D_MODEL), w_gu, w_gu, w_down, g_final.reshape(1, D_MODEL))


def _head_sum(x, e_ref):
    hi, lo = _split(x)
    e = e_ref[...]
    cols = []
    for c in range(x.shape[1] // LANES):
        sl = slice(c * LANES, (c + 1) * LANES)
        cols.append(_dot(hi[:, sl], e) + _dot(lo[:, sl], e))
    return jnp.concatenate(cols, axis=1)


def _sigmoid(x):
    return 1.0 / (1.0 + jnp.exp(-x))


def _rwkv_pre_kernel(x_ref, xb_ref, xa_ref, g_ref, mu_ref, wr_ref, wk_ref, wv_ref, w1_ref, w2_ref, w0_ref,
                     a1_ref, a2_ref, a0_ref, g1_ref, g2_ref, kk_ref, ka_ref, rk_ref, e_ref,
                     r_out, v_out, kn_out, kd_out, ba_out, lw_out, gate_out, bonus_out, *, tm, nts):
    it = pl.program_id(0) % nts
    g = g_ref[...]
    xn = _rms(x_ref[...], g)
    hb = xb_ref.shape[0]
    before = _rms(xb_ref[...], g)[hb - 1:hb] * (it > 0).astype(F32)
    after = _rms(xa_ref[...], g)[0:1] * (it < nts - 1).astype(F32)
    row = lax.broadcasted_iota(jnp.int32, xn.shape, 0)
    x_prev = jnp.where(row == 0, before, pltpu.roll(xn, 1, 0))
    x_next = jnp.where(row == tm - 1, after, pltpu.roll(xn, tm - 1, 0))
    xx = 0.5 * (x_prev + x_next) - xn
    mu = mu_ref[...]
    mix = lambda i: (xn + xx * mu[i:i + 1]).astype(BF16)

    r = _dot(mix(0), wr_ref[...])
    k_lin = _dot(mix(2), wk_ref[...])
    v = _dot(mix(3), wv_ref[...])
    gate = _dot(_sigmoid(_dot(mix(5), g1_ref[...])).astype(BF16), g2_ref[...])
    kn = k_lin * kk_ref[...]
    kn = kn * lax.rsqrt(jnp.maximum(_head_sum(kn * kn, e_ref), 1e-24))
    tw = jnp.tanh(_dot(mix(1), w1_ref[...])).astype(BF16)
    ta = _dot(mix(4), a1_ref[...]).astype(BF16)
    ka = ka_ref[...]
    kd_sum = jnp.zeros_like(k_lin)
    for d in range(2):
        wl = w0_ref[d:d + 1] + _dot(tw, w2_ref[d])
        soft = jnp.maximum(-wl, 0.0) + jnp.log(1.0 + jnp.exp(-jnp.abs(wl)))
        lw_out[d] = -jnp.exp(-soft - 0.5)
        a = _sigmoid(a0_ref[d:d + 1] + _dot(ta, a2_ref[d]))
        kd = k_lin * (1.0 + (a - 1.0) * ka)
        kd_sum = kd_sum + kd
        kd_out[d] = kd.astype(BF16)
        ba_out[d] = (kn * a).astype(BF16)
    r_out[...] = r.astype(BF16)
    v_out[...] = v.astype(BF16)
    kn_out[...] = kn.astype(BF16)
    gate_out[...] = gate.astype(BF16)
    bonus_out[...] = _head_sum(r * kd_sum * rk_ref[...], e_ref) * v


def _rwkv_pre(x, g, p, batch, seq):
    m = x.shape[0]
    tm = TM_RW
    nts = seq // tm
    hb = 8
    nhb = m // hb
    kern = functools.partial(_rwkv_pre_kernel, tm=tm, nts=nts)
    row = lambda: pl.BlockSpec((tm, D_MODEL), lambda i: (i, 0))
    row2 = lambda: pl.BlockSpec((2, tm, D_MODEL), lambda i: (0, i, 0))
    args = (x, x, x, g.reshape(1, D_MODEL), p['mu'], p['wr'], p['wk'], p['wv'], p['w1'], p['w2'], p['w0'],
            p['a1'], p['a2'], p['a0'], p['g1'], p['g2'], p['kk'], p['ka'], p['rk'], p['e'])
    in_specs = [row(),
                pl.BlockSpec((hb, D_MODEL), lambda i: (jnp.maximum(i * (tm // hb) - 1, 0), 0)),
                pl.BlockSpec((hb, D_MODEL), lambda i: (jnp.minimum((i + 1) * (tm // hb), nhb - 1), 0))]
    in_specs += [_full(a.shape) for a in args[3:]]
    out_shape = (jax.ShapeDtypeStruct((m, D_MODEL), BF16),) * 3 + (
        jax.ShapeDtypeStruct((2, m, D_MODEL), BF16), jax.ShapeDtypeStruct((2, m, D_MODEL), BF16),
        jax.ShapeDtypeStruct((2, m, D_MODEL), F32),
        jax.ShapeDtypeStruct((m, D_MODEL), BF16), jax.ShapeDtypeStruct((m, D_MODEL), F32))
    out_specs = (row(), row(), row(), row2(), row2(), row2(), row(), row())
    return pl.pallas_call(
        kern, grid=(m // tm,), in_specs=in_specs, out_specs=out_specs, out_shape=out_shape,
        compiler_params=_cparams(("parallel",)), name="rwkv_pre",
    )(*args)


def _bd(x, first):
    z = jnp.zeros_like(x)
    return jnp.concatenate([jnp.where(first, x, z), jnp.where(first, z, x)], axis=0)


def _pair_mm(a, b, first):
    return _dot(a.astype(BF16), _bd(b.astype(BF16), first))


def _compact(full, first):
    return jnp.where(first, full[:CHUNK], full[CHUNK:])


def _wkv_local_chunk(r, v, kn, kd, ba, lw, sgn, first, ones_bf):
    c = CHUNK
    row = lax.broadcasted_iota(jnp.int32, (c, c), 0)
    col = lax.broadcasted_iota(jnp.int32, (c, c), 1)
    tri = ((row - col) * sgn >= 0).astype(BF16)
    hi, lo = _split(lw)
    cum = _dot(tri, hi) + _dot(tri, lo)
    total = jnp.sum(lw, axis=0, keepdims=True)
    p_in = jnp.exp(cum)
    p_ex = jnp.exp(cum - lw)
    p_inv = jnp.exp(-cum)
    p_hat = jnp.exp(total - cum)
    at = -kn * p_ex
    rt = r * p_in
    bt = ba * p_inv
    kt = kd * p_inv
    bh = ba * p_hat
    kh = kd * p_hat

    prow = lax.broadcasted_iota(jnp.int32, (c, LANES), 0)
    pcol = lax.broadcasted_iota(jnp.int32, (c, LANES), 1) & (c - 1)
    dlt = (prow - pcol) * sgn
    strict = dlt > 0
    incl = dlt >= 0

    lhs = jnp.concatenate([at, rt], axis=0).astype(BF16)
    sb = _dot_nt(lhs, _bd(bt.astype(BF16), first))
    sk = _dot_nt(lhs, _bd(kt.astype(BF16), first))
    a_ab = jnp.where(strict, sb[:c], 0.0)
    a_ak = jnp.where(strict, sk[:c], 0.0)
    a_rb = jnp.where(incl, sb[c:], 0.0)
    a_rk = jnp.where(incl, sk[c:], 0.0)

    same = lambda sh: (prow >> sh) == (pcol >> sh)
    inv = jnp.where(prow == pcol, 1.0, 0.0) + jnp.where(same(1), a_ab, 0.0)
    sh = 1
    while (1 << sh) < c:
        off = jnp.where(same(sh + 1) & jnp.logical_not(same(sh)), a_ab, 0.0)
        inv = inv + _pair_mm(inv, _pair_mm(off, inv, first), first)
        sh += 1

    akv = _pair_mm(a_ak, v, first)
    w1 = _pair_mm(inv, at, first)
    u0 = _pair_mm(inv, akv, first)
    o0 = _pair_mm(a_rk, v, first)
    z0 = _compact(_dot_tn(kh.astype(BF16), v.astype(BF16)), first)
    em = jnp.exp(_compact(_dot_tn(hi, ones_bf) + _dot_tn(lo, ones_bf), first))
    return w1, u0, rt, o0, a_rb, bh, z0, em


def _wkv_local_kernel(r_ref, v_ref, kn_ref, kd_ref, ba_ref, lw_ref,
                      w1_ref, u0_ref, rt_ref, o0_ref, arb_ref, bh_ref, z0_ref, em_ref, *, ts):
    sgn = 1 - 2 * pl.program_id(0)
    first = lax.broadcasted_iota(jnp.int32, (CHUNK, LANES), 1) < RWKV_HEAD
    ones_bf = jnp.ones((CHUNK, LANES), BF16)
    outs = (w1_ref, u0_ref, rt_ref, o0_ref, arb_ref, bh_ref, z0_ref, em_ref)
    for c in range(ts // CHUNK):
        sl = pl.ds(c * CHUNK, CHUNK)
        f = lambda ref: ref[sl, :].astype(F32)
        res = _wkv_local_chunk(f(r_ref), f(v_ref), f(kn_ref), f(kd_ref), f(ba_ref), f(lw_ref),
                               sgn, first, ones_bf)
        for o_ref, val in zip(outs, res):
            o_ref[sl, :] = val.astype(o_ref.dtype)


def _wkv_local(r, v, kn, kd, ba, lw, batch, seq):
    ts = TS_LOCAL
    npair = D_MODEL // LANES
    sh3 = lambda a: a.reshape(batch, seq, D_MODEL)
    sh4 = lambda a: a.reshape(2, batch, seq, D_MODEL)
    shared = lambda: pl.BlockSpec((None, ts, LANES), lambda d, b, p, t: (b, t, p))
    per_dir = lambda: pl.BlockSpec((None, None, ts, LANES), lambda d, b, p, t: (d, b, t, p))
    dts = (BF16, F32, BF16, F32, BF16, BF16, F32, F32)
    return pl.pallas_call(
        functools.partial(_wkv_local_kernel, ts=ts),
        grid=(2, batch, npair, seq // ts),
        in_specs=[shared(), shared(), shared(), per_dir(), per_dir(), per_dir()],
        out_specs=tuple(per_dir() for _ in dts),
        out_shape=tuple(jax.ShapeDtypeStruct((2, batch, seq, D_MODEL), dt) for dt in dts),
        compiler_params=_cparams(("parallel",) * 4),
        name="wkv_local",
    )(sh3(r), sh3(v), sh3(kn), sh4(kd), sh4(ba), sh4(lw))


def _wkv_seq_kernel(*refs, ts, npairs):
    ins, (of_ref, ob_ref, state) = refs[:16], refs[16:]
    first = lax.broadcasted_iota(jnp.int32, (CHUNK, LANES), 1) < RWKV_HEAD

    @pl.when(pl.program_id(2) == 0)
    def _():
        state[...] = jnp.zeros_like(state)

    nchunk = ts // CHUNK
    for d in range(2):
        w1_ref, u0_ref, rt_ref, o0_ref, arb_ref, bh_ref, z0_ref, em_ref = ins[8 * d:8 * d + 8]
        o_ref = of_ref if d == 0 else ob_ref
        order = range(nchunk) if d == 0 else range(nchunk - 1, -1, -1)
        for p in range(npairs):
            z = state[d, p]
            for c in order:
                idx = (pl.ds(c * CHUNK, CHUNK), pl.ds(p * LANES, LANES))
                zb = _bd(z.astype(BF16), first)
                u = _dot(w1_ref[idx], zb) + u0_ref[idx]
                o = _dot(rt_ref[idx], zb) + _pair_mm(arb_ref[idx], u, first) + o0_ref[idx]
                o_ref[idx] = o
                z = em_ref[idx] * z + _compact(_dot_tn(bh_ref[idx], u.astype(BF16)), first) + z0_ref[idx]
            state[d, p] = z


def _wkv_seq(local, batch, seq):
    ts, npairs = TS_SEQ, PAIRS_SEQ
    nt = seq // ts
    width = npairs * LANES
    fwd = lambda: pl.BlockSpec((None, None, ts, width), lambda b, g, t: (0, b, t, g))
    bwd = lambda: pl.BlockSpec((None, None, ts, width), lambda b, g, t: (1, b, nt - 1 - t, g))
    out = lambda rev: pl.BlockSpec((None, ts, width), (lambda b, g, t: (b, nt - 1 - t, g)) if rev
                                   else (lambda b, g, t: (b, t, g)))
    o_f, o_b = pl.pallas_call(
        functools.partial(_wkv_seq_kernel, ts=ts, npairs=npairs),
        grid=(batch, D_MODEL // width, nt),
        in_specs=[fwd() for _ in local] + [bwd() for _ in local],
        out_specs=(out(False), out(True)),
        out_shape=(jax.ShapeDtypeStruct((batch, seq, D_MODEL), F32),) * 2,
        scratch_shapes=[pltpu.VMEM((2, npairs, CHUNK, LANES), F32)],
        compiler_params=_cparams(("parallel", "parallel", "arbitrary")),
        name="wkv_seq",
    )(*local, *local)
    return o_f.reshape(batch * seq, D_MODEL), o_b.reshape(batch * seq, D_MODEL)


def _rwkv_post_kernel(of_ref, ob_ref, bonus_ref, gate_ref, lw_ref, lb_ref, e_ref, wo_ref, x_ref, out_ref):
    y = of_ref[...] + ob_ref[...]
    mean = _head_sum(y, e_ref) * (1.0 / RWKV_HEAD)
    yc = y - mean
    var = _head_sum(yc * yc, e_ref) * (1.0 / RWKV_HEAD)
    yn = yc * lax.rsqrt(var + GN_EPS)
    y2 = yn * lw_ref[...] + lb_ref[...] + bonus_ref[...]
    out_ref[...] = x_ref[...] + _dot((y2 * gate_ref[...].astype(F32)).astype(BF16), wo_ref[...])


def _rwkv_post(x, o_f, o_b, bonus, gate, p):
    m = x.shape[0]
    tm = TM_RW
    row = lambda: pl.BlockSpec((tm, D_MODEL), lambda i: (i, 0))
    return pl.pallas_call(
        _rwkv_post_kernel,
        grid=(m // tm,),
        in_specs=[row(), row(), row(), row(), _full((1, D_MODEL)), _full((1, D_MODEL)),
                  _full((LANES, LANES)), _full((D_MODEL, D_MODEL)), row()],
        out_specs=row(),
        out_shape=jax.ShapeDtypeStruct((m, D_MODEL), F32),
        compiler_params=_cparams(("parallel",)),
        name="rwkv_post",
    )(o_f, o_b, bonus, gate, p['lnx_w'], p['lnx_b'], p['e'], p['wo'], x)


def _prep_odd(o, rw_mu, rw_wr, rw_wk, rw_wv, rw_wo, rw_w0, rw_w1, rw_w2, rw_a0, rw_a1, rw_a2, rw_g1, rw_g2,
              rw_kk, rw_ka, rw_rk, rw_lnx_w, rw_lnx_b):
    bf = lambda a: a.astype(BF16)

    def pad_dir(w):
        z = jnp.zeros_like(w[0])
        return jnp.stack([jnp.concatenate([w[0], z], axis=0), jnp.concatenate([z, w[1]], axis=0)])

    lane = jnp.arange(LANES) // RWKV_HEAD
    return dict(
        mu=rw_mu[o], wr=bf(rw_wr[o]), wk=bf(rw_wk[o]), wv=bf(rw_wv[o]), wo=bf(rw_wo[o]),
        w0=rw_w0[o], w1=bf(jnp.concatenate([rw_w1[o, 0], rw_w1[o, 1]], axis=1)), w2=bf(pad_dir(rw_w2[o])),
        a0=rw_a0[o], a1=bf(jnp.concatenate([rw_a1[o, 0], rw_a1[o, 1]], axis=1)), a2=bf(pad_dir(rw_a2[o])),
        g1=bf(rw_g1[o]), g2=bf(rw_g2[o]),
        kk=rw_kk[o].reshape(1, D_MODEL), ka=rw_ka[o].reshape(1, D_MODEL), rk=rw_rk[o].reshape(1, D_MODEL),
        lnx_w=rw_lnx_w[o].reshape(1, D_MODEL), lnx_b=rw_lnx_b[o].reshape(1, D_MODEL),
        e=(lane[:, None] == lane[None, :]).astype(BF16))


def _even_layer(x, g, w_in, w_out, conv_w, cos, sin, batch, seq):
    z1, z4, z16 = _qkv_proj(x, g, w_in[:, :3 * A_WIDTH], cos, sin, batch, seq)
    zc = _norm_matmul(x, g, w_in[:, 3 * A_WIDTH:], TM, B_WIDTH, BF16)
    m = x.shape[0]
    o1, l1 = _dilated_branch(z1.reshape(3, batch, seq, A_WIDTH), seq)
    o4, l4 = _dilated_branch(z4.reshape(3, batch * 4, seq // 4, A_WIDTH), seq // 4)
    o16, l16 = _dilated_branch(z16.reshape(3, batch * 16, seq // 16, A_WIDTH), seq // 16)
    o1, l1 = o1.reshape(m, A_WIDTH), l1.reshape(m, A_WIDTH)
    d4 = lambda a: a.reshape(batch, 4, seq // 4, A_WIDTH)
    d16 = lambda a: a.reshape(batch, 16, seq // 16, A_WIDTH)
    return _even_out(x, o1, l1, d4(o4), d4(l4), d16(o16), d16(l16), zc, conv_w, w_out, batch, seq)


def _odd_layer(x, g, p, batch, seq):
    r, v, kn, kd, ba, lw, gate, bonus = _rwkv_pre(x, g, p, batch, seq)
    local = _wkv_local(r, v, kn, kd, ba, lw, batch, seq)
    o_f, o_b = _wkv_seq(local, batch, seq)
    return _rwkv_post(x, o_f, o_b, bonus, gate, p)


def _trunk(x, mem, w):
    batch, seq, _ = x.shape
    n_mem = mem.shape[1]
    x = x.reshape(batch * seq, D_MODEL)
    mem = mem.reshape(batch * n_mem, D_MODEL)
    cos, sin = _rope_tables(seq)
    for l in range(DEPTH):
        if l % 2 == 0:
            e = l // 2
            x = _even_layer(x, w['norm_mix'][l], w['ab_w_in'][e], w['ab_w_out'][e], w['ab_conv'][e],
                            cos, sin, batch, seq)
        else:
            x = _odd_layer(x, w['norm_mix'][l], w['odd'][l // 2], batch, seq)
        kv = _norm_matmul(mem, w['norm_mem'][l], w['ca_wkv'][l], n_mem, D_MODEL, BF16)
        x = _cross_attn(x, w['norm_cross'][l], kv, w['ca_wq'][l], w['ca_wo'][l], batch, seq, n_mem)
        x = _ffn(x, w['norm_ffn'][l], w['ffn_wgu'][l], w['ffn_wdown'][l], w['norm_final'], l == DEPTH - 1)
    return x.reshape(batch, seq, D_MODEL)


def kernel(x_prompt, x_sample, mem_prompt, mem_sample, norm_mix, norm_cross, norm_mem, norm_ffn, norm_final,
           ab_w_in, ab_w_out, ab_conv, rw_mu, rw_wr, rw_wk, rw_wv, rw_wo, rw_w0, rw_w1, rw_w2, rw_a0, rw_a1,
           rw_a2, rw_g1, rw_g2, rw_kk, rw_ka, rw_rk, rw_lnx_w, rw_lnx_b, ca_wq, ca_wkv, ca_wo, ffn_wgu,
           ffn_wdown):
    bf = lambda a: a.astype(BF16)
    w = dict(norm_mix=norm_mix, norm_cross=norm_cross, norm_mem=norm_mem, norm_ffn=norm_ffn,
             norm_final=norm_final, ab_w_in=bf(ab_w_in), ab_w_out=bf(ab_w_out), ab_conv=ab_conv,
             ca_wq=bf(ca_wq), ca_wkv=bf(ca_wkv), ca_wo=bf(ca_wo), ffn_wgu=bf(ffn_wgu), ffn_wdown=bf(ffn_wdown),
             odd=[_prep_odd(o, rw_mu, rw_wr, rw_wk, rw_wv, rw_wo, rw_w0, rw_w1, rw_w2, rw_a0, rw_a1, rw_a2,
                            rw_g1, rw_g2, rw_kk, rw_ka, rw_rk, rw_lnx_w, rw_lnx_b)
                  for o in range(rw_mu.shape[0])])
    return _trunk(x_prompt, mem_prompt, w), _trunk(x_sample, mem_sample, w)
```

```python
import functools

import jax
import jax.numpy as jnp
from jax import lax
from jax.experimental import pallas as pl
from jax.experimental.pallas import tpu as pltpu

F32 = jnp.float32
BF16 = jnp.bfloat16

D_MODEL = 1024
DEPTH = 4
HEAD_DIM = 64
A_WIDTH = 512
B_WIDTH = 512
DILATIONS = (1, 4, 16)
BAND = 64
ROPE_THETA = 500000.0
ROPE_DIM = 16
RWKV_HEAD = 64
GN_EPS = 64e-5
CA_HEADS = 4
CA_HEAD_DIM = 256
D_FF = 2816
RMS_EPS = 1e-6
NEG_INF = -1e30
CHUNK = 64
LANES = 128

VMEM_LIMIT = 48 * 1024 * 1024

TM = 512
TM_RW = 256
TQ_DIL = 128
TF_FFN = 1408
TS_WKV = 512
PAIRS_WKV = 4


def _cparams(sem):
    return pltpu.CompilerParams(dimension_semantics=sem, vmem_limit_bytes=VMEM_LIMIT)


def _full(shape):
    n = len(shape)
    return pl.BlockSpec(shape, lambda *_: (0,) * n)


def _rms(x, g):
    ms = jnp.mean(x * x, axis=-1, keepdims=True)
    return x * lax.rsqrt(ms + RMS_EPS) * g


def _dot(a, b):
    return jnp.dot(a, b, preferred_element_type=F32)


def _dot_nt(a, b):
    return lax.dot_general(a, b, (((1,), (1,)), ((), ())), preferred_element_type=F32)


def _dot_tn(a, b):
    return lax.dot_general(a, b, (((0,), (0,)), ((), ())), preferred_element_type=F32)


def _split(x):
    hi = x.astype(BF16)
    lo = (x - hi.astype(F32)).astype(BF16)
    return hi, lo


def _norm_matmul_kernel(x_ref, g_ref, w_ref, o_ref, xn_ref):
    @pl.when(pl.program_id(1) == 0)
    def _():
        xn_ref[...] = _rms(x_ref[...], g_ref[...]).astype(BF16)

    o_ref[...] = _dot(xn_ref[...], w_ref[...]).astype(o_ref.dtype)


def _norm_matmul(x, g, w, tm, tn, out_dtype):
    m, k = x.shape
    n = w.shape[1]
    return pl.pallas_call(
        _norm_matmul_kernel,
        grid=(m // tm, n // tn),
        in_specs=[pl.BlockSpec((tm, k), lambda i, j: (i, 0)),
                  pl.BlockSpec((1, k), lambda i, j: (0, 0)),
                  pl.BlockSpec((k, tn), lambda i, j: (0, j))],
        out_specs=pl.BlockSpec((tm, tn), lambda i, j: (i, j)),
        out_shape=jax.ShapeDtypeStruct((m, n), out_dtype),
        scratch_shapes=[pltpu.VMEM((tm, k), BF16)],
        compiler_params=_cparams(("parallel", "arbitrary")),
        name="norm_matmul",
    )(x, g.reshape(1, k), w)


def _qkv_kernel(x_ref, g_ref, w_ref, cos_ref, sin_ref, o1_ref, o4_ref, o16_ref, xn_ref, acc_ref, slab_ref, *, tm):
    j = pl.program_id(1)

    @pl.when(j == 0)
    def _():
        xn_ref[...] = _rms(x_ref[...], g_ref[...]).astype(BF16)

    acc_ref[...] = _dot(xn_ref[...], w_ref[...])

    @pl.when(j < 2)
    def _():
        a = acc_ref[...]
        lane = lax.broadcasted_iota(jnp.int32, a.shape, 1) & (HEAD_DIM - 1)
        half = ROPE_DIM // 2
        partner = jnp.where(lane < half, pltpu.roll(a, A_WIDTH - half, 1), pltpu.roll(a, half, 1))
        scale = jnp.where(j == 0, HEAD_DIM ** -0.5, 1.0).astype(F32)
        acc_ref[...] = (a * cos_ref[...] + partner * sin_ref[...]) * scale

    o1_ref[...] = acc_ref[...].astype(BF16)
    for c in range(A_WIDTH // LANES):
        sl = slice(c * LANES, (c + 1) * LANES)
        slab_ref[c] = acc_ref[:, sl]
        for r in range(4):
            o4_ref[r, :, sl] = slab_ref[c, pl.ds(r, tm // 4, stride=4), :].astype(BF16)
        for r in range(16):
            o16_ref[r, :, sl] = slab_ref[c, pl.ds(r, tm // 16, stride=16), :].astype(BF16)


def _qkv_proj(x, g, w_qkv, cos, sin, batch, seq):
    m = x.shape[0]
    tm = TM
    nts = seq // tm
    kern = functools.partial(_qkv_kernel, tm=tm)
    out_shape = (jax.ShapeDtypeStruct((3, m, A_WIDTH), BF16),
                 jax.ShapeDtypeStruct((3, batch, 4, seq // 4, A_WIDTH), BF16),
                 jax.ShapeDtypeStruct((3, batch, 16, seq // 16, A_WIDTH), BF16))
    return pl.pallas_call(
        kern,
        grid=(m // tm, 3),
        in_specs=[pl.BlockSpec((tm, D_MODEL), lambda i, j: (i, 0)),
                  pl.BlockSpec((1, D_MODEL), lambda i, j: (0, 0)),
                  pl.BlockSpec((D_MODEL, A_WIDTH), lambda i, j: (0, j)),
                  pl.BlockSpec((tm, A_WIDTH), lambda i, j: (i % nts, 0)),
                  pl.BlockSpec((tm, A_WIDTH), lambda i, j: (i % nts, 0))],
        out_specs=(pl.BlockSpec((None, tm, A_WIDTH), lambda i, j: (j, i, 0)),
                   pl.BlockSpec((None, None, 4, tm // 4, A_WIDTH), lambda i, j: (j, i // nts, 0, i % nts, 0)),
                   pl.BlockSpec((None, None, 16, tm // 16, A_WIDTH), lambda i, j: (j, i // nts, 0, i % nts, 0))),
        out_shape=out_shape,
        scratch_shapes=[pltpu.VMEM((tm, D_MODEL), BF16), pltpu.VMEM((tm, A_WIDTH), F32),
                        pltpu.VMEM((A_WIDTH // LANES, tm, LANES), F32)],
        compiler_params=_cparams(("parallel", "arbitrary")),
        name="qkv_proj",
    )(x, g.reshape(1, D_MODEL), w_qkv, cos, sin)


def _rope_tables(seq):
    half = ROPE_DIM // 2
    inv = ROPE_THETA ** (-2.0 * jnp.arange(half, dtype=F32) / ROPE_DIM)
    ang = jnp.arange(seq, dtype=F32)[:, None] * inv[None, :]
    cos, sin = jnp.cos(ang), jnp.sin(ang)
    rest = HEAD_DIM - ROPE_DIM
    cos_h = jnp.concatenate([cos, cos, jnp.ones((seq, rest), F32)], axis=1)
    sin_h = jnp.concatenate([-sin, sin, jnp.zeros((seq, rest), F32)], axis=1)
    reps = A_WIDTH // HEAD_DIM
    return jnp.tile(cos_h, (1, reps)), jnp.tile(sin_h, (1, reps))


def _dil_kernel(q_ref, kp_ref, kc_ref, kn_ref, vp_ref, vc_ref, vn_ref, o_ref, lse_ref, *, tq, length):
    qi = pl.program_id(1)
    span = tq + 2 * BAND
    q = q_ref[...]
    k = jnp.concatenate([kp_ref[...], kc_ref[...], kn_ref[...]], axis=0)
    v = jnp.concatenate([vp_ref[...], vc_ref[...], vn_ref[...]], axis=0)
    row = lax.broadcasted_iota(jnp.int32, (tq, span), 0)
    col = lax.broadcasted_iota(jnp.int32, (tq, span), 1)
    key_pos = qi * tq - BAND + col
    rel = col - BAND - row
    valid = (rel <= BAND) & (rel >= -BAND) & (key_pos >= 0) & (key_pos < length)
    first = lax.broadcasted_iota(jnp.int32, (tq, LANES), 1) < HEAD_DIM
    for p in range(A_WIDTH // LANES):
        sl = slice(p * LANES, (p + 1) * LANES)
        qp, kp, vp = q[:, sl], k[:, sl], v[:, sl]
        outs, lses = [], []
        for sel in (first, jnp.logical_not(first)):
            s = _dot_nt(jnp.where(sel, qp, jnp.zeros_like(qp)), kp)
            s = jnp.where(valid, s, NEG_INF)
            mx = jnp.max(s, axis=-1, keepdims=True)
            e = jnp.exp(s - mx)
            den = jnp.sum(e, axis=-1, keepdims=True)
            outs.append(_dot(e.astype(BF16), vp) * (1.0 / den))
            lses.append(mx + jnp.log(den))
        o_ref[:, sl] = jnp.where(first, outs[0], outs[1]).astype(BF16)
        lse_ref[:, sl] = jnp.where(first, lses[0], lses[1])


def _dilated_branch(zd, length):
    g = zd.shape[1]
    tq = min(TQ_DIL, length)
    nb = length // BAND
    r = tq // BAND
    kern = functools.partial(_dil_kernel, tq=tq, length=length)

    def cur(which):
        return pl.BlockSpec((None, None, tq, A_WIDTH), lambda b, i: (which, b, i, 0))

    def prev(which):
        return pl.BlockSpec((None, None, BAND, A_WIDTH), lambda b, i: (which, b, jnp.maximum(i * r - 1, 0), 0))

    def nxt(which):
        return pl.BlockSpec((None, None, BAND, A_WIDTH), lambda b, i: (which, b, jnp.minimum((i + 1) * r, nb - 1), 0))

    return pl.pallas_call(
        kern,
        grid=(g, length // tq),
        in_specs=[cur(0), prev(1), cur(1), nxt(1), prev(2), cur(2), nxt(2)],
        out_specs=(pl.BlockSpec((None, tq, A_WIDTH), lambda b, i: (b, i, 0)),
                   pl.BlockSpec((None, tq, A_WIDTH), lambda b, i: (b, i, 0))),
        out_shape=(jax.ShapeDtypeStruct((g, length, A_WIDTH), BF16),
                   jax.ShapeDtypeStruct((g, length, A_WIDTH), F32)),
        compiler_params=_cparams(("parallel", "parallel")),
        name="dilated_attn",
    )(zd, zd, zd, zd, zd, zd, zd)


def _even_out_kernel(o1_ref, l1_ref, o4_ref, l4_ref, o16_ref, l16_ref, bg_ref, cg_ref, h_ref,
                     cgp_ref, hp_ref, cgn_ref, hn_ref, cw_ref, wa_ref, wb_ref, x_ref, out_ref,
                     s4o, s4l, s16o, s16l, *, tm, nts):
    it = pl.program_id(0) % nts
    nslab = A_WIDTH // LANES
    for c in range(nslab):
        sl = slice(c * LANES, (c + 1) * LANES)
        for r in range(4):
            s4o[c, pl.ds(r, tm // 4, stride=4), :] = o4_ref[r, :, sl].astype(F32)
            s4l[c, pl.ds(r, tm // 4, stride=4), :] = l4_ref[r, :, sl]
        for r in range(16):
            s16o[c, pl.ds(r, tm // 16, stride=16), :] = o16_ref[r, :, sl].astype(F32)
            s16l[c, pl.ds(r, tm // 16, stride=16), :] = l16_ref[r, :, sl]
    wide = lambda ref: jnp.concatenate([ref[c] for c in range(nslab)], axis=1)
    l1, l4, l16 = l1_ref[...], wide(s4l), wide(s16l)
    mx = jnp.maximum(jnp.maximum(l1, l4), l16)
    e1, e4, e16 = jnp.exp(l1 - mx), jnp.exp(l4 - mx), jnp.exp(l16 - mx)
    ya = (e1 * o1_ref[...].astype(F32) + e4 * wide(s4o) + e16 * wide(s16o)) * (1.0 / (e1 + e4 + e16))

    u = cg_ref[...].astype(F32) * h_ref[...].astype(F32)
    last = cgp_ref.shape[0] - 1
    u_before = (cgp_ref[...].astype(F32) * hp_ref[...].astype(F32))[last:last + 1]
    u_after = (cgn_ref[...].astype(F32) * hn_ref[...].astype(F32))[0:1]
    u_before = u_before * (it > 0).astype(F32)
    u_after = u_after * (it < nts - 1).astype(F32)
    row = lax.broadcasted_iota(jnp.int32, u.shape, 0)
    u_prev = jnp.where(row == 0, u_before, pltpu.roll(u, 1, 0))
    u_next = jnp.where(row == tm - 1, u_after, pltpu.roll(u, tm - 1, 0))
    cw = cw_ref[...]
    yb = bg_ref[...].astype(F32) * (cw[0:1] * u_prev + cw[1:2] * u + cw[2:3] * u_next)

    out_ref[...] = x_ref[...] + _dot(ya.astype(BF16), wa_ref[...]) + _dot(yb.astype(BF16), wb_ref[...])


def _even_out(x, o1, l1, o4, l4, o16, l16, zc, conv_w, w_out, batch, seq):
    m = x.shape[0]
    tm = TM
    nts = seq // tm
    hb = 16
    nhb = m // hb
    kern = functools.partial(_even_out_kernel, tm=tm, nts=nts)
    nat = lambda: pl.BlockSpec((tm, A_WIDTH), lambda i: (i, 0))
    dil = lambda d: pl.BlockSpec((None, d, tm // d, A_WIDTH), lambda i: (i // nts, 0, i % nts, 0))
    col = lambda c: pl.BlockSpec((tm, B_WIDTH), lambda i: (i, c))
    before = lambda c: pl.BlockSpec((hb, B_WIDTH), lambda i: (jnp.maximum(i * (tm // hb) - 1, 0), c))
    after = lambda c: pl.BlockSpec((hb, B_WIDTH), lambda i: (jnp.minimum((i + 1) * (tm // hb), nhb - 1), c))
    return pl.pallas_call(
        kern,
        grid=(m // tm,),
        in_specs=[nat(), nat(), dil(4), dil(4), dil(16), dil(16),
                  col(0), col(1), col(2), before(1), before(2), after(1), after(2),
                  _full((3, B_WIDTH)),
                  pl.BlockSpec((A_WIDTH, D_MODEL), lambda i: (0, 0)),
                  pl.BlockSpec((B_WIDTH, D_MODEL), lambda i: (1, 0)),
                  pl.BlockSpec((tm, D_MODEL), lambda i: (i, 0))],
        out_specs=pl.BlockSpec((tm, D_MODEL), lambda i: (i, 0)),
        out_shape=jax.ShapeDtypeStruct((m, D_MODEL), F32),
        scratch_shapes=[pltpu.VMEM((A_WIDTH // LANES, tm, LANES), F32)] * 4,
        compiler_params=_cparams(("parallel",)),
        name="even_out",
    )(o1, l1, o4, l4, o16, l16, zc, zc, zc, zc, zc, zc, zc, conv_w, w_out, w_out, x)


def _cross_kernel(x_ref, g_ref, k_ref, v_ref, wq_ref, wo_ref, out_ref):
    x = x_ref[...]
    xn = _rms(x, g_ref[...]).astype(BF16)
    q = (_dot(xn, wq_ref[...]) * (CA_HEAD_DIM ** -0.5)).astype(BF16)
    k = k_ref[...]
    v = v_ref[...]
    outs = []
    for h in range(CA_HEADS):
        sl = slice(h * CA_HEAD_DIM, (h + 1) * CA_HEAD_DIM)
        s = _dot_nt(q[:, sl], k[:, sl])
        mx = jnp.max(s, axis=-1, keepdims=True)
        e = jnp.exp(s - mx)
        den = jnp.sum(e, axis=-1, keepdims=True)
        outs.append(_dot((e * (1.0 / den)).astype(BF16), v[:, sl]).astype(BF16))
    o = jnp.concatenate(outs, axis=1)
    out_ref[...] = x + _dot(o, wo_ref[...])


def _cross_attn(x, g, kv, wq, wo, batch, seq, n_mem):
    m = x.shape[0]
    tq = TM
    nts = seq // tq
    return pl.pallas_call(
        _cross_kernel,
        grid=(m // tq,),
        in_specs=[pl.BlockSpec((tq, D_MODEL), lambda i: (i, 0)),
                  _full((1, D_MODEL)),
                  pl.BlockSpec((n_mem, D_MODEL), lambda i: (i // nts, 0)),
                  pl.BlockSpec((n_mem, D_MODEL), lambda i: (i // nts, 1)),
                  _full((D_MODEL, D_MODEL)),
                  _full((D_MODEL, D_MODEL))],
        out_specs=pl.BlockSpec((tq, D_MODEL), lambda i: (i, 0)),
        out_shape=jax.ShapeDtypeStruct((m, D_MODEL), F32),
        compiler_params=_cparams(("parallel",)),
        name="cross_attn",
    )(x, g.reshape(1, D_MODEL), kv, kv, wq, wo)


def _ffn_kernel(x_ref, g_ref, wg_ref, wu_ref, wd_ref, gf_ref, out_ref, xn_ref, acc_ref, *, final_norm):
    f = pl.program_id(1)

    @pl.when(f == 0)
    def _():
        xn_ref[...] = _rms(x_ref[...], g_ref[...]).astype(BF16)
        acc_ref[...] = x_ref[...]

    xn = xn_ref[...]
    gate = _dot(xn, wg_ref[...])
    up = _dot(xn, wu_ref[...])
    act = (gate * (1.0 / (1.0 + jnp.exp(-gate))) * up).astype(BF16)
    acc_ref[...] += _dot(act, wd_ref[...])

    @pl.when(f == pl.num_programs(1) - 1)
    def _():
        y = acc_ref[...]
        if final_norm:
            y = _rms(y, gf_ref[...])
        out_ref[...] = y


def _ffn(x, g, w_gu, w_down, g_final, final_norm):
    m = x.shape[0]
    tm, tf = TM, TF_FFN
    nf = D_FF // tf
    kern = functools.partial(_ffn_kernel, final_norm=final_norm)
    return pl.pallas_call(
        kern,
        grid=(m // tm, nf),
        in_specs=[pl.BlockSpec((tm, D_MODEL), lambda i, f: (i, 0)),
                  pl.BlockSpec((1, D_MODEL), lambda i, f: (0, 0)),
                  pl.BlockSpec((D_MODEL, tf), lambda i, f: (0, f)),
                  pl.BlockSpec((D_MODEL, tf), lambda i, f: (0, nf + f)),
                  pl.BlockSpec((tf, D_MODEL), lambda i, f: (f, 0)),
                  pl.BlockSpec((1, D_MODEL), lambda i, f: (0, 0))],
        out_specs=pl.BlockSpec((tm, D_MODEL), lambda i, f: (i, 0)),
        out_shape=jax.ShapeDtypeStruct((m, D_MODEL), F32),
        scratch_shapes=[pltpu.VMEM((tm, D_MODEL), BF16), pltpu.VMEM((tm, D_MODEL), F32)],
        compiler_params=_cparams(("parallel", "arbitrary")),
        name="swiglu",
    )(x, g.reshape(1, D_MODEL), w_gu, w_gu, w_down, g_final.reshape(1, D_MODEL))


def _head_sum(x, e_ref):
    hi, lo = _split(x)
    e = e_ref[...]
    cols = []
    for c in range(x.shape[1] // LANES):
        sl = slice(c * LANES, (c + 1) * LANES)
        cols.append(_dot(hi[:, sl], e) + _dot(lo[:, sl], e))
    return jnp.concatenate(cols, axis=1)


def _sigmoid(x):
    return 1.0 / (1.0 + jnp.exp(-x))


def _store_pairs(ref, val, *lead):
    for p in range(val.shape[1] // LANES):
        ref[(*lead, p)] = val[:, p * LANES:(p + 1) * LANES].astype(ref.dtype)


def _rwkv_pre_kernel(x_ref, xb_ref, xa_ref, g_ref, mu_ref, wr_ref, wk_ref, wv_ref, w1_ref, w2_ref, w0_ref,
                     a1_ref, a2_ref, a0_ref, g1_ref, g2_ref, kk_ref, ka_ref, rk_ref, e_ref,
                     r_out, v_out, kn_out, kd_out, ba_out, lw_out, gate_out, bonus_out, *, tm, nts):
    it = pl.program_id(0) % nts
    g = g_ref[...]
    xn = _rms(x_ref[...], g)
    hb = xb_ref.shape[0]
    before = _rms(xb_ref[...], g)[hb - 1:hb] * (it > 0).astype(F32)
    after = _rms(xa_ref[...], g)[0:1] * (it < nts - 1).astype(F32)
    row = lax.broadcasted_iota(jnp.int32, xn.shape, 0)
    x_prev = jnp.where(row == 0, before, pltpu.roll(xn, 1, 0))
    x_next = jnp.where(row == tm - 1, after, pltpu.roll(xn, tm - 1, 0))
    xx = 0.5 * (x_prev + x_next) - xn
    mu = mu_ref[...]
    mix = lambda i: (xn + xx * mu[i:i + 1]).astype(BF16)

    r = _dot(mix(0), wr_ref[...])
    k_lin = _dot(mix(2), wk_ref[...])
    v = _dot(mix(3), wv_ref[...])
    gate = _dot(_sigmoid(_dot(mix(5), g1_ref[...])).astype(BF16), g2_ref[...])
    kn = k_lin * kk_ref[...]
    kn = kn * lax.rsqrt(jnp.maximum(_head_sum(kn * kn, e_ref), 1e-24))
    tw = jnp.tanh(_dot(mix(1), w1_ref[...])).astype(BF16)
    ta = _dot(mix(4), a1_ref[...]).astype(BF16)
    ka = ka_ref[...]
    kd_sum = jnp.zeros_like(k_lin)
    for d in range(2):
        wl = w0_ref[d:d + 1] + _dot(tw, w2_ref[d])
        soft = jnp.maximum(-wl, 0.0) + jnp.log(1.0 + jnp.exp(-jnp.abs(wl)))
        _store_pairs(lw_out, -jnp.exp(-soft - 0.5), d)
        a = _sigmoid(a0_ref[d:d + 1] + _dot(ta, a2_ref[d]))
        kd = k_lin * (1.0 + (a - 1.0) * ka)
        kd_sum = kd_sum + kd
        _store_pairs(kd_out, kd, d)
        _store_pairs(ba_out, kn * a, d)
    _store_pairs(r_out, r)
    _store_pairs(v_out, v)
    _store_pairs(kn_out, kn)
    gate_out[...] = gate.astype(BF16)
    bonus_out[...] = _head_sum(r * kd_sum * rk_ref[...], e_ref) * v


def _rwkv_pre(x, g, p, batch, seq):
    m = x.shape[0]
    tm = TM_RW
    nts = seq // tm
    hb = 8
    nhb = m // hb
    kern = functools.partial(_rwkv_pre_kernel, tm=tm, nts=nts)
    row = lambda: pl.BlockSpec((tm, D_MODEL), lambda i: (i, 0))
    npair = D_MODEL // LANES
    pairs = lambda: pl.BlockSpec((npair, tm, LANES), lambda i: (0, i, 0))
    pairs2 = lambda: pl.BlockSpec((2, npair, tm, LANES), lambda i: (0, 0, i, 0))
    args = (x, x, x, g.reshape(1, D_MODEL), p['mu'], p['wr'], p['wk'], p['wv'], p['w1'], p['w2'], p['w0'],
            p['a1'], p['a2'], p['a0'], p['g1'], p['g2'], p['kk'], p['ka'], p['rk'], p['e'])
    in_specs = [row(),
                pl.BlockSpec((hb, D_MODEL), lambda i: (jnp.maximum(i * (tm // hb) - 1, 0), 0)),
                pl.BlockSpec((hb, D_MODEL), lambda i: (jnp.minimum((i + 1) * (tm // hb), nhb - 1), 0))]
    in_specs += [_full(a.shape) for a in args[3:]]
    out_shape = (jax.ShapeDtypeStruct((npair, m, LANES), BF16),) * 3 + (
        jax.ShapeDtypeStruct((2, npair, m, LANES), BF16), jax.ShapeDtypeStruct((2, npair, m, LANES), BF16),
        jax.ShapeDtypeStruct((2, npair, m, LANES), F32),
        jax.ShapeDtypeStruct((m, D_MODEL), BF16), jax.ShapeDtypeStruct((m, D_MODEL), F32))
    out_specs = (pairs(), pairs(), pairs(), pairs2(), pairs2(), pairs2(), row(), row())
    return pl.pallas_call(
        kern, grid=(m // tm,), in_specs=in_specs, out_specs=out_specs, out_shape=out_shape,
        compiler_params=_cparams(("parallel",)), name="rwkv_pre",
    )(*args)


def _bd(x, first):
    z = jnp.zeros_like(x)
    return jnp.concatenate([jnp.where(first, x, z), jnp.where(first, z, x)], axis=0)


def _compact(full, first):
    return jnp.where(first, full[:CHUNK], full[CHUNK:])


def _wkv_masks(sgn):
    c = CHUNK
    row = lax.broadcasted_iota(jnp.int32, (c, c), 0)
    col = lax.broadcasted_iota(jnp.int32, (c, c), 1)
    tri = jnp.where((row - col) * sgn >= 0, 1.0, 0.0).astype(BF16)
    prow = lax.broadcasted_iota(jnp.int32, (c, LANES), 0)
    pcol = lax.broadcasted_iota(jnp.int32, (c, LANES), 1) & (c - 1)
    dlt = (prow - pcol) * sgn
    strict = dlt > 0
    same = lambda sh: (prow >> sh) == (pcol >> sh)
    levels = [jnp.where(strict & same(1), 1.0, 0.0)]
    sh = 1
    while (1 << sh) < c:
        levels.append(jnp.where(strict & same(sh + 1) & jnp.logical_not(same(sh)), 1.0, 0.0))
        sh += 1
    return dict(tri=tri, strict=strict, incl=dlt >= 0, eye=jnp.where(prow == pcol, 1.0, 0.0), levels=levels)


def _wkv_local(chains, first):
    c = CHUNK
    for s in chains:
        s['hi'], s['lo'] = _split(s['lw'])
    for s in chains:
        s['cum'] = _dot(s['mk']['tri'], s['hi']) + _dot(s['mk']['tri'], s['lo'])
    for s in chains:
        cum, lw = s['cum'], s['lw']
        total = jnp.sum(lw, axis=0, keepdims=True)
        p_inv = jnp.exp(-cum)
        p_hat = jnp.exp(total - cum)
        s['at'] = (-s['kn'] * jnp.exp(cum - lw)).astype(BF16)
        s['rt'] = (s['r'] * jnp.exp(cum)).astype(BF16)
        s['bt'] = (s['ba'] * p_inv).astype(BF16)
        s['kt'] = (s['kd'] * p_inv).astype(BF16)
        s['bh'] = (s['ba'] * p_hat).astype(BF16)
        s['kh'] = (s['kd'] * p_hat).astype(BF16)
        s['v_bf'] = s['v'].astype(BF16)
        s['p_end'] = jnp.exp(total)
        s['lhs'] = jnp.concatenate([s['at'], s['rt']], axis=0)
    for s in chains:
        s['sb'] = _dot_nt(s['lhs'], _bd(s['bt'], first))
        s['sk'] = _dot_nt(s['lhs'], _bd(s['kt'], first))
    for s in chains:
        mk = s['mk']
        s['a_ab'] = jnp.where(mk['strict'], s['sb'][:c], 0.0)
        s['a_rb'] = jnp.where(mk['incl'], s['sb'][c:], 0.0).astype(BF16)
        s['a_kk'] = jnp.concatenate([jnp.where(mk['strict'], s['sk'][:c], 0.0),
                                     jnp.where(mk['incl'], s['sk'][c:], 0.0)], axis=0).astype(BF16)
        s['inv'] = mk['eye'] + s['a_ab'] * mk['levels'][0]
    for s in chains:
        s['av'] = _dot(s['a_kk'], _bd(s['v_bf'], first))
        s['y0'] = _compact(_dot_tn(s['v_bf'], s['kh']), first)
    for lvl in range(1, len(chains[0]['mk']['levels'])):
        for s in chains:
            s['inv_bf'] = s['inv'].astype(BF16)
            s['inner'] = _dot((s['a_ab'] * s['mk']['levels'][lvl]).astype(BF16), _bd(s['inv_bf'], first))
        for s in chains:
            s['inv'] = s['inv'] + _dot(s['inv_bf'], _bd(s['inner'].astype(BF16), first))
    for s in chains:
        inv_bf = s['inv'].astype(BF16)
        s['w1'] = _dot(inv_bf, _bd(s['at'], first)).astype(BF16)
        s['u0'] = _dot(inv_bf, _bd(s['av'][:c].astype(BF16), first))
        s['o0'] = s['av'][c:]


def _wkv_seq(group, ys, first):
    c = CHUNK
    wrs = [_dot_nt(jnp.concatenate([s['w1'], s['rt']], axis=0), _bd(y.astype(BF16), first))
           for s, y in zip(group, ys)]
    us = [(wr[:c] + s['u0']).astype(BF16) for s, wr in zip(group, wrs)]
    outs = [wr[c:] + _dot(s['a_rb'], _bd(u, first)) + s['o0'] for s, wr, u in zip(group, wrs, us)]
    ys = [s['p_end'] * y + _compact(_dot_tn(u, s['bh']), first) + s['y0'] for s, y, u in zip(group, ys, us)]
    return outs, ys


def _wkv_kernel(rf, vf, nf, kdf, baf, lwf, rb, vb, nb, kdb, bab, lwb, of_ref, ob_ref, state, *, ts, npairs):
    first = lax.broadcasted_iota(jnp.int32, (CHUNK, LANES), 1) < RWKV_HEAD

    @pl.when(pl.program_id(2) == 0)
    def _():
        state[...] = jnp.zeros_like(state)

    nchunk = ts // CHUNK
    names = ('r', 'v', 'kn', 'kd', 'ba', 'lw')
    dirs = ((_wkv_masks(1), (rf, vf, nf, kdf, baf, lwf), of_ref, list(range(nchunk))),
            (_wkv_masks(-1), (rb, vb, nb, kdb, bab, lwb), ob_ref, list(range(nchunk - 1, -1, -1))))
    scans = [(d, p) for d in range(2) for p in range(npairs)]
    steps = []
    for k in range(nchunk):
        row = []
        for d, p in scans:
            mk, ins, _, order = dirs[d]
            sl = pl.ds(order[k] * CHUNK, CHUNK)
            chain = {n: ref[p, sl, :].astype(F32) for n, ref in zip(names, ins)}
            chain.update(mk=mk, sl=sl)
            row.append(chain)
        steps.append(row)
    _wkv_local([s for row in steps for s in row], first)
    ys = [state[d, p] for d, p in scans]
    for row in steps:
        outs, ys = _wkv_seq(row, ys, first)
        for (d, p), s, out in zip(scans, row, outs):
            dirs[d][2][p, s['sl'], :] = out
    for (d, p), y in zip(scans, ys):
        state[d, p] = y


def _wkv(r, v, kn, kd, ba, lw, batch, seq):
    ts, npairs = TS_WKV, PAIRS_WKV
    nt = seq // ts
    npair = D_MODEL // LANES
    m = batch * seq
    fwd = lambda b, t: b * nt + t
    bwd = lambda b, t: b * nt + nt - 1 - t
    s3 = lambda at: pl.BlockSpec((npairs, ts, LANES), lambda b, g, t: (g, at(b, t), 0))
    s4 = lambda d, at: pl.BlockSpec((None, npairs, ts, LANES), lambda b, g, t: (d, g, at(b, t), 0))
    return pl.pallas_call(
        functools.partial(_wkv_kernel, ts=ts, npairs=npairs),
        grid=(batch, npair // npairs, nt),
        in_specs=[s3(fwd), s3(fwd), s3(fwd), s4(0, fwd), s4(0, fwd), s4(0, fwd),
                  s3(bwd), s3(bwd), s3(bwd), s4(1, bwd), s4(1, bwd), s4(1, bwd)],
        out_specs=(s3(fwd), s3(bwd)),
        out_shape=(jax.ShapeDtypeStruct((npair, m, LANES), F32),) * 2,
        scratch_shapes=[pltpu.VMEM((2, npairs, CHUNK, LANES), F32)],
        compiler_params=_cparams(("parallel", "parallel", "arbitrary")),
        name="wkv",
    )(r, v, kn, kd, ba, lw, r, v, kn, kd, ba, lw)


def _rwkv_post_kernel(of_ref, ob_ref, bonus_ref, gate_ref, lw_ref, lb_ref, e_ref, wo_ref, x_ref, out_ref):
    y = jnp.concatenate([of_ref[p] + ob_ref[p] for p in range(of_ref.shape[0])], axis=1)
    mean =_head_sum(y, e_ref) * (1.0 / RWKV_HEAD)
    yc = y - mean
    var = _head_sum(yc * yc, e_ref) * (1.0 / RWKV_HEAD)
    yn = yc * lax.rsqrt(var + GN_EPS)
    y2 = yn * lw_ref[...] + lb_ref[...] + bonus_ref[...]
    out_ref[...] = x_ref[...] + _dot((y2 * gate_ref[...].astype(F32)).astype(BF16), wo_ref[...])


def _rwkv_post(x, o_f, o_b, bonus, gate, p):
    m = x.shape[0]
    tm = TM_RW
    row = lambda: pl.BlockSpec((tm, D_MODEL), lambda i: (i, 0))
    pairs = lambda: pl.BlockSpec((D_MODEL // LANES, tm, LANES), lambda i: (0, i, 0))
    return pl.pallas_call(
        _rwkv_post_kernel,
        grid=(m // tm,),
        in_specs=[pairs(), pairs(), row(), row(), _full((1, D_MODEL)), _full((1, D_MODEL)),
                  _full((LANES, LANES)), _full((D_MODEL, D_MODEL)), row()],
        out_specs=row(),
        out_shape=jax.ShapeDtypeStruct((m, D_MODEL), F32),
        compiler_params=_cparams(("parallel",)),
        name="rwkv_post",
    )(o_f, o_b, bonus, gate, p['lnx_w'], p['lnx_b'], p['e'], p['wo'], x)


def _prep_odd(o, rw_mu, rw_wr, rw_wk, rw_wv, rw_wo, rw_w0, rw_w1, rw_w2, rw_a0, rw_a1, rw_a2, rw_g1, rw_g2,
              rw_kk, rw_ka, rw_rk, rw_lnx_w, rw_lnx_b):
    bf = lambda a: a.astype(BF16)

    def pad_dir(w):
        z = jnp.zeros_like(w[0])
        return jnp.stack([jnp.concatenate([w[0], z], axis=0), jnp.concatenate([z, w[1]], axis=0)])

    lane = jnp.arange(LANES) // RWKV_HEAD
    return dict(
        mu=rw_mu[o], wr=bf(rw_wr[o]), wk=bf(rw_wk[o]), wv=bf(rw_wv[o]), wo=bf(rw_wo[o]),
        w0=rw_w0[o], w1=bf(jnp.concatenate([rw_w1[o, 0], rw_w1[o, 1]], axis=1)), w2=bf(pad_dir(rw_w2[o])),
        a0=rw_a0[o], a1=bf(jnp.concatenate([rw_a1[o, 0], rw_a1[o, 1]], axis=1)), a2=bf(pad_dir(rw_a2[o])),
        g1=bf(rw_g1[o]), g2=bf(rw_g2[o]),
        kk=rw_kk[o].reshape(1, D_MODEL), ka=rw_ka[o].reshape(1, D_MODEL), rk=rw_rk[o].reshape(1, D_MODEL),
        lnx_w=rw_lnx_w[o].reshape(1, D_MODEL), lnx_b=rw_lnx_b[o].reshape(1, D_MODEL),
        e=(lane[:, None] == lane[None, :]).astype(BF16))


def _even_layer(x, g, w_in, w_out, conv_w, cos, sin, batch, seq):
    z1, z4, z16 = _qkv_proj(x, g, w_in[:, :3 * A_WIDTH], cos, sin, batch, seq)
    zc = _norm_matmul(x, g, w_in[:, 3 * A_WIDTH:], TM, 3 * B_WIDTH, BF16)
    m = x.shape[0]
    o1, l1 = _dilated_branch(z1.reshape(3, batch, seq, A_WIDTH), seq)
    o4, l4 = _dilated_branch(z4.reshape(3, batch * 4, seq // 4, A_WIDTH), seq // 4)
    o16, l16 = _dilated_branch(z16.reshape(3, batch * 16, seq // 16, A_WIDTH), seq // 16)
    o1, l1 = o1.reshape(m, A_WIDTH), l1.reshape(m, A_WIDTH)
    d4 = lambda a: a.reshape(batch, 4, seq // 4, A_WIDTH)
    d16 = lambda a: a.reshape(batch, 16, seq // 16, A_WIDTH)
    return _even_out(x, o1, l1, d4(o4), d4(l4), d16(o16), d16(l16), zc, conv_w, w_out, batch, seq)


def _odd_layer(x, g, p, batch, seq):
    r, v, kn, kd, ba, lw, gate, bonus = _rwkv_pre(x, g, p, batch, seq)
    o_f, o_b = _wkv(r, v, kn, kd, ba, lw, batch, seq)
    return _rwkv_post(x, o_f, o_b, bonus, gate, p)


def _trunk(x, mem, w):
    batch, seq, _ = x.shape
    n_mem = mem.shape[1]
    x = x.reshape(batch * seq, D_MODEL)
    mem = mem.reshape(batch * n_mem, D_MODEL)
    cos, sin = _rope_tables(seq)
    for l in range(DEPTH):
        if l % 2 == 0:
            e = l // 2
            x = _even_layer(x, w['norm_mix'][l], w['ab_w_in'][e], w['ab_w_out'][e], w['ab_conv'][e],
                            cos, sin, batch, seq)
        else:
            x = _odd_layer(x, w['norm_mix'][l], w['odd'][l // 2], batch, seq)
        kv = _norm_matmul(mem, w['norm_mem'][l], w['ca_wkv'][l], n_mem, D_MODEL, BF16)
        x = _cross_attn(x, w['norm_cross'][l], kv, w['ca_wq'][l], w['ca_wo'][l], batch, seq, n_mem)
        x = _ffn(x, w['norm_ffn'][l], w['ffn_wgu'][l], w['ffn_wdown'][l], w['norm_final'], l == DEPTH - 1)
    return x.reshape(batch, seq, D_MODEL)


def kernel(x_prompt, x_sample, mem_prompt, mem_sample, norm_mix, norm_cross, norm_mem, norm_ffn, norm_final,
           ab_w_in, ab_w_out, ab_conv, rw_mu, rw_wr, rw_wk, rw_wv, rw_wo, rw_w0, rw_w1, rw_w2, rw_a0, rw_a1,
           rw_a2, rw_g1, rw_g2, rw_kk, rw_ka, rw_rk, rw_lnx_w, rw_lnx_b, ca_wq, ca_wkv, ca_wo, ffn_wgu,
           ffn_wdown):
    bf = lambda a: a.astype(BF16)
    w = dict(norm_mix=norm_mix, norm_cross=norm_cross, norm_mem=norm_mem, norm_ffn=norm_ffn,
             norm_final=norm_final, ab_w_in=bf(ab_w_in), ab_w_out=bf(ab_w_out), ab_conv=ab_conv,
             ca_wq=bf(ca_wq), ca_wkv=bf(ca_wkv), ca_wo=bf(ca_wo), ffn_wgu=bf(ffn_wgu), ffn_wdown=bf(ffn_wdown),
             odd=[_prep_odd(o, rw_mu, rw_wr, rw_wk, rw_wv, rw_wo, rw_w0, rw_w1, rw_w2, rw_a0, rw_a1, rw_a2,
                            rw_g1, rw_g2, rw_kk, rw_ka, rw_rk, rw_lnx_w, rw_lnx_b)
                  for o in range(rw_mu.shape[0])])
    return _trunk(x_prompt, mem_prompt, w), _trunk(x_sample, mem_sample, w)
```

```python
import functools

import jax
import jax.numpy as jnp
from jax import lax
from jax.experimental import pallas as pl
from jax.experimental.pallas import tpu as pltpu

F32 = jnp.float32
BF16 = jnp.bfloat16

D_MODEL = 1024
DEPTH = 4
HEAD_DIM = 64
A_WIDTH = 512
B_WIDTH = 512
DILATIONS = (1, 4, 16)
BAND = 64
ROPE_THETA = 500000.0
ROPE_DIM = 16
RWKV_HEAD = 64
GN_EPS = 64e-5
DECAY_SCALE = 0.6065306597126334
CA_HEADS = 4
CA_HEAD_DIM = 256
D_FF = 2816
RMS_EPS = 1e-6
NEG_INF = -1e30
CHUNK = 64
LANES = 128
MXU_WIDTH = 256

VMEM_LIMIT = 48 * 1024 * 1024

TM = 512
TM_RW = 256
TB_DIL = 512
TQ_DIL = 128
TF_FFN = 1408
TS_WKV = 512
PAIRS_WKV = 4


def _cparams(sem):
    return pltpu.CompilerParams(dimension_semantics=sem, vmem_limit_bytes=VMEM_LIMIT)


def _full(shape):
    n = len(shape)
    return pl.BlockSpec(shape, lambda *_: (0,) * n)


def _rms(x, g):
    ms = jnp.mean(x * x, axis=-1, keepdims=True)
    return x * lax.rsqrt(ms + RMS_EPS) * g


def _dot(a, b):
    return jnp.dot(a, b, preferred_element_type=F32)


def _dot_nt(a, b):
    return lax.dot_general(a, b, (((1,), (1,)), ((), ())), preferred_element_type=F32)


def _dot_tn(a, b):
    return lax.dot_general(a, b, (((0,), (0,)), ((), ())), preferred_element_type=F32)


def _split(x):
    hi = x.astype(BF16)
    lo = (x - hi.astype(F32)).astype(BF16)
    return hi, lo


def _norm_matmul_kernel(x_ref, g_ref, w_ref, o_ref, xn_ref):
    @pl.when(pl.program_id(1) == 0)
    def _():
        xn_ref[...] = _rms(x_ref[...], g_ref[...]).astype(BF16)

    o_ref[...] = _dot(xn_ref[...], w_ref[...]).astype(o_ref.dtype)


def _norm_matmul(x, g, w, tm, tn, out_dtype):
    m, k = x.shape
    n = w.shape[1]
    return pl.pallas_call(
        _norm_matmul_kernel,
        grid=(m // tm, n // tn),
        in_specs=[pl.BlockSpec((tm, k), lambda i, j: (i, 0)),
                  pl.BlockSpec((1, k), lambda i, j: (0, 0)),
                  pl.BlockSpec((k, tn), lambda i, j: (0, j))],
        out_specs=pl.BlockSpec((tm, tn), lambda i, j: (i, j)),
        out_shape=jax.ShapeDtypeStruct((m, n), out_dtype),
        scratch_shapes=[pltpu.VMEM((tm, k), BF16)],
        compiler_params=_cparams(("parallel", "arbitrary")),
        name="norm_matmul",
    )(x, g.reshape(1, k), w)


def _even_proj_kernel(x_ref, g_ref, w_ref, cos_ref, sin_ref, p4_ref, p16_ref,
                      o1_ref, o4_ref, o16_ref, oc_ref, *, tm):
    xn = _rms(x_ref[...], g_ref[...]).astype(BF16)
    reps = A_WIDTH // LANES
    cos = jnp.concatenate([cos_ref[...]] * reps, axis=1)
    sin = jnp.concatenate([sin_ref[...]] * reps, axis=1)
    lane = lax.broadcasted_iota(jnp.int32, (tm, A_WIDTH), 1) & (HEAD_DIM - 1)
    half = ROPE_DIM // 2
    for j in range(3):
        a = _dot(xn, w_ref[:, j * A_WIDTH:(j + 1) * A_WIDTH])
        if j < 2:
            partner = jnp.where(lane < half, pltpu.roll(a, A_WIDTH - half, 1), pltpu.roll(a, half, 1))
            a = a * cos + partner * sin
            if j == 0:
                a = a * (HEAD_DIM ** -0.5)
        a = a.astype(BF16)
        o1_ref[j] = a
        for d, p_ref, o_ref in ((4, p4_ref, o4_ref), (16, p16_ref, o16_ref)):
            perm = _dot(p_ref[...], a).astype(BF16)
            for r in range(d):
                o_ref[j, r] = perm[r * (tm // d):(r + 1) * (tm // d)]
    oc_ref[...] = _dot(xn, w_ref[:, 3 * A_WIDTH:]).astype(BF16)


def _dedilate_perm(tm, d):
    i = jnp.arange(tm)
    src = (i % (tm // d)) * d + i // (tm // d)
    return (src[:, None] == jnp.arange(tm)[None, :]).astype(BF16)


def _even_proj(x, g, w_in, cos, sin, batch, seq):
    m = x.shape[0]
    tm = TM
    nts = seq // tm
    kern = functools.partial(_even_proj_kernel, tm=tm)
    out_shape = (jax.ShapeDtypeStruct((3, m, A_WIDTH), BF16),
                 jax.ShapeDtypeStruct((3, batch, 4, seq // 4, A_WIDTH), BF16),
                 jax.ShapeDtypeStruct((3, batch, 16, seq // 16, A_WIDTH), BF16),
                 jax.ShapeDtypeStruct((m, 3 * B_WIDTH), BF16))
    return pl.pallas_call(
        kern,
        grid=(m // tm,),
        in_specs=[pl.BlockSpec((tm, D_MODEL), lambda i: (i, 0)),
                  _full((1, D_MODEL)),
                  _full(w_in.shape),
                  pl.BlockSpec((tm, LANES), lambda i: (i % nts, 0)),
                  pl.BlockSpec((tm, LANES), lambda i: (i % nts, 0)),
                  _full((tm, tm)), _full((tm, tm))],
        out_specs=(pl.BlockSpec((3, tm, A_WIDTH), lambda i: (0, i, 0)),
                   pl.BlockSpec((3, None, 4, tm // 4, A_WIDTH), lambda i: (0, i // nts, 0, i % nts, 0)),
                   pl.BlockSpec((3, None, 16, tm // 16, A_WIDTH), lambda i: (0, i // nts, 0, i % nts, 0)),
                   pl.BlockSpec((tm, 3 * B_WIDTH), lambda i: (i, 0))),
        out_shape=out_shape,
        compiler_params=_cparams(("parallel",)),
        name="even_proj",
    )(x, g.reshape(1, D_MODEL), w_in, cos, sin, _dedilate_perm(tm, 4), _dedilate_perm(tm, 16))


def _rope_tables(seq):
    half = ROPE_DIM // 2
    inv = ROPE_THETA ** (-2.0 * jnp.arange(half, dtype=F32) / ROPE_DIM)
    ang = jnp.arange(seq, dtype=F32)[:, None] * inv[None, :]
    cos, sin = jnp.cos(ang), jnp.sin(ang)
    rest = HEAD_DIM - ROPE_DIM
    cos_h = jnp.concatenate([cos, cos, jnp.ones((seq, rest), F32)], axis=1)
    sin_h = jnp.concatenate([-sin, sin, jnp.zeros((seq, rest), F32)], axis=1)
    reps = LANES // HEAD_DIM
    return jnp.tile(cos_h, (1, reps)), jnp.tile(sin_h, (1, reps))


def _dil_kernel(q_ref, kp_ref, kc_ref, kn_ref, vp_ref, vc_ref, vn_ref, o_ref, lse_ref, *, tb, tq, length):
    qi = pl.program_id(1)
    span = tq + 2 * BAND
    k = jnp.concatenate([kp_ref[...], kc_ref[...], kn_ref[...]], axis=0)
    v = jnp.concatenate([vp_ref[...], vc_ref[...], vn_ref[...]], axis=0)
    row = lax.broadcasted_iota(jnp.int32, (tq, span), 0)
    col = lax.broadcasted_iota(jnp.int32, (tq, span), 1)
    rel = col - BAND - row
    band = (rel <= BAND) & (rel >= -BAND)
    first = lax.broadcasted_iota(jnp.int32, (tq, LANES), 1) < HEAD_DIM
    for s_idx in range(tb // tq):
        rows = slice(s_idx * tq, (s_idx + 1) * tq)
        keys = slice(s_idx * tq, s_idx * tq + span)
        key_pos = qi * tb + s_idx * tq - BAND + col
        valid = band & (key_pos >= 0) & (key_pos < length)
        for p in range(A_WIDTH // LANES):
            sl = slice(p * LANES, (p + 1) * LANES)
            qp, kp, vp = q_ref[rows, sl], k[keys, sl], v[keys, sl]
            outs, lses = [], []
            for sel in (first, jnp.logical_not(first)):
                s = _dot_nt(jnp.where(sel, qp, jnp.zeros_like(qp)), kp)
                s = jnp.where(valid, s, NEG_INF)
                mx = jnp.max(s, axis=-1, keepdims=True)
                e = jnp.exp(s - mx)
                den = jnp.sum(e, axis=-1, keepdims=True)
                outs.append(_dot(e.astype(BF16), vp) * (1.0 / den))
                lses.append(mx + jnp.log(den))
            o_ref[rows, sl] = jnp.where(first, outs[0], outs[1]).astype(BF16)
            lse_ref[rows, sl] = jnp.where(first, lses[0], lses[1])


def _dilated_branch(zd, length):
    g = zd.shape[1]
    tb = min(TB_DIL, length)
    tq = min(TQ_DIL, length)
    nb = length // BAND
    r = tb // BAND
    kern = functools.partial(_dil_kernel, tb=tb, tq=tq, length=length)

    def cur(which):
        return pl.BlockSpec((None, None, tb, A_WIDTH), lambda b, i: (which, b, i, 0))

    def prev(which):
        return pl.BlockSpec((None, None, BAND, A_WIDTH), lambda b, i: (which, b, jnp.maximum(i * r - 1, 0), 0))

    def nxt(which):
        return pl.BlockSpec((None, None, BAND, A_WIDTH), lambda b, i: (which, b, jnp.minimum((i + 1) * r, nb - 1), 0))

    return pl.pallas_call(
        kern,
        grid=(g, length // tb),
        in_specs=[cur(0), prev(1), cur(1), nxt(1), prev(2), cur(2), nxt(2)],
        out_specs=(pl.BlockSpec((None, tb, A_WIDTH), lambda b, i: (b, i, 0)),
                   pl.BlockSpec((None, tb, A_WIDTH), lambda b, i: (b, i, 0))),
        out_shape=(jax.ShapeDtypeStruct((g, length, A_WIDTH), BF16),
                   jax.ShapeDtypeStruct((g, length, A_WIDTH), F32)),
        compiler_params=_cparams(("parallel", "parallel")),
        name="dilated_attn",
    )(zd, zd, zd, zd, zd, zd, zd)


def _even_out_kernel(o1_ref, l1_ref, o4_ref, l4_ref, o16_ref, l16_ref, bg_ref, cg_ref, h_ref,
                     cgp_ref, hp_ref, cgn_ref, hn_ref, cw_ref, wa_ref, wb_ref, x_ref, out_ref,
                     s4o, s4l, s16o, s16l, *, tm, nts):
    it = pl.program_id(0) % nts
    nslab = A_WIDTH // LANES
    for c in range(nslab):
        sl = slice(c * LANES, (c + 1) * LANES)
        for r in range(4):
            s4o[c, pl.ds(r, tm // 4, stride=4), :] = o4_ref[r, :, sl].astype(F32)
            s4l[c, pl.ds(r, tm // 4, stride=4), :] = l4_ref[r, :, sl]
        for r in range(16):
            s16o[c, pl.ds(r, tm // 16, stride=16), :] = o16_ref[r, :, sl].astype(F32)
            s16l[c, pl.ds(r, tm // 16, stride=16), :] = l16_ref[r, :, sl]
    wide = lambda ref: jnp.concatenate([ref[c] for c in range(nslab)], axis=1)
    l1, l4, l16 = l1_ref[...], wide(s4l), wide(s16l)
    mx = jnp.maximum(jnp.maximum(l1, l4), l16)
    e1, e4, e16 = jnp.exp(l1 - mx), jnp.exp(l4 - mx), jnp.exp(l16 - mx)
    ya = (e1 * o1_ref[...].astype(F32) + e4 * wide(s4o) + e16 * wide(s16o)) * (1.0 / (e1 + e4 + e16))

    u = cg_ref[...].astype(F32) * h_ref[...].astype(F32)
    last = cgp_ref.shape[0] - 1
    u_before = (cgp_ref[...].astype(F32) * hp_ref[...].astype(F32))[last:last + 1]
    u_after = (cgn_ref[...].astype(F32) * hn_ref[...].astype(F32))[0:1]
    u_before = u_before * (it > 0).astype(F32)
    u_after = u_after * (it < nts - 1).astype(F32)
    row = lax.broadcasted_iota(jnp.int32, u.shape, 0)
    u_prev = jnp.where(row == 0, u_before, pltpu.roll(u, 1, 0))
    u_next = jnp.where(row == tm - 1, u_after, pltpu.roll(u, tm - 1, 0))
    cw = cw_ref[...]
    yb = bg_ref[...].astype(F32) * (cw[0:1] * u_prev + cw[1:2] * u + cw[2:3] * u_next)

    out_ref[...] = x_ref[...] + _dot(ya.astype(BF16), wa_ref[...]) + _dot(yb.astype(BF16), wb_ref[...])


def _even_out(x, o1, l1, o4, l4, o16, l16, zc, conv_w, w_out, batch, seq):
    m = x.shape[0]
    tm = TM
    nts = seq // tm
    hb = 16
    nhb = m // hb
    kern = functools.partial(_even_out_kernel, tm=tm, nts=nts)
    nat = lambda: pl.BlockSpec((tm, A_WIDTH), lambda i: (i, 0))
    dil = lambda d: pl.BlockSpec((None, d, tm // d, A_WIDTH), lambda i: (i // nts, 0, i % nts, 0))
    col = lambda c: pl.BlockSpec((tm, B_WIDTH), lambda i: (i, c))
    before = lambda c: pl.BlockSpec((hb, B_WIDTH), lambda i: (jnp.maximum(i * (tm // hb) - 1, 0), c))
    after = lambda c: pl.BlockSpec((hb, B_WIDTH), lambda i: (jnp.minimum((i + 1) * (tm // hb), nhb - 1), c))
    return pl.pallas_call(
        kern,
        grid=(m // tm,),
        in_specs=[nat(), nat(), dil(4), dil(4), dil(16), dil(16),
                  col(0), col(1), col(2), before(1), before(2), after(1), after(2),
                  _full((3, B_WIDTH)),
                  pl.BlockSpec((A_WIDTH, D_MODEL), lambda i: (0, 0)),
                  pl.BlockSpec((B_WIDTH, D_MODEL), lambda i: (1, 0)),
                  pl.BlockSpec((tm, D_MODEL), lambda i: (i, 0))],
        out_specs=pl.BlockSpec((tm, D_MODEL), lambda i: (i, 0)),
        out_shape=jax.ShapeDtypeStruct((m, D_MODEL), F32),
        scratch_shapes=[pltpu.VMEM((A_WIDTH // LANES, tm, LANES), F32)] * 4,
        compiler_params=_cparams(("parallel",)),
        name="even_out",
    )(o1, l1, o4, l4, o16, l16, zc, zc, zc, zc, zc, zc, zc, conv_w, w_out, w_out, x)


def _cross_kernel(x_ref, g_ref, k_ref, v_ref, wq_ref, wo_ref, out_ref):
    x = x_ref[...]
    xn = _rms(x, g_ref[...]).astype(BF16)
    q = (_dot(xn, wq_ref[...]) * (CA_HEAD_DIM ** -0.5)).astype(BF16)
    k = k_ref[...]
    v = v_ref[...]
    outs = []
    for h in range(CA_HEADS):
        sl = slice(h * CA_HEAD_DIM, (h + 1) * CA_HEAD_DIM)
        s = _dot_nt(q[:, sl], k[:, sl])
        mx = jnp.max(s, axis=-1, keepdims=True)
        e = jnp.exp(s - mx)
        den = jnp.sum(e, axis=-1, keepdims=True)
        outs.append(_dot((e * (1.0 / den)).astype(BF16), v[:, sl]).astype(BF16))
    o = jnp.concatenate(outs, axis=1)
    out_ref[...] = x + _dot(o, wo_ref[...])


def _cross_attn(x, g, kv, wq, wo, batch, seq, n_mem):
    m = x.shape[0]
    tq = TM
    nts = seq // tq
    return pl.pallas_call(
        _cross_kernel,
        grid=(m // tq,),
        in_specs=[pl.BlockSpec((tq, D_MODEL), lambda i: (i, 0)),
                  _full((1, D_MODEL)),
                  pl.BlockSpec((n_mem, D_MODEL), lambda i: (i // nts, 0)),
                  pl.BlockSpec((n_mem, D_MODEL), lambda i: (i // nts, 1)),
                  _full((D_MODEL, D_MODEL)),
                  _full((D_MODEL, D_MODEL))],
        out_specs=pl.BlockSpec((tq, D_MODEL), lambda i: (i, 0)),
        out_shape=jax.ShapeDtypeStruct((m, D_MODEL), F32),
        compiler_params=_cparams(("parallel",)),
        name="cross_attn",
    )(x, g.reshape(1, D_MODEL), kv, kv, wq, wo)


def _ffn_kernel(x_ref, g_ref, wg_ref, wu_ref, wd_ref, gf_ref, out_ref, xn_ref, acc_ref, *, final_norm):
    f = pl.program_id(1)

    @pl.when(f == 0)
    def _():
        xn_ref[...] = _rms(x_ref[...], g_ref[...]).astype(BF16)
        acc_ref[...] = x_ref[...]

    xn = xn_ref[...]
    gate = _dot(xn, wg_ref[...])
    up = _dot(xn, wu_ref[...])
    act = (gate * (1.0 / (1.0 + jnp.exp(-gate))) * up).astype(BF16)
    acc_ref[...] += _dot(act, wd_ref[...])

    @pl.when(f == pl.num_programs(1) - 1)
    def _():
        y = acc_ref[...]
        if final_norm:
            y = _rms(y, gf_ref[...])
        out_ref[...] = y


def _ffn(x, g, w_gu, w_down, g_final, final_norm):
    m = x.shape[0]
    tm, tf = TM, TF_FFN
    nf = D_FF // tf
    kern = functools.partial(_ffn_kernel, final_norm=final_norm)
    return pl.pallas_call(
        kern,
        grid=(m // tm, nf),
        in_specs=[pl.BlockSpec((tm, D_MODEL), lambda i, f: (i, 0)),
                  pl.BlockSpec((1, D_MODEL), lambda i, f: (0, 0)),
                  pl.BlockSpec((D_MODEL, tf), lambda i, f: (0, f)),
                  pl.BlockSpec((D_MODEL, tf), lambda i, f: (0, nf + f)),
                  pl.BlockSpec((tf, D_MODEL), lambda i, f: (f, 0)),
                  pl.BlockSpec((1, D_MODEL), lambda i, f: (0, 0))],
        out_specs=pl.BlockSpec((tm, D_MODEL), lambda i, f: (i, 0)),
        out_shape=jax.ShapeDtypeStruct((m, D_MODEL), F32),
        scratch_shapes=[pltpu.VMEM((tm, D_MODEL), BF16), pltpu.VMEM((tm, D_MODEL), F32)],
        compiler_params=_cparams(("parallel", "arbitrary")),
        name="swiglu",
    )(x, g.reshape(1, D_MODEL), w_gu, w_gu, w_down, g_final.reshape(1, D_MODEL))


def _head_sum(x, e_ref, split=True):
    hi, lo = _split(x) if split else (x.astype(BF16), None)
    e = e_ref[...]
    cols = []
    width = e.shape[0]
    for c in range(x.shape[1] // width):
        sl = slice(c * width, (c + 1) * width)
        cols.append(_dot(hi[:, sl], e) + _dot(lo[:, sl], e) if split else _dot(hi[:, sl], e))
    return jnp.concatenate(cols, axis=1)


def _sigmoid(x):
    return 0.5 * jnp.tanh(0.5 * x) + 0.5


def _store_pairs(ref, val, *lead):
    for p in range(val.shape[1] // LANES):
        ref[(*lead, p)] = val[:, p * LANES:(p + 1) * LANES].astype(ref.dtype)


def _rwkv_pre_kernel(x_ref, xb_ref, xa_ref, g_ref, mu_ref, wr_ref, wk_ref, wv_ref, w1_ref, w2_ref, w0_ref,
                     a1_ref, a2_ref, a0_ref, g1_ref, g2_ref, kk_ref, ka_ref, rk_ref, e_ref,
                     sm_ref, r_out, v_out, kn_out, kd_out, ba_out, lw_out, gate_out, bonus_out, *, nts):
    it = pl.program_id(0) % nts
    g = g_ref[...]
    xn_bf = _rms(x_ref[...], g).astype(BF16)
    before = (_rms(xb_ref[...], g) * (it > 0).astype(F32)).astype(BF16)
    after = (_rms(xa_ref[...], g) * (it < nts - 1).astype(F32)).astype(BF16)
    xx_bf = _dot(sm_ref[...], jnp.concatenate([before, xn_bf, after], axis=0)).astype(BF16)
    mu = mu_ref[...].astype(BF16)
    mix = lambda i: xn_bf + xx_bf * mu[i:i + 1]

    r = _dot(mix(0), wr_ref[...])
    k_lin = _dot(mix(2), wk_ref[...])
    v = _dot(mix(3), wv_ref[...])
    gate = _dot(_sigmoid(_dot(mix(5), g1_ref[...])).astype(BF16), g2_ref[...])
    kn = k_lin * kk_ref[...]
    kn = kn * lax.rsqrt(jnp.maximum(_head_sum(kn * kn, e_ref), 1e-24))
    tw = jnp.tanh(_dot(mix(1), w1_ref[...])).astype(BF16)
    ta = _dot(mix(4), a1_ref[...]).astype(BF16)
    ka = ka_ref[...]
    kd_sum = jnp.zeros_like(k_lin)
    for d in range(2):
        wl = w0_ref[d:d + 1] + _dot(tw, w2_ref[d])
        _store_pairs(lw_out, -DECAY_SCALE * _sigmoid(wl), d)
        a = _sigmoid(a0_ref[d:d + 1] + _dot(ta, a2_ref[d]))
        kd = k_lin * (1.0 + (a - 1.0) * ka)
        kd_sum = kd_sum + kd
        _store_pairs(kd_out, kd, d)
        _store_pairs(ba_out, kn * a, d)
    _store_pairs(r_out, r)
    _store_pairs(v_out, v)
    _store_pairs(kn_out, kn)
    gate_out[...] = gate.astype(BF16)
    bonus_out[...] = _head_sum(r * kd_sum * rk_ref[...], e_ref, split=False) * v


def _rwkv_pre(x, g, p, batch, seq):
    m = x.shape[0]
    tm = TM_RW
    nts = seq // tm
    hb = 16
    nhb = m // hb
    kern = functools.partial(_rwkv_pre_kernel, nts=nts)
    row = lambda: pl.BlockSpec((tm, D_MODEL), lambda i: (i, 0))
    npair = D_MODEL // LANES
    pairs = lambda: pl.BlockSpec((npair, tm, LANES), lambda i: (0, i, 0))
    pairs2 = lambda: pl.BlockSpec((2, npair, tm, LANES), lambda i: (0, 0, i, 0))
    t = jnp.arange(tm)[:, None]
    c = jnp.arange(tm + 2 * hb)[None, :] - hb
    shift = (jnp.where(c == t, -1.0, 0.0) + jnp.where(jnp.abs(c - t) == 1, 0.5, 0.0)).astype(BF16)
    args = (x, x, x, g.reshape(1, D_MODEL), p['mu'], p['wr'], p['wk'], p['wv'], p['w1'], p['w2'], p['w0'],
            p['a1'], p['a2'], p['a0'], p['g1'], p['g2'], p['kk'], p['ka'], p['rk'], p['e'], shift)
    in_specs = [row(),
                pl.BlockSpec((hb, D_MODEL), lambda i: (jnp.maximum(i * (tm // hb) - 1, 0), 0)),
                pl.BlockSpec((hb, D_MODEL), lambda i: (jnp.minimum((i + 1) * (tm // hb), nhb - 1), 0))]
    in_specs += [_full(a.shape) for a in args[3:]]
    out_shape = (jax.ShapeDtypeStruct((npair, m, LANES), BF16),) * 3 + (
        jax.ShapeDtypeStruct((2, npair, m, LANES), BF16), jax.ShapeDtypeStruct((2, npair, m, LANES), BF16),
        jax.ShapeDtypeStruct((2, npair, m, LANES), F32),
        jax.ShapeDtypeStruct((m, D_MODEL), BF16), jax.ShapeDtypeStruct((m, D_MODEL), F32))
    out_specs = (pairs(), pairs(), pairs(), pairs2(), pairs2(), pairs2(), row(), row())
    return pl.pallas_call(
        kern, grid=(m // tm,), in_specs=in_specs, out_specs=out_specs, out_shape=out_shape,
        compiler_params=_cparams(("parallel",)), name="rwkv_pre",
    )(*args)


def _bd(x, first):
    z = jnp.zeros_like(x)
    return jnp.concatenate([jnp.where(first, x, z), jnp.where(first, z, x)], axis=0)


def _compact(full, first):
    return jnp.where(first, full[:CHUNK], full[CHUNK:])


def _wkv_masks(sgn):
    c = CHUNK
    row = lax.broadcasted_iota(jnp.int32, (c, c), 0)
    col = lax.broadcasted_iota(jnp.int32, (c, c), 1)
    tri = jnp.where((row - col) * sgn >= 0, 1.0, 0.0).astype(BF16)
    prow = lax.broadcasted_iota(jnp.int32, (c, LANES), 0)
    pcol = lax.broadcasted_iota(jnp.int32, (c, LANES), 1) & (c - 1)
    dlt = (prow - pcol) * sgn
    strict = dlt > 0
    same = lambda sh: (prow >> sh) == (pcol >> sh)
    levels = [jnp.where(strict & same(1), 1.0, 0.0)]
    sh = 1
    while (1 << sh) < c:
        levels.append(jnp.where(strict & same(sh + 1) & jnp.logical_not(same(sh)), 1.0, 0.0))
        sh += 1
    return dict(tri=tri, strict=strict, incl=dlt >= 0, eye=jnp.where(prow == pcol, 1.0, 0.0), levels=levels)


def _wkv_local(chains, first):
    c = CHUNK
    for s in chains:
        s['hi'], s['lo'] = _split(s['lw'])
    for s in chains:
        s['cum'] = _dot(s['mk']['tri'], s['hi']) + _dot(s['mk']['tri'], s['lo'])
    for s in chains:
        cum, lw = s['cum'], s['lw']
        total = jnp.sum(lw, axis=0, keepdims=True)
        p_inv = jnp.exp(-cum)
        p_hat = jnp.exp(total - cum)
        s['at'] = (-s['kn'] * jnp.exp(cum - lw)).astype(BF16)
        s['rt'] = (s['r'] * jnp.exp(cum)).astype(BF16)
        s['bt'] = (s['ba'] * p_inv).astype(BF16)
        s['kt'] = (s['kd'] * p_inv).astype(BF16)
        s['bh'] = (s['ba'] * p_hat).astype(BF16)
        s['kh'] = (s['kd'] * p_hat).astype(BF16)
        s['v_bf'] = s['v'].astype(BF16)
        s['p_end'] = jnp.exp(total)
        s['lhs'] = jnp.concatenate([s['at'], s['rt']], axis=0)
    for s in chains:
        s['sbk'] = _dot_nt(s['lhs'], jnp.concatenate([_bd(s['bt'], first), _bd(s['kt'], first)], axis=0))
    for s in chains:
        mk = s['mk']
        sb, sk = s['sbk'][:, :LANES], s['sbk'][:, LANES:]
        s['a_ab'] = jnp.where(mk['strict'], sb[:c], 0.0)
        s['a_rb'] = jnp.where(mk['incl'], sb[c:], 0.0).astype(BF16)
        s['a_kk'] = jnp.concatenate([jnp.where(mk['strict'], sk[:c], 0.0),
                                     jnp.where(mk['incl'], sk[c:], 0.0)], axis=0).astype(BF16)
        s['inv'] = mk['eye'] + s['a_ab'] * mk['levels'][0]
    for s in chains:
        s['av'] = _dot(s['a_kk'], _bd(s['v_bf'], first))
        s['y0'] = _compact(_dot_tn(s['v_bf'], s['kh']), first)
    for lvl in range(1, len(chains[0]['mk']['levels'])):
        for s in chains:
            s['inv_bf'] = s['inv'].astype(BF16)
            s['inner'] = _dot((s['a_ab'] * s['mk']['levels'][lvl]).astype(BF16), _bd(s['inv_bf'], first))
        for s in chains:
            s['inv'] = s['inv'] + _dot(s['inv_bf'], _bd(s['inner'].astype(BF16), first))
    for s in chains:
        rhs = jnp.concatenate([_bd(s['at'], first), _bd(s['av'][:c].astype(BF16), first)], axis=1)
        wu = _dot(s['inv'].astype(BF16), rhs)
        s['w1'] = wu[:, :LANES].astype(BF16)
        s['u0'] = wu[:, LANES:]
        s['o0'] = s['av'][c:]


def _wkv_seq(group, ys, first):
    c = CHUNK
    wrs = [_dot_nt(jnp.concatenate([s['w1'], s['rt']], axis=0), _bd(y.astype(BF16), first))
           for s, y in zip(group, ys)]
    us = [(wr[:c] + s['u0']).astype(BF16) for s, wr in zip(group, wrs)]
    outs = [wr[c:] + _dot(s['a_rb'], _bd(u, first)) + s['o0'] for s, wr, u in zip(group, wrs, us)]
    ys = [s['p_end'] * y + _compact(_dot_tn(u, s['bh']), first) + s['y0'] for s, y, u in zip(group, ys, us)]
    return outs, ys


def _wkv_kernel(rf, vf, nf, kdf, baf, lwf, rb, vb, nb, kdb, bab, lwb, of_ref, ob_ref, state, *, ts, npairs):
    first = lax.broadcasted_iota(jnp.int32, (CHUNK, LANES), 1) < RWKV_HEAD

    @pl.when(pl.program_id(2) == 0)
    def _():
        state[...] = jnp.zeros_like(state)

    nchunk = ts // CHUNK
    names = ('r', 'v', 'kn', 'kd', 'ba', 'lw')
    dirs = ((_wkv_masks(1), (rf, vf, nf, kdf, baf, lwf), of_ref, list(range(nchunk))),
            (_wkv_masks(-1), (rb, vb, nb, kdb, bab, lwb), ob_ref, list(range(nchunk - 1, -1, -1))))
    scans = [(d, p) for d in range(2) for p in range(npairs)]
    steps = []
    for k in range(nchunk):
        row = []
        for d, p in scans:
            mk, ins, _, order = dirs[d]
            sl = pl.ds(order[k] * CHUNK, CHUNK)
            chain = {n: ref[p, sl, :].astype(F32) for n, ref in zip(names, ins)}
            chain.update(mk=mk, sl=sl)
            row.append(chain)
        steps.append(row)
    _wkv_local([s for row in steps for s in row], first)
    ys = [state[d, p] for d, p in scans]
    for row in steps:
        outs, ys = _wkv_seq(row, ys, first)
        for (d, p), s, out in zip(scans, row, outs):
            dirs[d][2][p, s['sl'], :] = out
    for (d, p), y in zip(scans, ys):
        state[d, p] = y


def _wkv(r, v, kn, kd, ba, lw, batch, seq):
    ts, npairs = TS_WKV, PAIRS_WKV
    nt = seq // ts
    npair = D_MODEL // LANES
    m = batch * seq
    fwd = lambda b, t: b * nt + t
    bwd = lambda b, t: b * nt + nt - 1 - t
    s3 = lambda at: pl.BlockSpec((npairs, ts, LANES), lambda b, g, t: (g, at(b, t), 0))
    s4 = lambda d, at: pl.BlockSpec((None, npairs, ts, LANES), lambda b, g, t: (d, g, at(b, t), 0))
    return pl.pallas_call(
        functools.partial(_wkv_kernel, ts=ts, npairs=npairs),
        grid=(batch, npair // npairs, nt),
        in_specs=[s3(fwd), s3(fwd), s3(fwd), s4(0, fwd), s4(0, fwd), s4(0, fwd),
                  s3(bwd), s3(bwd), s3(bwd), s4(1, bwd), s4(1, bwd), s4(1, bwd)],
        out_specs=(s3(fwd), s3(bwd)),
        out_shape=(jax.ShapeDtypeStruct((npair, m, LANES), F32),) * 2,
        scratch_shapes=[pltpu.VMEM((2, npairs, CHUNK, LANES), F32)],
        compiler_params=_cparams(("parallel", "parallel", "arbitrary")),
        name="wkv",
    )(r, v, kn, kd, ba, lw, r, v, kn, kd, ba, lw)


def _rwkv_post_kernel(of_ref, ob_ref, bonus_ref, gate_ref, lw_ref, lb_ref, e_ref, wo_ref, x_ref, out_ref):
    y = jnp.concatenate([of_ref[p] + ob_ref[p] for p in range(of_ref.shape[0])], axis=1)
    mean =_head_sum(y, e_ref) * (1.0 / RWKV_HEAD)
    yc = y - mean
    var = _head_sum(yc * yc, e_ref) * (1.0 / RWKV_HEAD)
    yn = yc * lax.rsqrt(var + GN_EPS)
    y2 = yn * lw_ref[...] + lb_ref[...] + bonus_ref[...]
    out_ref[...] = x_ref[...] + _dot((y2 * gate_ref[...].astype(F32)).astype(BF16), wo_ref[...])


def _rwkv_post(x, o_f, o_b, bonus, gate, p):
    m = x.shape[0]
    tm = TM_RW
    row = lambda: pl.BlockSpec((tm, D_MODEL), lambda i: (i, 0))
    pairs = lambda: pl.BlockSpec((D_MODEL // LANES, tm, LANES), lambda i: (0, i, 0))
    return pl.pallas_call(
        _rwkv_post_kernel,
        grid=(m // tm,),
        in_specs=[pairs(), pairs(), row(), row(), _full((1, D_MODEL)), _full((1, D_MODEL)),
                  _full(p['e'].shape), _full((D_MODEL, D_MODEL)), row()],
        out_specs=row(),
        out_shape=jax.ShapeDtypeStruct((m, D_MODEL), F32),
        compiler_params=_cparams(("parallel",)),
        name="rwkv_post",
    )(o_f, o_b, bonus, gate, p['lnx_w'], p['lnx_b'], p['e'], p['wo'], x)


def _prep_odd(o, rw_mu, rw_wr, rw_wk, rw_wv, rw_wo, rw_w0, rw_w1, rw_w2, rw_a0, rw_a1, rw_a2, rw_g1, rw_g2,
              rw_kk, rw_ka, rw_rk, rw_lnx_w, rw_lnx_b):
    bf = lambda a: a.astype(BF16)

    def pad_dir(w):
        z = jnp.zeros_like(w[0])
        return jnp.stack([jnp.concatenate([w[0], z], axis=0), jnp.concatenate([z, w[1]], axis=0)])

    lane = jnp.arange(MXU_WIDTH) // RWKV_HEAD
    return dict(
        mu=rw_mu[o], wr=bf(rw_wr[o]), wk=bf(rw_wk[o]), wv=bf(rw_wv[o]), wo=bf(rw_wo[o]),
        w0=rw_w0[o], w1=bf(jnp.concatenate([rw_w1[o, 0], rw_w1[o, 1]], axis=1)), w2=bf(pad_dir(rw_w2[o])),
        a0=rw_a0[o], a1=bf(jnp.concatenate([rw_a1[o, 0], rw_a1[o, 1]], axis=1)), a2=bf(pad_dir(rw_a2[o])),
        g1=bf(rw_g1[o]), g2=bf(rw_g2[o]),
        kk=rw_kk[o].reshape(1, D_MODEL), ka=rw_ka[o].reshape(1, D_MODEL), rk=rw_rk[o].reshape(1, D_MODEL),
        lnx_w=rw_lnx_w[o].reshape(1, D_MODEL), lnx_b=rw_lnx_b[o].reshape(1, D_MODEL),
        e=(lane[:, None] == lane[None, :]).astype(BF16))


def _even_layer(x, g, w_in, w_out, conv_w, cos, sin, batch, seq):
    z1, z4, z16, zc = _even_proj(x, g, w_in, cos, sin, batch, seq)
    m = x.shape[0]
    o1, l1 = _dilated_branch(z1.reshape(3, batch, seq, A_WIDTH), seq)
    o4, l4 = _dilated_branch(z4.reshape(3, batch * 4, seq // 4, A_WIDTH), seq // 4)
    o16, l16 = _dilated_branch(z16.reshape(3, batch * 16, seq // 16, A_WIDTH), seq // 16)
    o1, l1 = o1.reshape(m, A_WIDTH), l1.reshape(m, A_WIDTH)
    d4 = lambda a: a.reshape(batch, 4, seq // 4, A_WIDTH)
    d16 = lambda a: a.reshape(batch, 16, seq // 16, A_WIDTH)
    return _even_out(x, o1, l1, d4(o4), d4(l4), d16(o16), d16(l16), zc, conv_w, w_out, batch, seq)


def _odd_layer(x, g, p, batch, seq):
    r, v, kn, kd, ba, lw, gate, bonus = _rwkv_pre(x, g, p, batch, seq)
    o_f, o_b = _wkv(r, v, kn, kd, ba, lw, batch, seq)
    return _rwkv_post(x, o_f, o_b, bonus, gate, p)


def _trunk(x, mem, w):
    batch, seq, _ = x.shape
    n_mem = mem.shape[1]
    x = x.reshape(batch * seq, D_MODEL)
    mem = mem.reshape(batch * n_mem, D_MODEL)
    cos, sin = _rope_tables(seq)
    for l in range(DEPTH):
        if l % 2 == 0:
            e = l // 2
            x = _even_layer(x, w['norm_mix'][l], w['ab_w_in'][e], w['ab_w_out'][e], w['ab_conv'][e],
                            cos, sin, batch, seq)
        else:
            x = _odd_layer(x, w['norm_mix'][l], w['odd'][l // 2], batch, seq)
        kv = _norm_matmul(mem, w['norm_mem'][l], w['ca_wkv'][l], n_mem, D_MODEL, BF16)
        x = _cross_attn(x, w['norm_cross'][l], kv, w['ca_wq'][l], w['ca_wo'][l], batch, seq, n_mem)
        x = _ffn(x, w['norm_ffn'][l], w['ffn_wgu'][l], w['ffn_wdown'][l], w['norm_final'], l == DEPTH - 1)
    return x.reshape(batch, seq, D_MODEL)


def kernel(x_prompt, x_sample, mem_prompt, mem_sample, norm_mix, norm_cross, norm_mem, norm_ffn, norm_final,
           ab_w_in, ab_w_out, ab_conv, rw_mu, rw_wr, rw_wk, rw_wv, rw_wo, rw_w0, rw_w1, rw_w2, rw_a0, rw_a1,
           rw_a2, rw_g1, rw_g2, rw_kk, rw_ka, rw_rk, rw_lnx_w, rw_lnx_b, ca_wq, ca_wkv, ca_wo, ffn_wgu,
           ffn_wdown):
    bf = lambda a: a.astype(BF16)
    w = dict(norm_mix=norm_mix, norm_cross=norm_cross, norm_mem=norm_mem, norm_ffn=norm_ffn,
             norm_final=norm_final, ab_w_in=bf(ab_w_in), ab_w_out=bf(ab_w_out), ab_conv=ab_conv,
             ca_wq=bf(ca_wq), ca_wkv=bf(ca_wkv), ca_wo=bf(ca_wo), ffn_wgu=bf(ffn_wgu), ffn_wdown=bf(ffn_wdown),
             odd=[_prep_odd(o, rw_mu, rw_wr, rw_wk, rw_wv, rw_wo, rw_w0, rw_w1, rw_w2, rw_a0, rw_a1, rw_a2,
                            rw_g1, rw_g2, rw_kk, rw_ka, rw_rk, rw_lnx_w, rw_lnx_b)
                  for o in range(rw_mu.shape[0])])
    return _trunk(x_prompt, mem_prompt, w), _trunk(x_sample, mem_sample, w)
```

```python
import functools

import jax
import jax.numpy as jnp
from jax import lax
from jax.experimental import pallas as pl
from jax.experimental.pallas import tpu as pltpu

F32 = jnp.float32
BF16 = jnp.bfloat16

D_MODEL = 1024
DEPTH = 4
HEAD_DIM = 64
A_WIDTH = 512
B_WIDTH = 512
DILATIONS = (1, 4, 16)
BAND = 64
ROPE_THETA = 500000.0
ROPE_DIM = 16
RWKV_HEAD = 64
GN_EPS = 64e-5
DECAY_SCALE = 0.6065306597126334
CA_HEADS = 4
CA_HEAD_DIM = 256
D_FF = 2816
RMS_EPS = 1e-6
NEG_INF = -1e30
CHUNK = 64
LANES = 128
MXU_WIDTH = 256

VMEM_LIMIT = 48 * 1024 * 1024

TM = 512
TM_RW = 256
TB_DIL = 512
TQ_DIL = 128
TF_FFN = 1408
TS_WKV = 512
PAIRS_WKV = 4
WKV_GROUP = 2


def _cparams(sem):
    return pltpu.CompilerParams(dimension_semantics=sem, vmem_limit_bytes=VMEM_LIMIT)


def _full(shape):
    n = len(shape)
    return pl.BlockSpec(shape, lambda *_: (0,) * n)


def _rms(x, g):
    ms = jnp.mean(x * x, axis=-1, keepdims=True)
    return x * lax.rsqrt(ms + RMS_EPS) * g


def _dot(a, b):
    return jnp.dot(a, b, preferred_element_type=F32)


def _dot_nt(a, b):
    return lax.dot_general(a, b, (((1,), (1,)), ((), ())), preferred_element_type=F32)


def _dot_tn(a, b):
    return lax.dot_general(a, b, (((0,), (0,)), ((), ())), preferred_element_type=F32)


def _split(x):
    hi = x.astype(BF16)
    lo = (x - hi.astype(F32)).astype(BF16)
    return hi, lo


def _norm_matmul_kernel(x_ref, g_ref, w_ref, o_ref, xn_ref):
    @pl.when(pl.program_id(1) == 0)
    def _():
        xn_ref[...] = _rms(x_ref[...], g_ref[...]).astype(BF16)

    o_ref[...] = _dot(xn_ref[...], w_ref[...]).astype(o_ref.dtype)


def _norm_matmul(x, g, w, tm, tn, out_dtype):
    m, k = x.shape
    n = w.shape[1]
    return pl.pallas_call(
        _norm_matmul_kernel,
        grid=(m // tm, n // tn),
        in_specs=[pl.BlockSpec((tm, k), lambda i, j: (i, 0)),
                  pl.BlockSpec((1, k), lambda i, j: (0, 0)),
                  pl.BlockSpec((k, tn), lambda i, j: (0, j))],
        out_specs=pl.BlockSpec((tm, tn), lambda i, j: (i, j)),
        out_shape=jax.ShapeDtypeStruct((m, n), out_dtype),
        scratch_shapes=[pltpu.VMEM((tm, k), BF16)],
        compiler_params=_cparams(("parallel", "arbitrary")),
        name="norm_matmul",
    )(x, g.reshape(1, k), w)


def _even_proj_kernel(x_ref, g_ref, w_ref, cos_ref, sin_ref, p4_ref, p16_ref,
                      o1_ref, o4_ref, o16_ref, oc_ref, *, tm):
    xn = _rms(x_ref[...], g_ref[...]).astype(BF16)
    reps = A_WIDTH // LANES
    cos = jnp.concatenate([cos_ref[...]] * reps, axis=1)
    sin = jnp.concatenate([sin_ref[...]] * reps, axis=1)
    lane = lax.broadcasted_iota(jnp.int32, (tm, A_WIDTH), 1) & (HEAD_DIM - 1)
    half = ROPE_DIM // 2
    for j in range(3):
        a = _dot(xn, w_ref[:, j * A_WIDTH:(j + 1) * A_WIDTH])
        if j < 2:
            partner = jnp.where(lane < half, pltpu.roll(a, A_WIDTH - half, 1), pltpu.roll(a, half, 1))
            a = a * cos + partner * sin
            if j == 0:
                a = a * (HEAD_DIM ** -0.5)
        a = a.astype(BF16)
        o1_ref[j] = a
        for d, p_ref, o_ref in ((4, p4_ref, o4_ref), (16, p16_ref, o16_ref)):
            perm = _dot(p_ref[...], a).astype(BF16)
            for r in range(d):
                o_ref[j, r] = perm[r * (tm // d):(r + 1) * (tm // d)]
    oc_ref[...] = _dot(xn, w_ref[:, 3 * A_WIDTH:]).astype(BF16)


def _dedilate_perm(tm, d):
    i = jnp.arange(tm)
    src = (i % (tm // d)) * d + i // (tm // d)
    return (src[:, None] == jnp.arange(tm)[None, :]).astype(BF16)


def _even_proj(x, g, w_in, cos, sin, batch, seq):
    m = x.shape[0]
    tm = TM
    nts = seq // tm
    kern = functools.partial(_even_proj_kernel, tm=tm)
    out_shape = (jax.ShapeDtypeStruct((3, m, A_WIDTH), BF16),
                 jax.ShapeDtypeStruct((3, batch, 4, seq // 4, A_WIDTH), BF16),
                 jax.ShapeDtypeStruct((3, batch, 16, seq // 16, A_WIDTH), BF16),
                 jax.ShapeDtypeStruct((m, 3 * B_WIDTH), BF16))
    return pl.pallas_call(
        kern,
        grid=(m // tm,),
        in_specs=[pl.BlockSpec((tm, D_MODEL), lambda i: (i, 0)),
                  _full((1, D_MODEL)),
                  _full(w_in.shape),
                  pl.BlockSpec((tm, LANES), lambda i: (i % nts, 0)),
                  pl.BlockSpec((tm, LANES), lambda i: (i % nts, 0)),
                  _full((tm, tm)), _full((tm, tm))],
        out_specs=(pl.BlockSpec((3, tm, A_WIDTH), lambda i: (0, i, 0)),
                   pl.BlockSpec((3, None, 4, tm // 4, A_WIDTH), lambda i: (0, i // nts, 0, i % nts, 0)),
                   pl.BlockSpec((3, None, 16, tm // 16, A_WIDTH), lambda i: (0, i // nts, 0, i % nts, 0)),
                   pl.BlockSpec((tm, 3 * B_WIDTH), lambda i: (i, 0))),
        out_shape=out_shape,
        compiler_params=_cparams(("parallel",)),
        name="even_proj",
    )(x, g.reshape(1, D_MODEL), w_in, cos, sin, _dedilate_perm(tm, 4), _dedilate_perm(tm, 16))


def _rope_tables(seq):
    half = ROPE_DIM // 2
    inv = ROPE_THETA ** (-2.0 * jnp.arange(half, dtype=F32) / ROPE_DIM)
    ang = jnp.arange(seq, dtype=F32)[:, None] * inv[None, :]
    cos, sin = jnp.cos(ang), jnp.sin(ang)
    rest = HEAD_DIM - ROPE_DIM
    cos_h = jnp.concatenate([cos, cos, jnp.ones((seq, rest), F32)], axis=1)
    sin_h = jnp.concatenate([-sin, sin, jnp.zeros((seq, rest), F32)], axis=1)
    reps = LANES // HEAD_DIM
    return jnp.tile(cos_h, (1, reps)), jnp.tile(sin_h, (1, reps))


def _dil_kernel(q_ref, kp_ref, kc_ref, kn_ref, vp_ref, vc_ref, vn_ref, o_ref, lse_ref, *, tb, tq, length):
    qi = pl.program_id(1)
    span = tq + 2 * BAND
    k = jnp.concatenate([kp_ref[...], kc_ref[...], kn_ref[...]], axis=0)
    v = jnp.concatenate([vp_ref[...], vc_ref[...], vn_ref[...]], axis=0)
    row = lax.broadcasted_iota(jnp.int32, (tq, span), 0)
    col = lax.broadcasted_iota(jnp.int32, (tq, span), 1)
    rel = col - BAND - row
    band = (rel <= BAND) & (rel >= -BAND)
    first = lax.broadcasted_iota(jnp.int32, (tq, LANES), 1) < HEAD_DIM
    for s_idx in range(tb // tq):
        rows = slice(s_idx * tq, (s_idx + 1) * tq)
        keys = slice(s_idx * tq, s_idx * tq + span)
        key_pos = qi * tb + s_idx * tq - BAND + col
        valid = band & (key_pos >= 0) & (key_pos < length)
        for p in range(A_WIDTH // LANES):
            sl = slice(p * LANES, (p + 1) * LANES)
            qp, kp, vp = q_ref[rows, sl], k[keys, sl], v[keys, sl]
            outs, lses = [], []
            for sel in (first, jnp.logical_not(first)):
                s = _dot_nt(jnp.where(sel, qp, jnp.zeros_like(qp)), kp)
                s = jnp.where(valid, s, NEG_INF)
                mx = jnp.max(s, axis=-1, keepdims=True)
                e = jnp.exp(s - mx)
                den = jnp.sum(e, axis=-1, keepdims=True)
                outs.append(_dot(e.astype(BF16), vp) * (1.0 / den))
                lses.append(mx + jnp.log(den))
            o_ref[rows, sl] = jnp.where(first, outs[0], outs[1]).astype(BF16)
            lse_ref[rows, sl] = jnp.where(first, lses[0], lses[1])


def _dilated_branch(zd, length):
    g = zd.shape[1]
    tb = min(TB_DIL, length)
    tq = min(TQ_DIL, length)
    nb = length // BAND
    r = tb // BAND
    kern = functools.partial(_dil_kernel, tb=tb, tq=tq, length=length)

    def cur(which):
        return pl.BlockSpec((None, None, tb, A_WIDTH), lambda b, i: (which, b, i, 0))

    def prev(which):
        return pl.BlockSpec((None, None, BAND, A_WIDTH), lambda b, i: (which, b, jnp.maximum(i * r - 1, 0), 0))

    def nxt(which):
        return pl.BlockSpec((None, None, BAND, A_WIDTH), lambda b, i: (which, b, jnp.minimum((i + 1) * r, nb - 1), 0))

    return pl.pallas_call(
        kern,
        grid=(g, length // tb),
        in_specs=[cur(0), prev(1), cur(1), nxt(1), prev(2), cur(2), nxt(2)],
        out_specs=(pl.BlockSpec((None, tb, A_WIDTH), lambda b, i: (b, i, 0)),
                   pl.BlockSpec((None, tb, A_WIDTH), lambda b, i: (b, i, 0))),
        out_shape=(jax.ShapeDtypeStruct((g, length, A_WIDTH), BF16),
                   jax.ShapeDtypeStruct((g, length, A_WIDTH), F32)),
        compiler_params=_cparams(("parallel", "parallel")),
        name="dilated_attn",
    )(zd, zd, zd, zd, zd, zd, zd)


def _even_out_kernel(o1_ref, l1_ref, o4_ref, l4_ref, o16_ref, l16_ref, bg_ref, cg_ref, h_ref,
                     cgp_ref, hp_ref, cgn_ref, hn_ref, cw_ref, wa_ref, wb_ref, x_ref, out_ref,
                     s4o, s4l, s16o, s16l, *, tm, nts):
    it = pl.program_id(0) % nts
    nslab = A_WIDTH // LANES
    for c in range(nslab):
        sl = slice(c * LANES, (c + 1) * LANES)
        for r in range(4):
            s4o[c, pl.ds(r, tm // 4, stride=4), :] = o4_ref[r, :, sl].astype(F32)
            s4l[c, pl.ds(r, tm // 4, stride=4), :] = l4_ref[r, :, sl]
        for r in range(16):
            s16o[c, pl.ds(r, tm // 16, stride=16), :] = o16_ref[r, :, sl].astype(F32)
            s16l[c, pl.ds(r, tm // 16, stride=16), :] = l16_ref[r, :, sl]
    wide = lambda ref: jnp.concatenate([ref[c] for c in range(nslab)], axis=1)
    l1, l4, l16 = l1_ref[...], wide(s4l), wide(s16l)
    mx = jnp.maximum(jnp.maximum(l1, l4), l16)
    e1, e4, e16 = jnp.exp(l1 - mx), jnp.exp(l4 - mx), jnp.exp(l16 - mx)
    ya = (e1 * o1_ref[...].astype(F32) + e4 * wide(s4o) + e16 * wide(s16o)) * (1.0 / (e1 + e4 + e16))

    u = cg_ref[...].astype(F32) * h_ref[...].astype(F32)
    last = cgp_ref.shape[0] - 1
    u_before = (cgp_ref[...].astype(F32) * hp_ref[...].astype(F32))[last:last + 1]
    u_after = (cgn_ref[...].astype(F32) * hn_ref[...].astype(F32))[0:1]
    u_before = u_before * (it > 0).astype(F32)
    u_after = u_after * (it < nts - 1).astype(F32)
    row = lax.broadcasted_iota(jnp.int32, u.shape, 0)
    u_prev = jnp.where(row == 0, u_before, pltpu.roll(u, 1, 0))
    u_next = jnp.where(row == tm - 1, u_after, pltpu.roll(u, tm - 1, 0))
    cw = cw_ref[...]
    yb = bg_ref[...].astype(F32) * (cw[0:1] * u_prev + cw[1:2] * u + cw[2:3] * u_next)

    out_ref[...] = x_ref[...] + _dot(ya.astype(BF16), wa_ref[...]) + _dot(yb.astype(BF16), wb_ref[...])


def _even_out(x, o1, l1, o4, l4, o16, l16, zc, conv_w, w_out, batch, seq):
    m = x.shape[0]
    tm = TM
    nts = seq // tm
    hb = 16
    nhb = m // hb
    kern = functools.partial(_even_out_kernel, tm=tm, nts=nts)
    nat = lambda: pl.BlockSpec((tm, A_WIDTH), lambda i: (i, 0))
    dil = lambda d: pl.BlockSpec((None, d, tm // d, A_WIDTH), lambda i: (i // nts, 0, i % nts, 0))
    col = lambda c: pl.BlockSpec((tm, B_WIDTH), lambda i: (i, c))
    before = lambda c: pl.BlockSpec((hb, B_WIDTH), lambda i: (jnp.maximum(i * (tm // hb) - 1, 0), c))
    after = lambda c: pl.BlockSpec((hb, B_WIDTH), lambda i: (jnp.minimum((i + 1) * (tm // hb), nhb - 1), c))
    return pl.pallas_call(
        kern,
        grid=(m // tm,),
        in_specs=[nat(), nat(), dil(4), dil(4), dil(16), dil(16),
                  col(0), col(1), col(2), before(1), before(2), after(1), after(2),
                  _full((3, B_WIDTH)),
                  pl.BlockSpec((A_WIDTH, D_MODEL), lambda i: (0, 0)),
                  pl.BlockSpec((B_WIDTH, D_MODEL), lambda i: (1, 0)),
                  pl.BlockSpec((tm, D_MODEL), lambda i: (i, 0))],
        out_specs=pl.BlockSpec((tm, D_MODEL), lambda i: (i, 0)),
        out_shape=jax.ShapeDtypeStruct((m, D_MODEL), F32),
        scratch_shapes=[pltpu.VMEM((A_WIDTH // LANES, tm, LANES), F32)] * 4,
        compiler_params=_cparams(("parallel",)),
        name="even_out",
    )(o1, l1, o4, l4, o16, l16, zc, zc, zc, zc, zc, zc, zc, conv_w, w_out, w_out, x)


def _cross_kernel(x_ref, g_ref, k_ref, v_ref, wq_ref, wo_ref, out_ref):
    x = x_ref[...]
    xn = _rms(x, g_ref[...]).astype(BF16)
    q = (_dot(xn, wq_ref[...]) * (CA_HEAD_DIM ** -0.5)).astype(BF16)
    k = k_ref[...]
    v = v_ref[...]
    outs = []
    for h in range(CA_HEADS):
        sl = slice(h * CA_HEAD_DIM, (h + 1) * CA_HEAD_DIM)
        s = _dot_nt(q[:, sl], k[:, sl])
        mx = jnp.max(s, axis=-1, keepdims=True)
        e = jnp.exp(s - mx)
        den = jnp.sum(e, axis=-1, keepdims=True)
        outs.append(_dot((e * (1.0 / den)).astype(BF16), v[:, sl]).astype(BF16))
    o = jnp.concatenate(outs, axis=1)
    out_ref[...] = x + _dot(o, wo_ref[...])


def _cross_attn(x, g, kv, wq, wo, batch, seq, n_mem):
    m = x.shape[0]
    tq = TM
    nts = seq // tq
    return pl.pallas_call(
        _cross_kernel,
        grid=(m // tq,),
        in_specs=[pl.BlockSpec((tq, D_MODEL), lambda i: (i, 0)),
                  _full((1, D_MODEL)),
                  pl.BlockSpec((n_mem, D_MODEL), lambda i: (i // nts, 0)),
                  pl.BlockSpec((n_mem, D_MODEL), lambda i: (i // nts, 1)),
                  _full((D_MODEL, D_MODEL)),
                  _full((D_MODEL, D_MODEL))],
        out_specs=pl.BlockSpec((tq, D_MODEL), lambda i: (i, 0)),
        out_shape=jax.ShapeDtypeStruct((m, D_MODEL), F32),
        compiler_params=_cparams(("parallel",)),
        name="cross_attn",
    )(x, g.reshape(1, D_MODEL), kv, kv, wq, wo)


def _ffn_kernel(x_ref, g_ref, wg_ref, wu_ref, wd_ref, gf_ref, out_ref, xn_ref, acc_ref, *, final_norm):
    f = pl.program_id(1)

    @pl.when(f == 0)
    def _():
        xn_ref[...] = _rms(x_ref[...], g_ref[...]).astype(BF16)
        acc_ref[...] = x_ref[...]

    xn = xn_ref[...]
    gate = _dot(xn, wg_ref[...])
    up = _dot(xn, wu_ref[...])
    act = (gate * (1.0 / (1.0 + jnp.exp(-gate))) * up).astype(BF16)
    acc_ref[...] += _dot(act, wd_ref[...])

    @pl.when(f == pl.num_programs(1) - 1)
    def _():
        y = acc_ref[...]
        if final_norm:
            y = _rms(y, gf_ref[...])
        out_ref[...] = y


def _ffn(x, g, w_gu, w_down, g_final, final_norm):
    m = x.shape[0]
    tm, tf = TM, TF_FFN
    nf = D_FF // tf
    kern = functools.partial(_ffn_kernel, final_norm=final_norm)
    return pl.pallas_call(
        kern,
        grid=(m // tm, nf),
        in_specs=[pl.BlockSpec((tm, D_MODEL), lambda i, f: (i, 0)),
                  pl.BlockSpec((1, D_MODEL), lambda i, f: (0, 0)),
                  pl.BlockSpec((D_MODEL, tf), lambda i, f: (0, f)),
                  pl.BlockSpec((D_MODEL, tf), lambda i, f: (0, nf + f)),
                  pl.BlockSpec((tf, D_MODEL), lambda i, f: (f, 0)),
                  pl.BlockSpec((1, D_MODEL), lambda i, f: (0, 0))],
        out_specs=pl.BlockSpec((tm, D_MODEL), lambda i, f: (i, 0)),
        out_shape=jax.ShapeDtypeStruct((m, D_MODEL), F32),
        scratch_shapes=[pltpu.VMEM((tm, D_MODEL), BF16), pltpu.VMEM((tm, D_MODEL), F32)],
        compiler_params=_cparams(("parallel", "arbitrary")),
        name="swiglu",
    )(x, g.reshape(1, D_MODEL), w_gu, w_gu, w_down, g_final.reshape(1, D_MODEL))


def _head_sum(x, e_ref):
    x = x.astype(BF16)
    e = e_ref[...]
    width = e.shape[0]
    return jnp.concatenate([_dot(x[:, c * width:(c + 1) * width], e) for c in range(x.shape[1] // width)],
                           axis=1)


def _sigmoid(x):
    return 0.5 * jnp.tanh(0.5 * x) + 0.5


def _store_pairs(ref, val, *lead):
    for p in range(val.shape[1] // LANES):
        ref[(*lead, p)] = val[:, p * LANES:(p + 1) * LANES].astype(ref.dtype)


def _rwkv_pre_kernel(x_ref, xb_ref, xa_ref, g_ref, mu_ref, wr_ref, wk_ref, wv_ref, w1_ref, w2_ref, w0_ref,
                     a1_ref, a2_ref, a0_ref, g1_ref, g2_ref, kk_ref, ka_ref, rk_ref, e_ref,
                     sm_ref, r_out, v_out, kn_out, kd_out, ba_out, lw_out, gate_out, bonus_out, *, nts):
    it = pl.program_id(0) % nts
    g = g_ref[...]
    xn_bf = _rms(x_ref[...], g).astype(BF16)
    before = (_rms(xb_ref[...], g) * (it > 0).astype(F32)).astype(BF16)
    after = (_rms(xa_ref[...], g) * (it < nts - 1).astype(F32)).astype(BF16)
    xx_bf = _dot(sm_ref[...], jnp.concatenate([before, xn_bf, after], axis=0)).astype(BF16)
    mu = mu_ref[...].astype(BF16)
    mix = lambda i: xn_bf + xx_bf * mu[i:i + 1]

    r = _dot(mix(0), wr_ref[...])
    k_lin = _dot(mix(2), wk_ref[...])
    v = _dot(mix(3), wv_ref[...])
    gate = _dot(_sigmoid(_dot(mix(5), g1_ref[...])).astype(BF16), g2_ref[...])
    kn = k_lin * kk_ref[...]
    kn = kn * lax.rsqrt(jnp.maximum(_head_sum(kn * kn, e_ref), 1e-24))
    tw = jnp.tanh(_dot(mix(1), w1_ref[...])).astype(BF16)
    ta = _dot(mix(4), a1_ref[...]).astype(BF16)
    ka = ka_ref[...]
    kd_sum = jnp.zeros_like(k_lin)
    for d in range(2):
        wl = w0_ref[d:d + 1] + _dot(tw, w2_ref[d])
        _store_pairs(lw_out, -DECAY_SCALE * _sigmoid(wl), d)
        a = _sigmoid(a0_ref[d:d + 1] + _dot(ta, a2_ref[d]))
        kd = k_lin * (1.0 + (a - 1.0) * ka)
        kd_sum = kd_sum + kd
        _store_pairs(kd_out, kd, d)
        _store_pairs(ba_out, kn * a, d)
    _store_pairs(r_out, r)
    _store_pairs(v_out, v)
    _store_pairs(kn_out, kn)
    gate_out[...] = gate.astype(BF16)
    bonus_out[...] = _head_sum(r * kd_sum * rk_ref[...], e_ref) * v


def _rwkv_pre(x, g, p, batch, seq):
    m = x.shape[0]
    tm = TM_RW
    nts = seq // tm
    hb = 16
    nhb = m // hb
    kern = functools.partial(_rwkv_pre_kernel, nts=nts)
    row = lambda: pl.BlockSpec((tm, D_MODEL), lambda i: (i, 0))
    npair = D_MODEL // LANES
    pairs = lambda: pl.BlockSpec((npair, tm, LANES), lambda i: (0, i, 0))
    pairs2 = lambda: pl.BlockSpec((2, npair, tm, LANES), lambda i: (0, 0, i, 0))
    t = jnp.arange(tm)[:, None]
    c = jnp.arange(tm + 2 * hb)[None, :] - hb
    shift = (jnp.where(c == t, -1.0, 0.0) + jnp.where(jnp.abs(c - t) == 1, 0.5, 0.0)).astype(BF16)
    args = (x, x, x, g.reshape(1, D_MODEL), p['mu'], p['wr'], p['wk'], p['wv'], p['w1'], p['w2'], p['w0'],
            p['a1'], p['a2'], p['a0'], p['g1'], p['g2'], p['kk'], p['ka'], p['rk'], p['e'], shift)
    in_specs = [row(),
                pl.BlockSpec((hb, D_MODEL), lambda i: (jnp.maximum(i * (tm // hb) - 1, 0), 0)),
                pl.BlockSpec((hb, D_MODEL), lambda i: (jnp.minimum((i + 1) * (tm // hb), nhb - 1), 0))]
    in_specs += [_full(a.shape) for a in args[3:]]
    out_shape = (jax.ShapeDtypeStruct((npair, m, LANES), BF16),) * 3 + (
        jax.ShapeDtypeStruct((2, npair, m, LANES), BF16), jax.ShapeDtypeStruct((2, npair, m, LANES), BF16),
        jax.ShapeDtypeStruct((2, npair, m, LANES), F32),
        jax.ShapeDtypeStruct((m, D_MODEL), BF16), jax.ShapeDtypeStruct((m, D_MODEL), F32))
    out_specs = (pairs(), pairs(), pairs(), pairs2(), pairs2(), pairs2(), row(), row())
    return pl.pallas_call(
        kern, grid=(m // tm,), in_specs=in_specs, out_specs=out_specs, out_shape=out_shape,
        compiler_params=_cparams(("parallel",)), name="rwkv_pre",
    )(*args)


def _bd(x, first):
    z = jnp.zeros_like(x)
    return jnp.concatenate([jnp.where(first, x, z), jnp.where(first, z, x)], axis=0)


def _compact(full, first):
    return jnp.where(first, full[:CHUNK], full[CHUNK:])


def _wkv_masks(sgn):
    c = CHUNK
    row = lax.broadcasted_iota(jnp.int32, (c, c), 0)
    col = lax.broadcasted_iota(jnp.int32, (c, c), 1)
    tri = jnp.where((row - col) * sgn >= 0, 1.0, 0.0).astype(BF16)
    prow = lax.broadcasted_iota(jnp.int32, (c, LANES), 0)
    pcol = lax.broadcasted_iota(jnp.int32, (c, LANES), 1) & (c - 1)
    dlt = (prow - pcol) * sgn
    strict = dlt > 0
    same = lambda sh: (prow >> sh) == (pcol >> sh)
    levels = [jnp.where(strict & same(1), 1.0, 0.0)]
    sh = 1
    while (1 << sh) < c:
        levels.append(jnp.where(strict & same(sh + 1) & jnp.logical_not(same(sh)), 1.0, 0.0))
        sh += 1
    return dict(tri=tri, strict=strict, incl=dlt >= 0, eye=jnp.where(prow == pcol, 1.0, 0.0), levels=levels)


def _wkv_local(chains, first):
    c = CHUNK
    for s in chains:
        hl = _dot(s['mk']['tri'], jnp.concatenate(_split(s['lw']), axis=1))
        s['cum'] = hl[:, :LANES] + hl[:, LANES:]
    yield
    for s in chains:
        cum, lw = s['cum'], s['lw']
        total = jnp.sum(lw, axis=0, keepdims=True)
        p_inv = jnp.exp(-cum)
        p_hat = jnp.exp(total - cum)
        s['at'] = (-s['kn'] * jnp.exp(cum - lw)).astype(BF16)
        s['rt'] = (s['r'] * jnp.exp(cum)).astype(BF16)
        s['bt'] = (s['ba'] * p_inv).astype(BF16)
        s['kt'] = (s['kd'] * p_inv).astype(BF16)
        s['bh'] = (s['ba'] * p_hat).astype(BF16)
        s['kh'] = (s['kd'] * p_hat).astype(BF16)
        s['v_bf'] = s['v'].astype(BF16)
        s['p_end'] = jnp.exp(total)
        s['lhs'] = jnp.concatenate([s['at'], s['rt']], axis=0)
    yield
    for s in chains:
        s['sbk'] = _dot_nt(s['lhs'], jnp.concatenate([_bd(s['bt'], first), _bd(s['kt'], first)], axis=0))
    yield
    for s in chains:
        mk = s['mk']
        sb, sk = s['sbk'][:, :LANES], s['sbk'][:, LANES:]
        s['a_ab'] = jnp.where(mk['strict'], sb[:c], 0.0)
        s['a_rb'] = jnp.where(mk['incl'], sb[c:], 0.0).astype(BF16)
        s['a_kk'] = jnp.concatenate([jnp.where(mk['strict'], sk[:c], 0.0),
                                     jnp.where(mk['incl'], sk[c:], 0.0)], axis=0).astype(BF16)
        s['inv'] = mk['eye'] + s['a_ab'] * mk['levels'][0]
    yield
    for s in chains:
        s['av'] = _dot(s['a_kk'], _bd(s['v_bf'], first))
        s['y0'] = _compact(_dot_tn(s['v_bf'], s['kh']), first)
    yield
    for lvl in range(1, len(chains[0]['mk']['levels'])):
        for s in chains:
            s['inv_bf'] = s['inv'].astype(BF16)
            s['inner'] = _dot((s['a_ab'] * s['mk']['levels'][lvl]).astype(BF16), _bd(s['inv_bf'], first))
        yield
        for s in chains:
            s['inv'] = s['inv'] + _dot(s['inv_bf'], _bd(s['inner'].astype(BF16), first))
        yield
    for s in chains:
        rhs = jnp.concatenate([_bd(s['at'], first), _bd(s['av'][:c].astype(BF16), first)], axis=1)
        wu = _dot(s['inv'].astype(BF16), rhs)
        s['w1'] = wu[:, :LANES].astype(BF16)
        s['u0'] = wu[:, LANES:]
        s['o0'] = s['av'][c:]


def _wkv_seq(rows, ys, out_refs, first):
    c = CHUNK
    for row in rows:
        wrs = [_dot_nt(jnp.concatenate([s['w1'], s['rt']], axis=0), _bd(y.astype(BF16), first))
               for s, y in zip(row, ys)]
        yield
        us = [(wr[:c] + s['u0']).astype(BF16) for s, wr in zip(row, wrs)]
        for s, wr, u, (o_ref, p) in zip(row, wrs, us, out_refs):
            o_ref[p, s['sl'], :] = wr[c:] + _dot(s['a_rb'], _bd(u, first)) + s['o0']
        ys[:] = [s['p_end'] * y + _compact(_dot_tn(u, s['bh']), first) + s['y0'] for s, y, u in zip(row, ys, us)]
        yield


def _interleave(main, side, every):
    side_done = side is None
    for i, _ in enumerate(main):
        if not side_done and i % every == every - 1:
            side_done = next(side, StopIteration) is StopIteration
    if not side_done:
        for _ in side:
            pass


def _wkv_kernel(rf, vf, nf, kdf, baf, lwf, rb, vb, nb, kdb, bab, lwb, of_ref, ob_ref, state, *, ts, npairs):
    first = lax.broadcasted_iota(jnp.int32, (CHUNK, LANES), 1) < RWKV_HEAD

    @pl.when(pl.program_id(2) == 0)
    def _():
        state[...] = jnp.zeros_like(state)

    nchunk = ts // CHUNK
    names = ('r', 'v', 'kn', 'kd', 'ba', 'lw')
    dirs = ((_wkv_masks(1), (rf, vf, nf, kdf, baf, lwf), of_ref, list(range(nchunk))),
            (_wkv_masks(-1), (rb, vb, nb, kdb, bab, lwb), ob_ref, list(range(nchunk - 1, -1, -1))))
    scans = [(d, p) for d in range(2) for p in range(npairs)]
    out_refs = [(dirs[d][2], p) for d, p in scans]
    steps = []
    for k in range(nchunk):
        row = []
        for d, p in scans:
            mk, ins, _, order = dirs[d]
            sl = pl.ds(order[k] * CHUNK, CHUNK)
            chain = {n: ref[p, sl, :].astype(F32) for n, ref in zip(names, ins)}
            chain.update(mk=mk, sl=sl)
            row.append(chain)
        steps.append(row)
    ys = [state[d, p] for d, p in scans]
    groups = [steps[k:k + WKV_GROUP] for k in range(0, nchunk, WKV_GROUP)]
    pending = None
    for rows in groups:
        _interleave(_wkv_local([s for row in rows for s in row], first), pending, every=2)
        pending = _wkv_seq(rows, ys, out_refs, first)
    _interleave(pending, None, every=1)
    for (d, p), y in zip(scans, ys):
        state[d, p] = y


def _wkv(r, v, kn, kd, ba, lw, batch, seq):
    ts, npairs = TS_WKV, PAIRS_WKV
    nt = seq // ts
    npair = D_MODEL // LANES
    m = batch * seq
    fwd = lambda b, t: b * nt + t
    bwd = lambda b, t: b * nt + nt - 1 - t
    s3 = lambda at: pl.BlockSpec((npairs, ts, LANES), lambda b, g, t: (g, at(b, t), 0))
    s4 = lambda d, at: pl.BlockSpec((None, npairs, ts, LANES), lambda b, g, t: (d, g, at(b, t), 0))
    return pl.pallas_call(
        functools.partial(_wkv_kernel, ts=ts, npairs=npairs),
        grid=(batch, npair // npairs, nt),
        in_specs=[s3(fwd), s3(fwd), s3(fwd), s4(0, fwd), s4(0, fwd), s4(0, fwd),
                  s3(bwd), s3(bwd), s3(bwd), s4(1, bwd), s4(1, bwd), s4(1, bwd)],
        out_specs=(s3(fwd), s3(bwd)),
        out_shape=(jax.ShapeDtypeStruct((npair, m, LANES), F32),) * 2,
        scratch_shapes=[pltpu.VMEM((2, npairs, CHUNK, LANES), F32)],
        compiler_params=_cparams(("parallel", "parallel", "arbitrary")),
        name="wkv",
    )(r, v, kn, kd, ba, lw, r, v, kn, kd, ba, lw)


def _rwkv_post_kernel(of_ref, ob_ref, bonus_ref, gate_ref, lw_ref, lb_ref, e_ref, wo_ref, x_ref, out_ref):
    y = jnp.concatenate([of_ref[p] + ob_ref[p] for p in range(of_ref.shape[0])], axis=1)
    mean =_head_sum(y, e_ref) * (1.0 / RWKV_HEAD)
    yc = y - mean
    var = _head_sum(yc * yc, e_ref) * (1.0 / RWKV_HEAD)
    yn = yc * lax.rsqrt(var + GN_EPS)
    y2 = yn * lw_ref[...] + lb_ref[...] + bonus_ref[...]
    out_ref[...] = x_ref[...] + _dot((y2 * gate_ref[...].astype(F32)).astype(BF16), wo_ref[...])


def _rwkv_post(x, o_f, o_b, bonus, gate, p):
    m = x.shape[0]
    tm = TM
    row = lambda: pl.BlockSpec((tm, D_MODEL), lambda i: (i, 0))
    pairs = lambda: pl.BlockSpec((D_MODEL // LANES, tm, LANES), lambda i: (0, i, 0))
    return pl.pallas_call(
        _rwkv_post_kernel,
        grid=(m // tm,),
        in_specs=[pairs(), pairs(), row(), row(), _full((1, D_MODEL)), _full((1, D_MODEL)),
                  _full(p['e'].shape), _full((D_MODEL, D_MODEL)), row()],
        out_specs=row(),
        out_shape=jax.ShapeDtypeStruct((m, D_MODEL), F32),
        compiler_params=_cparams(("parallel",)),
        name="rwkv_post",
    )(o_f, o_b, bonus, gate, p['lnx_w'], p['lnx_b'], p['e'], p['wo'], x)


def _prep_odd(o, rw_mu, rw_wr, rw_wk, rw_wv, rw_wo, rw_w0, rw_w1, rw_w2, rw_a0, rw_a1, rw_a2, rw_g1, rw_g2,
              rw_kk, rw_ka, rw_rk, rw_lnx_w, rw_lnx_b):
    bf = lambda a: a.astype(BF16)

    def pad_dir(w):
        z = jnp.zeros_like(w[0])
        return jnp.stack([jnp.concatenate([w[0], z], axis=0), jnp.concatenate([z, w[1]], axis=0)])

    lane = jnp.arange(MXU_WIDTH) // RWKV_HEAD
    return dict(
        mu=rw_mu[o], wr=bf(rw_wr[o]), wk=bf(rw_wk[o]), wv=bf(rw_wv[o]), wo=bf(rw_wo[o]),
        w0=rw_w0[o], w1=bf(jnp.concatenate([rw_w1[o, 0], rw_w1[o, 1]], axis=1)), w2=bf(pad_dir(rw_w2[o])),
        a0=rw_a0[o], a1=bf(jnp.concatenate([rw_a1[o, 0], rw_a1[o, 1]], axis=1)), a2=bf(pad_dir(rw_a2[o])),
        g1=bf(rw_g1[o]), g2=bf(rw_g2[o]),
        kk=rw_kk[o].reshape(1, D_MODEL), ka=rw_ka[o].reshape(1, D_MODEL), rk=rw_rk[o].reshape(1, D_MODEL),
        lnx_w=rw_lnx_w[o].reshape(1, D_MODEL), lnx_b=rw_lnx_b[o].reshape(1, D_MODEL),
        e=(lane[:, None] == lane[None, :]).astype(BF16))


def _even_layer(x, g, w_in, w_out, conv_w, cos, sin, batch, seq):
    z1, z4, z16, zc = _even_proj(x, g, w_in, cos, sin, batch, seq)
    m = x.shape[0]
    o1, l1 = _dilated_branch(z1.reshape(3, batch, seq, A_WIDTH), seq)
    o4, l4 = _dilated_branch(z4.reshape(3, batch * 4, seq // 4, A_WIDTH), seq // 4)
    o16, l16 = _dilated_branch(z16.reshape(3, batch * 16, seq // 16, A_WIDTH), seq // 16)
    o1, l1 = o1.reshape(m, A_WIDTH), l1.reshape(m, A_WIDTH)
    d4 = lambda a: a.reshape(batch, 4, seq // 4, A_WIDTH)
    d16 = lambda a: a.reshape(batch, 16, seq // 16, A_WIDTH)
    return _even_out(x, o1, l1, d4(o4), d4(l4), d16(o16), d16(l16), zc, conv_w, w_out, batch, seq)


def _odd_layer(x, g, p, batch, seq):
    r, v, kn, kd, ba, lw, gate, bonus = _rwkv_pre(x, g, p, batch, seq)
    o_f, o_b = _wkv(r, v, kn, kd, ba, lw, batch, seq)
    return _rwkv_post(x, o_f, o_b, bonus, gate, p)


def _trunk(x, mem, w):
    batch, seq, _ = x.shape
    n_mem = mem.shape[1]
    x = x.reshape(batch * seq, D_MODEL)
    mem = mem.reshape(batch * n_mem, D_MODEL)
    cos, sin = _rope_tables(seq)
    for l in range(DEPTH):
        if l % 2 == 0:
            e = l // 2
            x = _even_layer(x, w['norm_mix'][l], w['ab_w_in'][e], w['ab_w_out'][e], w['ab_conv'][e],
                            cos, sin, batch, seq)
        else:
            x = _odd_layer(x, w['norm_mix'][l], w['odd'][l // 2], batch, seq)
        kv = _norm_matmul(mem, w['norm_mem'][l], w['ca_wkv'][l], n_mem, D_MODEL, BF16)
        x = _cross_attn(x, w['norm_cross'][l], kv, w['ca_wq'][l], w['ca_wo'][l], batch, seq, n_mem)
        x = _ffn(x, w['norm_ffn'][l], w['ffn_wgu'][l], w['ffn_wdown'][l], w['norm_final'], l == DEPTH - 1)
    return x.reshape(batch, seq, D_MODEL)


def kernel(x_prompt, x_sample, mem_prompt, mem_sample, norm_mix, norm_cross, norm_mem, norm_ffn, norm_final,
           ab_w_in, ab_w_out, ab_conv, rw_mu, rw_wr, rw_wk, rw_wv, rw_wo, rw_w0, rw_w1, rw_w2, rw_a0, rw_a1,
           rw_a2, rw_g1, rw_g2, rw_kk, rw_ka, rw_rk, rw_lnx_w, rw_lnx_b, ca_wq, ca_wkv, ca_wo, ffn_wgu,
           ffn_wdown):
    bf = lambda a: a.astype(BF16)
    w = dict(norm_mix=norm_mix, norm_cross=norm_cross, norm_mem=norm_mem, norm_ffn=norm_ffn,
             norm_final=norm_final, ab_w_in=bf(ab_w_in), ab_w_out=bf(ab_w_out), ab_conv=ab_conv,
             ca_wq=bf(ca_wq), ca_wkv=bf(ca_wkv), ca_wo=bf(ca_wo), ffn_wgu=bf(ffn_wgu), ffn_wdown=bf(ffn_wdown),
             odd=[_prep_odd(o, rw_mu, rw_wr, rw_wk, rw_wv, rw_wo, rw_w0, rw_w1, rw_w2, rw_a0, rw_a1, rw_a2,
                            rw_g1, rw_g2, rw_kk, rw_ka, rw_rk, rw_lnx_w, rw_lnx_b)
                  for o in range(rw_mu.shape[0])])
    return _trunk(x_prompt, mem_prompt, w), _trunk(x_sample, mem_sample, w)
```

```python
import functools

import jax
import jax.numpy as jnp
from jax import lax
from jax.experimental import pallas as pl
from jax.experimental.pallas import tpu as pltpu

F32 = jnp.float32
BF16 = jnp.bfloat16

D_MODEL = 1024
DEPTH = 4
HEAD_DIM = 64
A_WIDTH = 512
B_WIDTH = 512
DILATIONS = (1, 4, 16)
BAND = 64
ROPE_THETA = 500000.0
ROPE_DIM = 16
RWKV_HEAD = 64
GN_EPS = 64e-5
DECAY_SCALE = 0.6065306597126334
CA_HEADS = 4
CA_HEAD_DIM = 256
D_FF = 2816
RMS_EPS = 1e-6
NEG_INF = -1e30
CHUNK = 64
LANES = 128
MXU_WIDTH = 256

VMEM_LIMIT = 48 * 1024 * 1024

TM = 512
TM_RW = 256
TB_DIL = 512
TQ_DIL = 128
TF_FFN = 256
TS_WKV = 512
PAIRS_WKV = 4
WKV_GROUP = 2


def _cparams(sem):
    return pltpu.CompilerParams(dimension_semantics=sem, vmem_limit_bytes=VMEM_LIMIT)


def _full(shape):
    n = len(shape)
    return pl.BlockSpec(shape, lambda *_: (0,) * n)


def _rms(x, g):
    ms = jnp.mean(x * x, axis=-1, keepdims=True)
    return x * lax.rsqrt(ms + RMS_EPS) * g


def _dot(a, b):
    return jnp.dot(a, b, preferred_element_type=F32)


def _dot_nt(a, b):
    return lax.dot_general(a, b, (((1,), (1,)), ((), ())), preferred_element_type=F32)


def _dot_tn(a, b):
    return lax.dot_general(a, b, (((0,), (0,)), ((), ())), preferred_element_type=F32)


def _split(x):
    hi = x.astype(BF16)
    lo = (x - hi.astype(F32)).astype(BF16)
    return hi, lo


def _norm_matmul_kernel(x_ref, g_ref, w_ref, o_ref, xn_ref):
    @pl.when(pl.program_id(1) == 0)
    def _():
        xn_ref[...] = _rms(x_ref[...], g_ref[...]).astype(BF16)

    o_ref[...] = _dot(xn_ref[...], w_ref[...]).astype(o_ref.dtype)


def _norm_matmul(x, g, w, tm, tn, out_dtype):
    m, k = x.shape
    n = w.shape[1]
    return pl.pallas_call(
        _norm_matmul_kernel,
        grid=(m // tm, n // tn),
        in_specs=[pl.BlockSpec((tm, k), lambda i, j: (i, 0)),
                  pl.BlockSpec((1, k), lambda i, j: (0, 0)),
                  pl.BlockSpec((k, tn), lambda i, j: (0, j))],
        out_specs=pl.BlockSpec((tm, tn), lambda i, j: (i, j)),
        out_shape=jax.ShapeDtypeStruct((m, n), out_dtype),
        scratch_shapes=[pltpu.VMEM((tm, k), BF16)],
        compiler_params=_cparams(("parallel", "arbitrary")),
        name="norm_matmul",
    )(x, g.reshape(1, k), w)


def _even_proj_kernel(x_ref, g_ref, w_ref, cos_ref, sin_ref, p4_ref, p16_ref,
                      o1_ref, o4_ref, o16_ref, oc_ref, *, tm):
    xn = _rms(x_ref[...], g_ref[...]).astype(BF16)
    reps = A_WIDTH // LANES
    cos = jnp.concatenate([cos_ref[...]] * reps, axis=1)
    sin = jnp.concatenate([sin_ref[...]] * reps, axis=1)
    lane = lax.broadcasted_iota(jnp.int32, (tm, A_WIDTH), 1) & (HEAD_DIM - 1)
    half = ROPE_DIM // 2
    nparts = 3
    acc = None
    for j in range(nparts + 1):
        nxt = _dot(xn, w_ref[:, j * A_WIDTH:(j + 1) * A_WIDTH]) if j < nparts else None
        if j == nparts:
            oc_ref[...] = _dot(xn, w_ref[:, 3 * A_WIDTH:]).astype(BF16)
        if j > 0:
            a, part = acc, j - 1
            if part < 2:
                partner = jnp.where(lane < half, pltpu.roll(a, A_WIDTH - half, 1), pltpu.roll(a, half, 1))
                a = a * cos + partner * sin
                if part == 0:
                    a = a * (HEAD_DIM ** -0.5)
            a = a.astype(BF16)
            o1_ref[part] = a
            for d, p_ref, o_ref in ((4, p4_ref, o4_ref), (16, p16_ref, o16_ref)):
                perm = _dot(p_ref[...], a).astype(BF16)
                for r in range(d):
                    o_ref[part, r] = perm[r * (tm // d):(r + 1) * (tm // d)]
        acc = nxt


def _dedilate_perm(tm, d):
    i = jnp.arange(tm)
    src = (i % (tm // d)) * d + i // (tm // d)
    return (src[:, None] == jnp.arange(tm)[None, :]).astype(BF16)


def _even_proj(x, g, w_in, cos, sin, batch, seq):
    m = x.shape[0]
    tm = TM
    nts = seq // tm
    kern = functools.partial(_even_proj_kernel, tm=tm)
    out_shape = (jax.ShapeDtypeStruct((3, m, A_WIDTH), BF16),
                 jax.ShapeDtypeStruct((3, batch, 4, seq // 4, A_WIDTH), BF16),
                 jax.ShapeDtypeStruct((3, batch, 16, seq // 16, A_WIDTH), BF16),
                 jax.ShapeDtypeStruct((m, 3 * B_WIDTH), BF16))
    return pl.pallas_call(
        kern,
        grid=(m // tm,),
        in_specs=[pl.BlockSpec((tm, D_MODEL), lambda i: (i, 0)),
                  _full((1, D_MODEL)),
                  _full(w_in.shape),
                  pl.BlockSpec((tm, LANES), lambda i: (i % nts, 0)),
                  pl.BlockSpec((tm, LANES), lambda i: (i % nts, 0)),
                  _full((tm, tm)), _full((tm, tm))],
        out_specs=(pl.BlockSpec((3, tm, A_WIDTH), lambda i: (0, i, 0)),
                   pl.BlockSpec((3, None, 4, tm // 4, A_WIDTH), lambda i: (0, i // nts, 0, i % nts, 0)),
                   pl.BlockSpec((3, None, 16, tm // 16, A_WIDTH), lambda i: (0, i // nts, 0, i % nts, 0)),
                   pl.BlockSpec((tm, 3 * B_WIDTH), lambda i: (i, 0))),
        out_shape=out_shape,
        compiler_params=_cparams(("parallel",)),
        name="even_proj",
    )(x, g.reshape(1, D_MODEL), w_in, cos, sin, _dedilate_perm(tm, 4), _dedilate_perm(tm, 16))


def _rope_tables(seq):
    half = ROPE_DIM // 2
    inv = ROPE_THETA ** (-2.0 * jnp.arange(half, dtype=F32) / ROPE_DIM)
    ang = jnp.arange(seq, dtype=F32)[:, None] * inv[None, :]
    cos, sin = jnp.cos(ang), jnp.sin(ang)
    rest = HEAD_DIM - ROPE_DIM
    cos_h = jnp.concatenate([cos, cos, jnp.ones((seq, rest), F32)], axis=1)
    sin_h = jnp.concatenate([-sin, sin, jnp.zeros((seq, rest), F32)], axis=1)
    reps = LANES // HEAD_DIM
    return jnp.tile(cos_h, (1, reps)), jnp.tile(sin_h, (1, reps))


def _dil_kernel(q_ref, kp_ref, kc_ref, kn_ref, vp_ref, vc_ref, vn_ref, o_ref, lse_ref, *, tb, tq, length):
    qi = pl.program_id(1)
    span = tq + 2 * BAND
    k = jnp.concatenate([kp_ref[...], kc_ref[...], kn_ref[...]], axis=0)
    v = jnp.concatenate([vp_ref[...], vc_ref[...], vn_ref[...]], axis=0)
    row = lax.broadcasted_iota(jnp.int32, (tq, span), 0)
    col = lax.broadcasted_iota(jnp.int32, (tq, span), 1)
    rel = col - BAND - row
    band = (rel <= BAND) & (rel >= -BAND)
    first = lax.broadcasted_iota(jnp.int32, (tq, LANES), 1) < HEAD_DIM
    for s_idx in range(tb // tq):
        rows = slice(s_idx * tq, (s_idx + 1) * tq)
        keys = slice(s_idx * tq, s_idx * tq + span)
        key_pos = qi * tb + s_idx * tq - BAND + col
        valid = band & (key_pos >= 0) & (key_pos < length)
        for p in range(A_WIDTH // LANES):
            sl = slice(p * LANES, (p + 1) * LANES)
            qp, kp, vp = q_ref[rows, sl], k[keys, sl], v[keys, sl]
            outs, lses = [], []
            for sel in (first, jnp.logical_not(first)):
                s = _dot_nt(jnp.where(sel, qp, jnp.zeros_like(qp)), kp)
                s = jnp.where(valid, s, NEG_INF)
                mx = jnp.max(s, axis=-1, keepdims=True)
                e = jnp.exp(s - mx)
                den = jnp.sum(e, axis=-1, keepdims=True)
                outs.append(_dot(e.astype(BF16), vp) * (1.0 / den))
                lses.append(mx + jnp.log(den))
            o_ref[rows, sl] = jnp.where(first, outs[0], outs[1]).astype(BF16)
            lse_ref[rows, sl] = jnp.where(first, lses[0], lses[1])


def _dilated_branch(zd, length):
    g = zd.shape[1]
    tb = min(TB_DIL, length)
    tq = min(TQ_DIL, length)
    nb = length // BAND
    r = tb // BAND
    kern = functools.partial(_dil_kernel, tb=tb, tq=tq, length=length)

    def cur(which):
        return pl.BlockSpec((None, None, tb, A_WIDTH), lambda b, i: (which, b, i, 0))

    def prev(which):
        return pl.BlockSpec((None, None, BAND, A_WIDTH), lambda b, i: (which, b, jnp.maximum(i * r - 1, 0), 0))

    def nxt(which):
        return pl.BlockSpec((None, None, BAND, A_WIDTH), lambda b, i: (which, b, jnp.minimum((i + 1) * r, nb - 1), 0))

    return pl.pallas_call(
        kern,
        grid=(g, length // tb),
        in_specs=[cur(0), prev(1), cur(1), nxt(1), prev(2), cur(2), nxt(2)],
        out_specs=(pl.BlockSpec((None, tb, A_WIDTH), lambda b, i: (b, i, 0)),
                   pl.BlockSpec((None, tb, A_WIDTH), lambda b, i: (b, i, 0))),
        out_shape=(jax.ShapeDtypeStruct((g, length, A_WIDTH), BF16),
                   jax.ShapeDtypeStruct((g, length, A_WIDTH), F32)),
        compiler_params=_cparams(("parallel", "parallel")),
        name="dilated_attn",
    )(zd, zd, zd, zd, zd, zd, zd)


def _even_out_kernel(o1_ref, l1_ref, o4_ref, l4_ref, o16_ref, l16_ref, bg_ref, cg_ref, h_ref,
                     cgp_ref, hp_ref, cgn_ref, hn_ref, cw_ref, wa_ref, wb_ref, x_ref, out_ref,
                     s4o, s4l, s16o, s16l, *, tm, nts):
    it = pl.program_id(0) % nts
    nslab = A_WIDTH // LANES
    for c in range(nslab):
        sl = slice(c * LANES, (c + 1) * LANES)
        for r in range(4):
            s4o[c, pl.ds(r, tm // 4, stride=4), :] = o4_ref[r, :, sl].astype(F32)
            s4l[c, pl.ds(r, tm // 4, stride=4), :] = l4_ref[r, :, sl]
        for r in range(16):
            s16o[c, pl.ds(r, tm // 16, stride=16), :] = o16_ref[r, :, sl].astype(F32)
            s16l[c, pl.ds(r, tm // 16, stride=16), :] = l16_ref[r, :, sl]
    wide = lambda ref: jnp.concatenate([ref[c] for c in range(nslab)], axis=1)
    l1, l4, l16 = l1_ref[...], wide(s4l), wide(s16l)
    mx = jnp.maximum(jnp.maximum(l1, l4), l16)
    e1, e4, e16 = jnp.exp(l1 - mx), jnp.exp(l4 - mx), jnp.exp(l16 - mx)
    ya = (e1 * o1_ref[...].astype(F32) + e4 * wide(s4o) + e16 * wide(s16o)) * (1.0 / (e1 + e4 + e16))

    u = cg_ref[...].astype(F32) * h_ref[...].astype(F32)
    last = cgp_ref.shape[0] - 1
    u_before = (cgp_ref[...].astype(F32) * hp_ref[...].astype(F32))[last:last + 1]
    u_after = (cgn_ref[...].astype(F32) * hn_ref[...].astype(F32))[0:1]
    u_before = u_before * (it > 0).astype(F32)
    u_after = u_after * (it < nts - 1).astype(F32)
    row = lax.broadcasted_iota(jnp.int32, u.shape, 0)
    u_prev = jnp.where(row == 0, u_before, pltpu.roll(u, 1, 0))
    u_next = jnp.where(row == tm - 1, u_after, pltpu.roll(u, tm - 1, 0))
    cw = cw_ref[...]
    yb = bg_ref[...].astype(F32) * (cw[0:1] * u_prev + cw[1:2] * u + cw[2:3] * u_next)

    out_ref[...] = x_ref[...] + _dot(ya.astype(BF16), wa_ref[...]) + _dot(yb.astype(BF16), wb_ref[...])


def _even_out(x, o1, l1, o4, l4, o16, l16, zc, conv_w, w_out, batch, seq):
    m = x.shape[0]
    tm = TM
    nts = seq // tm
    hb = 16
    nhb = m // hb
    kern = functools.partial(_even_out_kernel, tm=tm, nts=nts)
    nat = lambda: pl.BlockSpec((tm, A_WIDTH), lambda i: (i, 0))
    dil = lambda d: pl.BlockSpec((None, d, tm // d, A_WIDTH), lambda i: (i // nts, 0, i % nts, 0))
    col = lambda c: pl.BlockSpec((tm, B_WIDTH), lambda i: (i, c))
    before = lambda c: pl.BlockSpec((hb, B_WIDTH), lambda i: (jnp.maximum(i * (tm // hb) - 1, 0), c))
    after = lambda c: pl.BlockSpec((hb, B_WIDTH), lambda i: (jnp.minimum((i + 1) * (tm // hb), nhb - 1), c))
    return pl.pallas_call(
        kern,
        grid=(m // tm,),
        in_specs=[nat(), nat(), dil(4), dil(4), dil(16), dil(16),
                  col(0), col(1), col(2), before(1), before(2), after(1), after(2),
                  _full((3, B_WIDTH)),
                  pl.BlockSpec((A_WIDTH, D_MODEL), lambda i: (0, 0)),
                  pl.BlockSpec((B_WIDTH, D_MODEL), lambda i: (1, 0)),
                  pl.BlockSpec((tm, D_MODEL), lambda i: (i, 0))],
        out_specs=pl.BlockSpec((tm, D_MODEL), lambda i: (i, 0)),
        out_shape=jax.ShapeDtypeStruct((m, D_MODEL), F32),
        scratch_shapes=[pltpu.VMEM((A_WIDTH // LANES, tm, LANES), F32)] * 4,
        compiler_params=_cparams(("parallel",)),
        name="even_out",
    )(o1, l1, o4, l4, o16, l16, zc, zc, zc, zc, zc, zc, zc, conv_w, w_out, w_out, x)


def _cross_kernel(x_ref, g_ref, k_ref, v_ref, wq_ref, wo_ref, out_ref, *, nsub):
    rows = x_ref.shape[0] // nsub
    g = g_ref[...]
    heads = [slice(h * CA_HEAD_DIM, (h + 1) * CA_HEAD_DIM) for h in range(CA_HEADS)]
    k = k_ref[...]
    v = v_ref[...]
    qs = []
    for i in range(nsub):
        xn = _rms(x_ref[i * rows:(i + 1) * rows, :], g).astype(BF16)
        qs.append((_dot(xn, wq_ref[...]) * (CA_HEAD_DIM ** -0.5)).astype(BF16))
    units = [(i, h) for i in range(nsub) for h in range(CA_HEADS)]
    scores, probs, outs = {}, {}, {}
    for step in range(len(units) + 2):
        if step < len(units):
            i, h = units[step]
            scores[step] = _dot_nt(qs[i][:, heads[h]], k[:, heads[h]])
        if 0 <= step - 1 < len(units):
            s = scores.pop(step - 1)
            e = jnp.exp(s - jnp.max(s, axis=-1, keepdims=True))
            probs[step - 1] = (e * (1.0 / jnp.sum(e, axis=-1, keepdims=True))).astype(BF16)
        if 0 <= step - 2 < len(units):
            i, h = units[step - 2]
            outs[i, h] = _dot(probs.pop(step - 2), v[:, heads[h]]).astype(BF16)
            if h == CA_HEADS - 1:
                o = jnp.concatenate([outs.pop((i, hh)) for hh in range(CA_HEADS)], axis=1)
                out_ref[i * rows:(i + 1) * rows, :] = x_ref[i * rows:(i + 1) * rows, :] + _dot(o, wo_ref[...])


def _cross_attn(x, g, kv, wq, wo, batch, seq, n_mem):
    m = x.shape[0]
    tq = TM
    nts = seq // tq
    return pl.pallas_call(
        functools.partial(_cross_kernel, nsub=2),
        grid=(m // tq,),
        in_specs=[pl.BlockSpec((tq, D_MODEL), lambda i: (i, 0)),
                  _full((1, D_MODEL)),
                  pl.BlockSpec((n_mem, D_MODEL), lambda i: (i // nts, 0)),
                  pl.BlockSpec((n_mem, D_MODEL), lambda i: (i // nts, 1)),
                  _full((D_MODEL, D_MODEL)),
                  _full((D_MODEL, D_MODEL))],
        out_specs=pl.BlockSpec((tq, D_MODEL), lambda i: (i, 0)),
        out_shape=jax.ShapeDtypeStruct((m, D_MODEL), F32),
        compiler_params=_cparams(("parallel",)),
        name="cross_attn",
    )(x, g.reshape(1, D_MODEL), kv, kv, wq, wo)


def _ffn_kernel(x_ref, g_ref, wgu_ref, wd_ref, gf_ref, out_ref, *, final_norm, tf, nsub):
    rows = x_ref.shape[0] // nsub
    g = g_ref[...]
    accs = [x_ref[i * rows:(i + 1) * rows, :] for i in range(nsub)]
    xns = [_rms(x, g).astype(BF16) for x in accs]
    nf = D_FF // tf
    acts = None
    for f in range(nf + 1):
        new_acts = []
        if f < nf:
            for i in range(nsub):
                gate = _dot(xns[i], wgu_ref[:, f * tf:(f + 1) * tf])
                up = _dot(xns[i], wgu_ref[:, D_FF + f * tf:D_FF + (f + 1) * tf])
                new_acts.append((gate * _sigmoid(gate) * up).astype(BF16))
        if f > 0:
            for i in range(nsub):
                accs[i] = accs[i] + _dot(acts[i], wd_ref[(f - 1) * tf:f * tf, :])
        acts = new_acts
    for i in range(nsub):
        y = _rms(accs[i], gf_ref[...]) if final_norm else accs[i]
        out_ref[i * rows:(i + 1) * rows, :] = y


def _ffn(x, g, w_gu, w_down, g_final, final_norm):
    m = x.shape[0]
    tm = TM
    kern = functools.partial(_ffn_kernel, final_norm=final_norm, tf=TF_FFN, nsub=2)
    resident = lambda shape: pl.BlockSpec(shape, lambda i: (0, 0), pipeline_mode=pl.Buffered(1))
    return pl.pallas_call(
        kern,
        grid=(m // tm,),
        in_specs=[pl.BlockSpec((tm, D_MODEL), lambda i: (i, 0)),
                  _full((1, D_MODEL)),
                  resident(w_gu.shape),
                  resident(w_down.shape),
                  _full((1, D_MODEL))],
        out_specs=pl.BlockSpec((tm, D_MODEL), lambda i: (i, 0)),
        out_shape=jax.ShapeDtypeStruct((m, D_MODEL), F32),
        compiler_params=_cparams(("parallel",)),
        name="swiglu",
    )(x, g.reshape(1, D_MODEL), w_gu, w_down, g_final.reshape(1, D_MODEL))


def _head_sum(x, e_ref):
    x = x.astype(BF16)
    e = e_ref[...]
    width = e.shape[0]
    return jnp.concatenate([_dot(x[:, c * width:(c + 1) * width], e) for c in range(x.shape[1] // width)],
                           axis=1)


def _sigmoid(x):
    return 0.5 * jnp.tanh(0.5 * x) + 0.5


def _store_pairs(ref, val, *lead):
    for p in range(val.shape[1] // LANES):
        ref[(*lead, p)] = val[:, p * LANES:(p + 1) * LANES].astype(ref.dtype)


def _rwkv_pre_kernel(x_ref, xb_ref, xa_ref, g_ref, mu_ref, wr_ref, wk_ref, wv_ref, w1_ref, w2_ref, w0_ref,
                     a1_ref, a2_ref, a0_ref, g1_ref, g2_ref, kk_ref, ka_ref, rk_ref, e_ref,
                     sm_ref, r_out, v_out, kn_out, kd_out, ba_out, lw_out, gate_out, bonus_out, *, nts):
    it = pl.program_id(0) % nts
    g = g_ref[...]
    xn_bf = _rms(x_ref[...], g).astype(BF16)
    before = (_rms(xb_ref[...], g) * (it > 0).astype(F32)).astype(BF16)
    after = (_rms(xa_ref[...], g) * (it < nts - 1).astype(F32)).astype(BF16)
    xx_bf = _dot(sm_ref[...], jnp.concatenate([before, xn_bf, after], axis=0)).astype(BF16)
    mu = mu_ref[...].astype(BF16)
    mix = lambda i: xn_bf + xx_bf * mu[i:i + 1]

    tw = jnp.tanh(_dot(mix(1), w1_ref[...])).astype(BF16)
    ta = _dot(mix(4), a1_ref[...]).astype(BF16)
    tg = _sigmoid(_dot(mix(5), g1_ref[...])).astype(BF16)
    k_lin = _dot(mix(2), wk_ref[...])
    wls = [w0_ref[d:d + 1] + _dot(tw, w2_ref[d]) for d in range(2)]
    r = _dot(mix(0), wr_ref[...])
    for d in range(2):
        _store_pairs(lw_out, -DECAY_SCALE * _sigmoid(wls[d]), d)
    kn = k_lin * kk_ref[...]
    kn = kn * lax.rsqrt(jnp.maximum(_head_sum(kn * kn, e_ref), 1e-24))
    _store_pairs(kn_out, kn)
    als = [a0_ref[d:d + 1] + _dot(ta, a2_ref[d]) for d in range(2)]
    v = _dot(mix(3), wv_ref[...])
    ka = ka_ref[...]
    kd_sum = jnp.zeros_like(k_lin)
    for d in range(2):
        a = _sigmoid(als[d])
        kd = k_lin * (1.0 + (a - 1.0) * ka)
        kd_sum = kd_sum + kd
        _store_pairs(kd_out, kd, d)
        _store_pairs(ba_out, kn * a, d)
    gate_out[...] = _dot(tg, g2_ref[...]).astype(BF16)
    _store_pairs(r_out, r)
    _store_pairs(v_out, v)
    bonus_out[...] = _head_sum(r * kd_sum * rk_ref[...], e_ref) * v


def _rwkv_pre(x, g, p, batch, seq):
    m = x.shape[0]
    tm = TM_RW
    nts = seq // tm
    hb = 16
    nhb = m // hb
    kern = functools.partial(_rwkv_pre_kernel, nts=nts)
    row = lambda: pl.BlockSpec((tm, D_MODEL), lambda i: (i, 0))
    npair = D_MODEL // LANES
    pairs = lambda: pl.BlockSpec((npair, tm, LANES), lambda i: (0, i, 0))
    pairs2 = lambda: pl.BlockSpec((2, npair, tm, LANES), lambda i: (0, 0, i, 0))
    t = jnp.arange(tm)[:, None]
    c = jnp.arange(tm + 2 * hb)[None, :] - hb
    shift = (jnp.where(c == t, -1.0, 0.0) + jnp.where(jnp.abs(c - t) == 1, 0.5, 0.0)).astype(BF16)
    args = (x, x, x, g.reshape(1, D_MODEL), p['mu'], p['wr'], p['wk'], p['wv'], p['w1'], p['w2'], p['w0'],
            p['a1'], p['a2'], p['a0'], p['g1'], p['g2'], p['kk'], p['ka'], p['rk'], p['e'], shift)
    in_specs = [row(),
                pl.BlockSpec((hb, D_MODEL), lambda i: (jnp.maximum(i * (tm // hb) - 1, 0), 0)),
                pl.BlockSpec((hb, D_MODEL), lambda i: (jnp.minimum((i + 1) * (tm // hb), nhb - 1), 0))]
    in_specs += [_full(a.shape) for a in args[3:]]
    out_shape = (jax.ShapeDtypeStruct((npair, m, LANES), BF16),) * 3 + (
        jax.ShapeDtypeStruct((2, npair, m, LANES), BF16), jax.ShapeDtypeStruct((2, npair, m, LANES), BF16),
        jax.ShapeDtypeStruct((2, npair, m, LANES), F32),
        jax.ShapeDtypeStruct((m, D_MODEL), BF16), jax.ShapeDtypeStruct((m, D_MODEL), F32))
    out_specs = (pairs(), pairs(), pairs(), pairs2(), pairs2(), pairs2(), row(), row())
    return pl.pallas_call(
        kern, grid=(m // tm,), in_specs=in_specs, out_specs=out_specs, out_shape=out_shape,
        compiler_params=_cparams(("parallel",)), name="rwkv_pre",
    )(*args)


def _bd(x, first):
    z = jnp.zeros_like(x)
    return jnp.concatenate([jnp.where(first, x, z), jnp.where(first, z, x)], axis=0)


def _compact(full, first):
    return jnp.where(first, full[:CHUNK], full[CHUNK:])


def _wkv_masks(sgn):
    c = CHUNK
    row = lax.broadcasted_iota(jnp.int32, (c, c), 0)
    col = lax.broadcasted_iota(jnp.int32, (c, c), 1)
    tri = jnp.where((row - col) * sgn >= 0, 1.0, 0.0).astype(BF16)
    prow = lax.broadcasted_iota(jnp.int32, (c, LANES), 0)
    pcol = lax.broadcasted_iota(jnp.int32, (c, LANES), 1) & (c - 1)
    dlt = (prow - pcol) * sgn
    strict = dlt > 0
    same = lambda sh: (prow >> sh) == (pcol >> sh)
    levels = [jnp.where(strict & same(1), 1.0, 0.0)]
    sh = 1
    while (1 << sh) < c:
        levels.append(jnp.where(strict & same(sh + 1) & jnp.logical_not(same(sh)), 1.0, 0.0))
        sh += 1
    return dict(tri=tri, strict=strict, incl=dlt >= 0, eye=jnp.where(prow == pcol, 1.0, 0.0), levels=levels)


def _wkv_local(chains, first):
    c = CHUNK
    for s in chains:
        hl = _dot(s['mk']['tri'], jnp.concatenate(_split(s['lw']), axis=1))
        s['cum'] = hl[:, :LANES] + hl[:, LANES:]
    yield
    for s in chains:
        cum, lw = s['cum'], s['lw']
        total = jnp.sum(lw, axis=0, keepdims=True)
        p_inv = jnp.exp(-cum)
        p_hat = jnp.exp(total - cum)
        s['at'] = (-s['kn'] * jnp.exp(cum - lw)).astype(BF16)
        s['rt'] = (s['r'] * jnp.exp(cum)).astype(BF16)
        s['bt'] = (s['ba'] * p_inv).astype(BF16)
        s['kt'] = (s['kd'] * p_inv).astype(BF16)
        s['bh'] = (s['ba'] * p_hat).astype(BF16)
        s['kh'] = (s['kd'] * p_hat).astype(BF16)
        s['v_bf'] = s['v'].astype(BF16)
        s['p_end'] = jnp.exp(total)
        s['lhs'] = jnp.concatenate([s['at'], s['rt']], axis=0)
    yield
    for s in chains:
        s['sbk'] = _dot_nt(s['lhs'], jnp.concatenate([_bd(s['bt'], first), _bd(s['kt'], first)], axis=0))
    yield
    for s in chains:
        mk = s['mk']
        sb, sk = s['sbk'][:, :LANES], s['sbk'][:, LANES:]
        s['a_ab'] = jnp.where(mk['strict'], sb[:c], 0.0)
        s['a_rb'] = jnp.where(mk['incl'], sb[c:], 0.0).astype(BF16)
        s['a_kk'] = jnp.concatenate([jnp.where(mk['strict'], sk[:c], 0.0),
                                     jnp.where(mk['incl'], sk[c:], 0.0)], axis=0).astype(BF16)
        s['inv'] = mk['eye'] + s['a_ab'] * mk['levels'][0]
    yield
    for s in chains:
        s['av'] = _dot(s['a_kk'], _bd(s['v_bf'], first))
        s['y0'] = _compact(_dot_tn(s['v_bf'], s['kh']), first)
    yield
    for lvl in range(1, len(chains[0]['mk']['levels'])):
        for s in chains:
            s['inv_bf'] = s['inv'].astype(BF16)
            s['inner'] = _dot((s['a_ab'] * s['mk']['levels'][lvl]).astype(BF16), _bd(s['inv_bf'], first))
        yield
        for s in chains:
            s['inv'] = s['inv'] + _dot(s['inv_bf'], _bd(s['inner'].astype(BF16), first))
        yield
    for s in chains:
        rhs = jnp.concatenate([_bd(s['at'], first), _bd(s['av'][:c].astype(BF16), first)], axis=1)
        wu = _dot(s['inv'].astype(BF16), rhs)
        s['w1'] = wu[:, :LANES].astype(BF16)
        s['u0'] = wu[:, LANES:]
        s['o0'] = s['av'][c:]


def _wkv_seq(rows, ys, out_refs, first):
    c = CHUNK
    for row in rows:
        wrs = [_dot_nt(jnp.concatenate([s['w1'], s['rt']], axis=0), _bd(y.astype(BF16), first))
               for s, y in zip(row, ys)]
        yield
        us = [(wr[:c] + s['u0']).astype(BF16) for s, wr in zip(row, wrs)]
        for s, wr, u, (o_ref, p) in zip(row, wrs, us, out_refs):
            o_ref[p, s['sl'], :] = wr[c:] + _dot(s['a_rb'], _bd(u, first)) + s['o0']
        ys[:] = [s['p_end'] * y + _compact(_dot_tn(u, s['bh']), first) + s['y0'] for s, y, u in zip(row, ys, us)]
        yield


def _interleave(main, side, every):
    side_done = side is None
    for i, _ in enumerate(main):
        if not side_done and i % every == every - 1:
            side_done = next(side, StopIteration) is StopIteration
    if not side_done:
        for _ in side:
            pass


def _wkv_kernel(rf, vf, nf, kdf, baf, lwf, rb, vb, nb, kdb, bab, lwb, of_ref, ob_ref, state, *, ts, npairs):
    first = lax.broadcasted_iota(jnp.int32, (CHUNK, LANES), 1) < RWKV_HEAD

    @pl.when(pl.program_id(2) == 0)
    def _():
        state[...] = jnp.zeros_like(state)

    nchunk = ts // CHUNK
    names = ('r', 'v', 'kn', 'kd', 'ba', 'lw')
    dirs = ((_wkv_masks(1), (rf, vf, nf, kdf, baf, lwf), of_ref, list(range(nchunk))),
            (_wkv_masks(-1), (rb, vb, nb, kdb, bab, lwb), ob_ref, list(range(nchunk - 1, -1, -1))))
    scans = [(d, p) for d in range(2) for p in range(npairs)]
    out_refs = [(dirs[d][2], p) for d, p in scans]
    steps = []
    for k in range(nchunk):
        row = []
        for d, p in scans:
            mk, ins, _, order = dirs[d]
            sl = pl.ds(order[k] * CHUNK, CHUNK)
            chain = {n: ref[p, sl, :].astype(F32) for n, ref in zip(names, ins)}
            chain.update(mk=mk, sl=sl)
            row.append(chain)
        steps.append(row)
    ys = [state[d, p] for d, p in scans]
    groups = [steps[k:k + WKV_GROUP] for k in range(0, nchunk, WKV_GROUP)]
    pending = None
    for rows in groups:
        _interleave(_wkv_local([s for row in rows for s in row], first), pending, every=2)
        pending = _wkv_seq(rows, ys, out_refs, first)
    _interleave(pending, None, every=1)
    for (d, p), y in zip(scans, ys):
        state[d, p] = y


def _wkv(r, v, kn, kd, ba, lw, batch, seq):
    ts, npairs = TS_WKV, PAIRS_WKV
    nt = seq // ts
    npair = D_MODEL // LANES
    m = batch * seq
    fwd = lambda b, t: b * nt + t
    bwd = lambda b, t: b * nt + nt - 1 - t
    s3 = lambda at: pl.BlockSpec((npairs, ts, LANES), lambda b, g, t: (g, at(b, t), 0))
    s4 = lambda d, at: pl.BlockSpec((None, npairs, ts, LANES), lambda b, g, t: (d, g, at(b, t), 0))
    return pl.pallas_call(
        functools.partial(_wkv_kernel, ts=ts, npairs=npairs),
        grid=(batch, npair // npairs, nt),
        in_specs=[s3(fwd), s3(fwd), s3(fwd), s4(0, fwd), s4(0, fwd), s4(0, fwd),
                  s3(bwd), s3(bwd), s3(bwd), s4(1, bwd), s4(1, bwd), s4(1, bwd)],
        out_specs=(s3(fwd), s3(bwd)),
        out_shape=(jax.ShapeDtypeStruct((npair, m, LANES), F32),) * 2,
        scratch_shapes=[pltpu.VMEM((2, npairs, CHUNK, LANES), F32)],
        compiler_params=_cparams(("parallel", "parallel", "arbitrary")),
        name="wkv",
    )(r, v, kn, kd, ba, lw, r, v, kn, kd, ba, lw)


def _rwkv_post_kernel(of_ref, ob_ref, bonus_ref, gate_ref, lw_ref, lb_ref, e_ref, wo_ref, x_ref, out_ref):
    y = jnp.concatenate([of_ref[p] + ob_ref[p] for p in range(of_ref.shape[0])], axis=1)
    mean = _head_sum(y, e_ref) * (1.0 / RWKV_HEAD)
    yc = y - mean
    var = _head_sum(yc * yc, e_ref) * (1.0 / RWKV_HEAD)
    yn = yc * lax.rsqrt(var + GN_EPS)
    y2 = yn * lw_ref[...] + lb_ref[...] + bonus_ref[...]
    out_ref[...] = x_ref[...] + _dot((y2 * gate_ref[...].astype(F32)).astype(BF16), wo_ref[...])


def _rwkv_post(x, o_f, o_b, bonus, gate, p):
    m = x.shape[0]
    tm = TM
    row = lambda: pl.BlockSpec((tm, D_MODEL), lambda i: (i, 0))
    pairs = lambda: pl.BlockSpec((D_MODEL // LANES, tm, LANES), lambda i: (0, i, 0))
    return pl.pallas_call(
        _rwkv_post_kernel,
        grid=(m // tm,),
        in_specs=[pairs(), pairs(), row(), row(), _full((1, D_MODEL)), _full((1, D_MODEL)),
                  _full(p['e'].shape), _full((D_MODEL, D_MODEL)), row()],
        out_specs=row(),
        out_shape=jax.ShapeDtypeStruct((m, D_MODEL), F32),
        compiler_params=_cparams(("parallel",)),
        name="rwkv_post",
    )(o_f, o_b, bonus, gate, p['lnx_w'], p['lnx_b'], p['e'], p['wo'], x)


def _prep_odd(o, rw_mu, rw_wr, rw_wk, rw_wv, rw_wo, rw_w0, rw_w1, rw_w2, rw_a0, rw_a1, rw_a2, rw_g1, rw_g2,
              rw_kk, rw_ka, rw_rk, rw_lnx_w, rw_lnx_b):
    bf = lambda a: a.astype(BF16)

    def pad_dir(w):
        z = jnp.zeros_like(w[0])
        return jnp.stack([jnp.concatenate([w[0], z], axis=0), jnp.concatenate([z, w[1]], axis=0)])

    lane = jnp.arange(MXU_WIDTH) // RWKV_HEAD
    return dict(
        mu=rw_mu[o], wr=bf(rw_wr[o]), wk=bf(rw_wk[o]), wv=bf(rw_wv[o]), wo=bf(rw_wo[o]),
        w0=rw_w0[o], w1=bf(jnp.concatenate([rw_w1[o, 0], rw_w1[o, 1]], axis=1)), w2=bf(pad_dir(rw_w2[o])),
        a0=rw_a0[o], a1=bf(jnp.concatenate([rw_a1[o, 0], rw_a1[o, 1]], axis=1)), a2=bf(pad_dir(rw_a2[o])),
        g1=bf(rw_g1[o]), g2=bf(rw_g2[o]),
        kk=rw_kk[o].reshape(1, D_MODEL), ka=rw_ka[o].reshape(1, D_MODEL), rk=rw_rk[o].reshape(1, D_MODEL),
        lnx_w=rw_lnx_w[o].reshape(1, D_MODEL), lnx_b=rw_lnx_b[o].reshape(1, D_MODEL),
        e=(lane[:, None] == lane[None, :]).astype(BF16))


def _even_layer(x, g, w_in, w_out, conv_w, cos, sin, batch, seq):
    z1, z4, z16, zc = _even_proj(x, g, w_in, cos, sin, batch, seq)
    m = x.shape[0]
    o1, l1 = _dilated_branch(z1.reshape(3, batch, seq, A_WIDTH), seq)
    o4, l4 = _dilated_branch(z4.reshape(3, batch * 4, seq // 4, A_WIDTH), seq // 4)
    o16, l16 = _dilated_branch(z16.reshape(3, batch * 16, seq // 16, A_WIDTH), seq // 16)
    o1, l1 = o1.reshape(m, A_WIDTH), l1.reshape(m, A_WIDTH)
    d4 = lambda a: a.reshape(batch, 4, seq // 4, A_WIDTH)
    d16 = lambda a: a.reshape(batch, 16, seq // 16, A_WIDTH)
    return _even_out(x, o1, l1, d4(o4), d4(l4), d16(o16), d16(l16), zc, conv_w, w_out, batch, seq)


def _odd_layer(x, g, p, batch, seq):
    r, v, kn, kd, ba, lw, gate, bonus = _rwkv_pre(x, g, p, batch, seq)
    o_f, o_b = _wkv(r, v, kn, kd, ba, lw, batch, seq)
    return _rwkv_post(x, o_f, o_b, bonus, gate, p)


def _trunk(x, mem, w):
    batch, seq, _ = x.shape
    n_mem = mem.shape[1]
    x = x.reshape(batch * seq, D_MODEL)
    mem = mem.reshape(batch * n_mem, D_MODEL)
    cos, sin = _rope_tables(seq)
    for l in range(DEPTH):
        if l % 2 == 0:
            e = l // 2
            x = _even_layer(x, w['norm_mix'][l], w['ab_w_in'][e], w['ab_w_out'][e], w['ab_conv'][e],
                            cos, sin, batch, seq)
        else:
            x = _odd_layer(x, w['norm_mix'][l], w['odd'][l // 2], batch, seq)
        kv = _norm_matmul(mem, w['norm_mem'][l], w['ca_wkv'][l], n_mem, D_MODEL, BF16)
        x = _cross_attn(x, w['norm_cross'][l], kv, w['ca_wq'][l], w['ca_wo'][l], batch, seq, n_mem)
        x = _ffn(x, w['norm_ffn'][l], w['ffn_wgu'][l], w['ffn_wdown'][l], w['norm_final'], l == DEPTH - 1)
    return x.reshape(batch, seq, D_MODEL)


def kernel(x_prompt, x_sample, mem_prompt, mem_sample, norm_mix, norm_cross, norm_mem, norm_ffn, norm_final,
           ab_w_in, ab_w_out, ab_conv, rw_mu, rw_wr, rw_wk, rw_wv, rw_wo, rw_w0, rw_w1, rw_w2, rw_a0, rw_a1,
           rw_a2, rw_g1, rw_g2, rw_kk, rw_ka, rw_rk, rw_lnx_w, rw_lnx_b, ca_wq, ca_wkv, ca_wo, ffn_wgu,
           ffn_wdown):
    bf = lambda a: a.astype(BF16)
    w = dict(norm_mix=norm_mix, norm_cross=norm_cross, norm_mem=norm_mem, norm_ffn=norm_ffn,
             norm_final=norm_final, ab_w_in=bf(ab_w_in), ab_w_out=bf(ab_w_out), ab_conv=ab_conv,
             ca_wq=bf(ca_wq), ca_wkv=bf(ca_wkv), ca_wo=bf(ca_wo), ffn_wgu=bf(ffn_wgu), ffn_wdown=bf(ffn_wdown),
             odd=[_prep_odd(o, rw_mu, rw_wr, rw_wk, rw_wv, rw_wo, rw_w0, rw_w1, rw_w2, rw_a0, rw_a1, rw_a2,
                            rw_g1, rw_g2, rw_kk, rw_ka, rw_rk, rw_lnx_w, rw_lnx_b)
                  for o in range(rw_mu.shape[0])])
    return _trunk(x_prompt, mem_prompt, w), _trunk(x_sample, mem_sample, w)
```

```python
import functools

import jax
import jax.numpy as jnp
from jax import lax
from jax.experimental import pallas as pl
from jax.experimental.pallas import tpu as pltpu

F32 = jnp.float32
BF16 = jnp.bfloat16

D_MODEL = 1024
DEPTH = 4
HEAD_DIM = 64
A_WIDTH = 512
B_WIDTH = 512
DILATIONS = (1, 4, 16)
BAND = 64
ROPE_THETA = 500000.0
ROPE_DIM = 16
RWKV_HEAD = 64
GN_EPS = 64e-5
DECAY_SCALE = 0.6065306597126334
CA_HEADS = 4
CA_HEAD_DIM = 256
D_FF = 2816
RMS_EPS = 1e-6
NEG_INF = -1e30
CHUNK = 64
LANES = 128
MXU_WIDTH = 256

VMEM_LIMIT = 48 * 1024 * 1024

TM = 512
TM_RW = 256
TB_DIL = 512
TQ_DIL = 128
TF_FFN = 256
TS_WKV = 512
PAIRS_WKV = 4
WKV_GROUP = 2


def _cparams(sem):
    return pltpu.CompilerParams(dimension_semantics=sem, vmem_limit_bytes=VMEM_LIMIT)


def _full(shape):
    n = len(shape)
    return pl.BlockSpec(shape, lambda *_: (0,) * n)


def _rms(x, g):
    ms = jnp.mean(x * x, axis=-1, keepdims=True)
    return x * lax.rsqrt(ms + RMS_EPS) * g


def _dot(a, b):
    return jnp.dot(a, b, preferred_element_type=F32)


def _dot_nt(a, b):
    return lax.dot_general(a, b, (((1,), (1,)), ((), ())), preferred_element_type=F32)


def _dot_tn(a, b):
    return lax.dot_general(a, b, (((0,), (0,)), ((), ())), preferred_element_type=F32)


def _split(x):
    hi = x.astype(BF16)
    lo = (x - hi.astype(F32)).astype(BF16)
    return hi, lo


def _norm_matmul_kernel(x_ref, g_ref, w_ref, o_ref, xn_ref):
    @pl.when(pl.program_id(1) == 0)
    def _():
        xn_ref[...] = _rms(x_ref[...], g_ref[...]).astype(BF16)

    o_ref[...] = _dot(xn_ref[...], w_ref[...]).astype(o_ref.dtype)


def _norm_matmul(x, g, w, tm, tn, out_dtype):
    m, k = x.shape
    n = w.shape[1]
    return pl.pallas_call(
        _norm_matmul_kernel,
        grid=(m // tm, n // tn),
        in_specs=[pl.BlockSpec((tm, k), lambda i, j: (i, 0)),
                  pl.BlockSpec((1, k), lambda i, j: (0, 0)),
                  pl.BlockSpec((k, tn), lambda i, j: (0, j))],
        out_specs=pl.BlockSpec((tm, tn), lambda i, j: (i, j)),
        out_shape=jax.ShapeDtypeStruct((m, n), out_dtype),
        scratch_shapes=[pltpu.VMEM((tm, k), BF16)],
        compiler_params=_cparams(("parallel", "arbitrary")),
        name="norm_matmul",
    )(x, g.reshape(1, k), w)


def _even_proj_kernel(x_ref, g_ref, w_ref, cos_ref, sin_ref, p4_ref, p16_ref,
                      o1_ref, o4_ref, o16_ref, oc_ref, *, tm):
    xn = _rms(x_ref[...], g_ref[...]).astype(BF16)
    reps = A_WIDTH // LANES
    cos = jnp.concatenate([cos_ref[...]] * reps, axis=1)
    sin = jnp.concatenate([sin_ref[...]] * reps, axis=1)
    lane = lax.broadcasted_iota(jnp.int32, (tm, A_WIDTH), 1) & (HEAD_DIM - 1)
    half = ROPE_DIM // 2
    nparts = 3
    acc = None
    for j in range(nparts + 1):
        nxt = _dot(xn, w_ref[:, j * A_WIDTH:(j + 1) * A_WIDTH]) if j < nparts else None
        if j == nparts:
            oc_ref[...] = _dot(xn, w_ref[:, 3 * A_WIDTH:]).astype(BF16)
        if j > 0:
            a, part = acc, j - 1
            if part < 2:
                partner = jnp.where(lane < half, pltpu.roll(a, A_WIDTH - half, 1), pltpu.roll(a, half, 1))
                a = a * cos + partner * sin
                if part == 0:
                    a = a * (HEAD_DIM ** -0.5)
            a = a.astype(BF16)
            o1_ref[part] = a
            for d, p_ref, o_ref in ((4, p4_ref, o4_ref), (16, p16_ref, o16_ref)):
                perm = _dot(p_ref[...], a).astype(BF16)
                for r in range(d):
                    o_ref[part, r] = perm[r * (tm // d):(r + 1) * (tm // d)]
        acc = nxt


def _dedilate_perm(tm, d):
    i = jnp.arange(tm)
    src = (i % (tm // d)) * d + i // (tm // d)
    return (src[:, None] == jnp.arange(tm)[None, :]).astype(BF16)


def _even_proj(x, g, w_in, cos, sin, batch, seq):
    m = x.shape[0]
    tm = TM
    nts = seq // tm
    kern = functools.partial(_even_proj_kernel, tm=tm)
    out_shape = (jax.ShapeDtypeStruct((3, m, A_WIDTH), BF16),
                 jax.ShapeDtypeStruct((3, batch, 4, seq // 4, A_WIDTH), BF16),
                 jax.ShapeDtypeStruct((3, batch, 16, seq // 16, A_WIDTH), BF16),
                 jax.ShapeDtypeStruct((m, 3 * B_WIDTH), BF16))
    return pl.pallas_call(
        kern,
        grid=(m // tm,),
        in_specs=[pl.BlockSpec((tm, D_MODEL), lambda i: (i, 0)),
                  _full((1, D_MODEL)),
                  _full(w_in.shape),
                  pl.BlockSpec((tm, LANES), lambda i: (i % nts, 0)),
                  pl.BlockSpec((tm, LANES), lambda i: (i % nts, 0)),
                  _full((tm, tm)), _full((tm, tm))],
        out_specs=(pl.BlockSpec((3, tm, A_WIDTH), lambda i: (0, i, 0)),
                   pl.BlockSpec((3, None, 4, tm // 4, A_WIDTH), lambda i: (0, i // nts, 0, i % nts, 0)),
                   pl.BlockSpec((3, None, 16, tm // 16, A_WIDTH), lambda i: (0, i // nts, 0, i % nts, 0)),
                   pl.BlockSpec((tm, 3 * B_WIDTH), lambda i: (i, 0))),
        out_shape=out_shape,
        compiler_params=_cparams(("parallel",)),
        name="even_proj",
    )(x, g.reshape(1, D_MODEL), w_in, cos, sin, _dedilate_perm(tm, 4), _dedilate_perm(tm, 16))


def _rope_tables(seq):
    half = ROPE_DIM // 2
    inv = ROPE_THETA ** (-2.0 * jnp.arange(half, dtype=F32) / ROPE_DIM)
    ang = jnp.arange(seq, dtype=F32)[:, None] * inv[None, :]
    cos, sin = jnp.cos(ang), jnp.sin(ang)
    rest = HEAD_DIM - ROPE_DIM
    cos_h = jnp.concatenate([cos, cos, jnp.ones((seq, rest), F32)], axis=1)
    sin_h = jnp.concatenate([-sin, sin, jnp.zeros((seq, rest), F32)], axis=1)
    reps = LANES // HEAD_DIM
    return jnp.tile(cos_h, (1, reps)), jnp.tile(sin_h, (1, reps))


def _dil_kernel(q_ref, kp_ref, kc_ref, kn_ref, vp_ref, vc_ref, vn_ref, o_ref, lse_ref, *, tb, tq, length):
    qi = pl.program_id(1)
    span = tq + 2 * BAND
    k = jnp.concatenate([kp_ref[...], kc_ref[...], kn_ref[...]], axis=0)
    v = jnp.concatenate([vp_ref[...], vc_ref[...], vn_ref[...]], axis=0)
    row = lax.broadcasted_iota(jnp.int32, (tq, span), 0)
    col = lax.broadcasted_iota(jnp.int32, (tq, span), 1)
    rel = col - BAND - row
    band = (rel <= BAND) & (rel >= -BAND)
    first = lax.broadcasted_iota(jnp.int32, (tq, LANES), 1) < HEAD_DIM
    for s_idx in range(tb // tq):
        rows = slice(s_idx * tq, (s_idx + 1) * tq)
        keys = slice(s_idx * tq, s_idx * tq + span)
        key_pos = qi * tb + s_idx * tq - BAND + col
        valid = band & (key_pos >= 0) & (key_pos < length)
        for p in range(A_WIDTH // LANES):
            sl = slice(p * LANES, (p + 1) * LANES)
            qp, kp, vp = q_ref[rows, sl], k[keys, sl], v[keys, sl]
            outs, lses = [], []
            for sel in (first, jnp.logical_not(first)):
                s = _dot_nt(jnp.where(sel, qp, jnp.zeros_like(qp)), kp)
                s = jnp.where(valid, s, NEG_INF)
                mx = jnp.max(s, axis=-1, keepdims=True)
                e = jnp.exp(s - mx)
                den = jnp.sum(e, axis=-1, keepdims=True)
                outs.append(_dot(e.astype(BF16), vp) * (1.0 / den))
                lses.append(mx + jnp.log(den))
            o_ref[rows, sl] = jnp.where(first, outs[0], outs[1]).astype(BF16)
            lse_ref[rows, sl] = jnp.where(first, lses[0], lses[1])


def _dilated_branch(zd, length):
    g = zd.shape[1]
    tb = min(TB_DIL, length)
    tq = min(TQ_DIL, length)
    nb = length // BAND
    r = tb // BAND
    kern = functools.partial(_dil_kernel, tb=tb, tq=tq, length=length)

    def cur(which):
        return pl.BlockSpec((None, None, tb, A_WIDTH), lambda b, i: (which, b, i, 0))

    def prev(which):
        return pl.BlockSpec((None, None, BAND, A_WIDTH), lambda b, i: (which, b, jnp.maximum(i * r - 1, 0), 0))

    def nxt(which):
        return pl.BlockSpec((None, None, BAND, A_WIDTH), lambda b, i: (which, b, jnp.minimum((i + 1) * r, nb - 1), 0))

    return pl.pallas_call(
        kern,
        grid=(g, length // tb),
        in_specs=[cur(0), prev(1), cur(1), nxt(1), prev(2), cur(2), nxt(2)],
        out_specs=(pl.BlockSpec((None, tb, A_WIDTH), lambda b, i: (b, i, 0)),
                   pl.BlockSpec((None, tb, A_WIDTH), lambda b, i: (b, i, 0))),
        out_shape=(jax.ShapeDtypeStruct((g, length, A_WIDTH), BF16),
                   jax.ShapeDtypeStruct((g, length, A_WIDTH), F32)),
        compiler_params=_cparams(("parallel", "parallel")),
        name="dilated_attn",
    )(zd, zd, zd, zd, zd, zd, zd)


def _even_out_kernel(o1_ref, l1_ref, o4_ref, l4_ref, o16_ref, l16_ref, bg_ref, cg_ref, h_ref,
                     cgp_ref, hp_ref, cgn_ref, hn_ref, cw_ref, wa_ref, wb_ref, x_ref, out_ref,
                     s4o, s4l, s16o, s16l, *, tm, nts):
    it = pl.program_id(0) % nts
    nslab = A_WIDTH // LANES
    for c in range(nslab):
        sl = slice(c * LANES, (c + 1) * LANES)
        for r in range(4):
            s4o[c, pl.ds(r, tm // 4, stride=4), :] = o4_ref[r, :, sl].astype(F32)
            s4l[c, pl.ds(r, tm // 4, stride=4), :] = l4_ref[r, :, sl]
        for r in range(16):
            s16o[c, pl.ds(r, tm // 16, stride=16), :] = o16_ref[r, :, sl].astype(F32)
            s16l[c, pl.ds(r, tm // 16, stride=16), :] = l16_ref[r, :, sl]
    wide = lambda ref: jnp.concatenate([ref[c] for c in range(nslab)], axis=1)
    l1, l4, l16 = l1_ref[...], wide(s4l), wide(s16l)
    mx = jnp.maximum(jnp.maximum(l1, l4), l16)
    e1, e4, e16 = jnp.exp(l1 - mx), jnp.exp(l4 - mx), jnp.exp(l16 - mx)
    ya = (e1 * o1_ref[...].astype(F32) + e4 * wide(s4o) + e16 * wide(s16o)) * (1.0 / (e1 + e4 + e16))

    u = cg_ref[...].astype(F32) * h_ref[...].astype(F32)
    last = cgp_ref.shape[0] - 1
    u_before = (cgp_ref[...].astype(F32) * hp_ref[...].astype(F32))[last:last + 1]
    u_after = (cgn_ref[...].astype(F32) * hn_ref[...].astype(F32))[0:1]
    u_before = u_before * (it > 0).astype(F32)
    u_after = u_after * (it < nts - 1).astype(F32)
    row = lax.broadcasted_iota(jnp.int32, u.shape, 0)
    u_prev = jnp.where(row == 0, u_before, pltpu.roll(u, 1, 0))
    u_next = jnp.where(row == tm - 1, u_after, pltpu.roll(u, tm - 1, 0))
    cw = cw_ref[...]
    yb = bg_ref[...].astype(F32) * (cw[0:1] * u_prev + cw[1:2] * u + cw[2:3] * u_next)

    out_ref[...] = x_ref[...] + _dot(ya.astype(BF16), wa_ref[...]) + _dot(yb.astype(BF16), wb_ref[...])


def _even_out(x, o1, l1, o4, l4, o16, l16, zc, conv_w, w_out, batch, seq):
    m = x.shape[0]
    tm = TM
    nts = seq // tm
    hb = 16
    nhb = m // hb
    kern = functools.partial(_even_out_kernel, tm=tm, nts=nts)
    nat = lambda: pl.BlockSpec((tm, A_WIDTH), lambda i: (i, 0))
    dil = lambda d: pl.BlockSpec((None, d, tm // d, A_WIDTH), lambda i: (i // nts, 0, i % nts, 0))
    col = lambda c: pl.BlockSpec((tm, B_WIDTH), lambda i: (i, c))
    before = lambda c: pl.BlockSpec((hb, B_WIDTH), lambda i: (jnp.maximum(i * (tm // hb) - 1, 0), c))
    after = lambda c: pl.BlockSpec((hb, B_WIDTH), lambda i: (jnp.minimum((i + 1) * (tm // hb), nhb - 1), c))
    return pl.pallas_call(
        kern,
        grid=(m // tm,),
        in_specs=[nat(), nat(), dil(4), dil(4), dil(16), dil(16),
                  col(0), col(1), col(2), before(1), before(2), after(1), after(2),
                  _full((3, B_WIDTH)),
                  pl.BlockSpec((A_WIDTH, D_MODEL), lambda i: (0, 0)),
                  pl.BlockSpec((B_WIDTH, D_MODEL), lambda i: (1, 0)),
                  pl.BlockSpec((tm, D_MODEL), lambda i: (i, 0))],
        out_specs=pl.BlockSpec((tm, D_MODEL), lambda i: (i, 0)),
        out_shape=jax.ShapeDtypeStruct((m, D_MODEL), F32),
        scratch_shapes=[pltpu.VMEM((A_WIDTH // LANES, tm, LANES), F32)] * 4,
        compiler_params=_cparams(("parallel",)),
        name="even_out",
    )(o1, l1, o4, l4, o16, l16, zc, zc, zc, zc, zc, zc, zc, conv_w, w_out, w_out, x)


def _cross_kernel(x_ref, g_ref, k_ref, v_ref, wq_ref, wo_ref, out_ref, *, nsub):
    rows = x_ref.shape[0] // nsub
    g = g_ref[...]
    heads = [slice(h * CA_HEAD_DIM, (h + 1) * CA_HEAD_DIM) for h in range(CA_HEADS)]
    k = k_ref[...]
    v = v_ref[...]
    qs = []
    for i in range(nsub):
        xn = _rms(x_ref[i * rows:(i + 1) * rows, :], g).astype(BF16)
        qs.append((_dot(xn, wq_ref[...]) * (CA_HEAD_DIM ** -0.5)).astype(BF16))
    units = [(i, h) for i in range(nsub) for h in range(CA_HEADS)]
    scores, probs, outs = {}, {}, {}
    for step in range(len(units) + 2):
        if step < len(units):
            i, h = units[step]
            scores[step] = _dot_nt(qs[i][:, heads[h]], k[:, heads[h]])
        if 0 <= step - 1 < len(units):
            s = scores.pop(step - 1)
            e = jnp.exp(s - jnp.max(s, axis=-1, keepdims=True))
            probs[step - 1] = (e * (1.0 / jnp.sum(e, axis=-1, keepdims=True))).astype(BF16)
        if 0 <= step - 2 < len(units):
            i, h = units[step - 2]
            outs[i, h] = _dot(probs.pop(step - 2), v[:, heads[h]]).astype(BF16)
            if h == CA_HEADS - 1:
                o = jnp.concatenate([outs.pop((i, hh)) for hh in range(CA_HEADS)], axis=1)
                out_ref[i * rows:(i + 1) * rows, :] = x_ref[i * rows:(i + 1) * rows, :] + _dot(o, wo_ref[...])


def _cross_attn(x, g, kv, wq, wo, batch, seq, n_mem):
    m = x.shape[0]
    tq = TM
    nts = seq // tq
    return pl.pallas_call(
        functools.partial(_cross_kernel, nsub=2),
        grid=(m // tq,),
        in_specs=[pl.BlockSpec((tq, D_MODEL), lambda i: (i, 0)),
                  _full((1, D_MODEL)),
                  pl.BlockSpec((n_mem, D_MODEL), lambda i: (i // nts, 0)),
                  pl.BlockSpec((n_mem, D_MODEL), lambda i: (i // nts, 1)),
                  _full((D_MODEL, D_MODEL)),
                  _full((D_MODEL, D_MODEL))],
        out_specs=pl.BlockSpec((tq, D_MODEL), lambda i: (i, 0)),
        out_shape=jax.ShapeDtypeStruct((m, D_MODEL), F32),
        compiler_params=_cparams(("parallel",)),
        name="cross_attn",
    )(x, g.reshape(1, D_MODEL), kv, kv, wq, wo)


def _ffn_kernel(x_ref, g_ref, wgu_ref, wd_ref, gf_ref, out_ref, *, final_norm, tf, nsub):
    rows = x_ref.shape[0] // nsub
    g = g_ref[...]
    accs = [x_ref[i * rows:(i + 1) * rows, :] for i in range(nsub)]
    xns = [_rms(x, g).astype(BF16) for x in accs]
    nf = D_FF // tf
    acts = None
    for f in range(nf + 1):
        new_acts = []
        if f < nf:
            for i in range(nsub):
                gate = _dot(xns[i], wgu_ref[:, f * tf:(f + 1) * tf])
                up = _dot(xns[i], wgu_ref[:, D_FF + f * tf:D_FF + (f + 1) * tf])
                new_acts.append((gate * _sigmoid(gate) * up).astype(BF16))
        if f > 0:
            for i in range(nsub):
                accs[i] = accs[i] + _dot(acts[i], wd_ref[(f - 1) * tf:f * tf, :])
        acts = new_acts
    for i in range(nsub):
        y = _rms(accs[i], gf_ref[...]) if final_norm else accs[i]
        out_ref[i * rows:(i + 1) * rows, :] = y


def _ffn(x, g, w_gu, w_down, g_final, final_norm):
    m = x.shape[0]
    tm = TM
    kern = functools.partial(_ffn_kernel, final_norm=final_norm, tf=TF_FFN, nsub=2)
    resident = lambda shape: pl.BlockSpec(shape, lambda i: (0, 0), pipeline_mode=pl.Buffered(1))
    return pl.pallas_call(
        kern,
        grid=(m // tm,),
        in_specs=[pl.BlockSpec((tm, D_MODEL), lambda i: (i, 0)),
                  _full((1, D_MODEL)),
                  resident(w_gu.shape),
                  resident(w_down.shape),
                  _full((1, D_MODEL))],
        out_specs=pl.BlockSpec((tm, D_MODEL), lambda i: (i, 0)),
        out_shape=jax.ShapeDtypeStruct((m, D_MODEL), F32),
        compiler_params=_cparams(("parallel",)),
        name="swiglu",
    )(x, g.reshape(1, D_MODEL), w_gu, w_down, g_final.reshape(1, D_MODEL))


def _head_sum(x, e_ref):
    x = x.astype(BF16)
    e = e_ref[...]
    width = e.shape[0]
    return jnp.concatenate([_dot(x[:, c * width:(c + 1) * width], e) for c in range(x.shape[1] // width)],
                           axis=1)


def _sigmoid(x):
    return 0.5 * jnp.tanh(0.5 * x) + 0.5


def _store_pairs(ref, val, *lead):
    for p in range(val.shape[1] // LANES):
        ref[(*lead, p)] = val[:, p * LANES:(p + 1) * LANES].astype(ref.dtype)


def _rwkv_pre_kernel(x_ref, xb_ref, xa_ref, g_ref, mu_ref, wr_ref, wk_ref, wv_ref, w1_ref, w2_ref, w0_ref,
                     a1_ref, a2_ref, a0_ref, g1_ref, g2_ref, kk_ref, ka_ref, rk_ref, e_ref,
                     sm_ref, r_out, v_out, kn_out, kd_out, ba_out, lw_out, gate_out, bonus_out, *, nts):
    it = pl.program_id(0) % nts
    g = g_ref[...]
    xn_bf = _rms(x_ref[...], g).astype(BF16)
    before = (_rms(xb_ref[...], g) * (it > 0).astype(F32)).astype(BF16)
    after = (_rms(xa_ref[...], g) * (it < nts - 1).astype(F32)).astype(BF16)
    xx_bf = _dot(sm_ref[...], jnp.concatenate([before, xn_bf, after], axis=0)).astype(BF16)
    mu = mu_ref[...].astype(BF16)
    mix = lambda i: xn_bf + xx_bf * mu[i:i + 1]

    tw = jnp.tanh(_dot(mix(1), w1_ref[...])).astype(BF16)
    ta = _dot(mix(4), a1_ref[...]).astype(BF16)
    tg = _sigmoid(_dot(mix(5), g1_ref[...])).astype(BF16)
    k_lin = _dot(mix(2), wk_ref[...])
    wls = [w0_ref[d:d + 1] + _dot(tw, w2_ref[d]) for d in range(2)]
    r = _dot(mix(0), wr_ref[...])
    for d in range(2):
        _store_pairs(lw_out, -DECAY_SCALE * _sigmoid(wls[d]), d)
    kn = k_lin * kk_ref[...]
    kn = kn * lax.rsqrt(jnp.maximum(_head_sum(kn * kn, e_ref), 1e-24))
    _store_pairs(kn_out, kn)
    als = [a0_ref[d:d + 1] + _dot(ta, a2_ref[d]) for d in range(2)]
    v = _dot(mix(3), wv_ref[...])
    ka = ka_ref[...]
    kd_sum = jnp.zeros_like(k_lin)
    for d in range(2):
        a = _sigmoid(als[d])
        kd = k_lin * (1.0 + (a - 1.0) * ka)
        kd_sum = kd_sum + kd
        _store_pairs(kd_out, kd, d)
        _store_pairs(ba_out, kn * a, d)
    gate_out[...] = _dot(tg, g2_ref[...]).astype(BF16)
    _store_pairs(r_out, r)
    _store_pairs(v_out, v)
    bonus_out[...] = (_head_sum(r * kd_sum * rk_ref[...], e_ref) * v).astype(BF16)


def _rwkv_pre(x, g, p, batch, seq):
    m = x.shape[0]
    tm = TM_RW
    nts = seq // tm
    hb = 16
    nhb = m // hb
    kern = functools.partial(_rwkv_pre_kernel, nts=nts)
    row = lambda: pl.BlockSpec((tm, D_MODEL), lambda i: (i, 0))
    npair = D_MODEL // LANES
    pairs = lambda: pl.BlockSpec((npair, tm, LANES), lambda i: (0, i, 0))
    pairs2 = lambda: pl.BlockSpec((2, npair, tm, LANES), lambda i: (0, 0, i, 0))
    t = jnp.arange(tm)[:, None]
    c = jnp.arange(tm + 2 * hb)[None, :] - hb
    shift = (jnp.where(c == t, -1.0, 0.0) + jnp.where(jnp.abs(c - t) == 1, 0.5, 0.0)).astype(BF16)
    args = (x, x, x, g.reshape(1, D_MODEL), p['mu'], p['wr'], p['wk'], p['wv'], p['w1'], p['w2'], p['w0'],
            p['a1'], p['a2'], p['a0'], p['g1'], p['g2'], p['kk'], p['ka'], p['rk'], p['e'], shift)
    in_specs = [row(),
                pl.BlockSpec((hb, D_MODEL), lambda i: (jnp.maximum(i * (tm // hb) - 1, 0), 0)),
                pl.BlockSpec((hb, D_MODEL), lambda i: (jnp.minimum((i + 1) * (tm // hb), nhb - 1), 0))]
    in_specs += [_full(a.shape) for a in args[3:]]
    out_shape = (jax.ShapeDtypeStruct((npair, m, LANES), BF16),) * 3 + (
        jax.ShapeDtypeStruct((2, npair, m, LANES), BF16), jax.ShapeDtypeStruct((2, npair, m, LANES), BF16),
        jax.ShapeDtypeStruct((2, npair, m, LANES), F32),
        jax.ShapeDtypeStruct((m, D_MODEL), BF16), jax.ShapeDtypeStruct((m, D_MODEL), BF16))
    out_specs = (pairs(), pairs(), pairs(), pairs2(), pairs2(), pairs2(), row(), row())
    return pl.pallas_call(
        kern, grid=(m // tm,), in_specs=in_specs, out_specs=out_specs, out_shape=out_shape,
        compiler_params=_cparams(("parallel",)), name="rwkv_pre",
    )(*args)


def _bd(x, first):
    z = jnp.zeros_like(x)
    return jnp.concatenate([jnp.where(first, x, z), jnp.where(first, z, x)], axis=0)


def _compact(full, first):
    return jnp.where(first, full[:CHUNK], full[CHUNK:])


def _wkv_masks(sgn):
    c = CHUNK
    row = lax.broadcasted_iota(jnp.int32, (c, c), 0)
    col = lax.broadcasted_iota(jnp.int32, (c, c), 1)
    tri = jnp.where((row - col) * sgn >= 0, 1.0, 0.0).astype(BF16)
    prow = lax.broadcasted_iota(jnp.int32, (c, LANES), 0)
    pcol = lax.broadcasted_iota(jnp.int32, (c, LANES), 1) & (c - 1)
    dlt = (prow - pcol) * sgn
    strict = dlt > 0
    same = lambda sh: (prow >> sh) == (pcol >> sh)
    levels = [jnp.where(strict & same(1), 1.0, 0.0)]
    sh = 1
    while (1 << sh) < c:
        levels.append(jnp.where(strict & same(sh + 1) & jnp.logical_not(same(sh)), 1.0, 0.0))
        sh += 1
    return dict(tri=tri, strict=strict, incl=dlt >= 0, eye=jnp.where(prow == pcol, 1.0, 0.0), levels=levels)


def _wkv_local(chains, first):
    c = CHUNK
    for s in chains:
        hl = _dot(s['mk']['tri'], jnp.concatenate(_split(s['lw']), axis=1))
        s['cum'] = hl[:, :LANES] + hl[:, LANES:]
    yield
    for s in chains:
        cum, lw = s['cum'], s['lw']
        total = jnp.sum(lw, axis=0, keepdims=True)
        p_inv = jnp.exp(-cum)
        p_hat = jnp.exp(total - cum)
        s['at'] = (-s['kn'] * jnp.exp(cum - lw)).astype(BF16)
        s['rt'] = (s['r'] * jnp.exp(cum)).astype(BF16)
        s['bt'] = (s['ba'] * p_inv).astype(BF16)
        s['kt'] = (s['kd'] * p_inv).astype(BF16)
        s['bh'] = (s['ba'] * p_hat).astype(BF16)
        s['kh'] = (s['kd'] * p_hat).astype(BF16)
        s['v_bf'] = s['v'].astype(BF16)
        s['p_end'] = jnp.exp(total)
        s['lhs'] = jnp.concatenate([s['at'], s['rt']], axis=0)
    yield
    for s in chains:
        s['sbk'] = _dot_nt(s['lhs'], jnp.concatenate([_bd(s['bt'], first), _bd(s['kt'], first)], axis=0))
    yield
    for s in chains:
        mk = s['mk']
        sb, sk = s['sbk'][:, :LANES], s['sbk'][:, LANES:]
        s['a_ab'] = jnp.where(mk['strict'], sb[:c], 0.0)
        s['a_rb'] = jnp.where(mk['incl'], sb[c:], 0.0).astype(BF16)
        s['a_kk'] = jnp.concatenate([jnp.where(mk['strict'], sk[:c], 0.0),
                                     jnp.where(mk['incl'], sk[c:], 0.0)], axis=0).astype(BF16)
        s['inv'] = mk['eye'] + s['a_ab'] * mk['levels'][0]
    yield
    for s in chains:
        s['av'] = _dot(s['a_kk'], _bd(s['v_bf'], first))
        s['y0'] = _compact(_dot_tn(s['v_bf'], s['kh']), first)
    yield
    for lvl in range(1, len(chains[0]['mk']['levels'])):
        for s in chains:
            s['inv_bf'] = s['inv'].astype(BF16)
            s['inner'] = _dot((s['a_ab'] * s['mk']['levels'][lvl]).astype(BF16), _bd(s['inv_bf'], first))
        yield
        for s in chains:
            s['inv'] = s['inv'] + _dot(s['inv_bf'], _bd(s['inner'].astype(BF16), first))
        yield
    for s in chains:
        rhs = jnp.concatenate([_bd(s['at'], first), _bd(s['av'][:c].astype(BF16), first)], axis=1)
        wu = _dot(s['inv'].astype(BF16), rhs)
        s['w1'] = wu[:, :LANES].astype(BF16)
        s['u0'] = wu[:, LANES:]
        s['o0'] = s['av'][c:]


def _wkv_seq(rows, ys, out_refs, first):
    c = CHUNK
    for row in rows:
        wrs = [_dot_nt(jnp.concatenate([s['w1'], s['rt']], axis=0), _bd(y.astype(BF16), first))
               for s, y in zip(row, ys)]
        yield
        us = [(wr[:c] + s['u0']).astype(BF16) for s, wr in zip(row, wrs)]
        for s, wr, u, (o_ref, p) in zip(row, wrs, us, out_refs):
            o_ref[p, s['sl'], :] = (wr[c:] + _dot(s['a_rb'], _bd(u, first)) + s['o0']).astype(o_ref.dtype)
        ys[:] = [s['p_end'] * y + _compact(_dot_tn(u, s['bh']), first) + s['y0'] for s, y, u in zip(row, ys, us)]
        yield


def _interleave(main, side, every):
    side_done = side is None
    for i, _ in enumerate(main):
        if not side_done and i % every == every - 1:
            side_done = next(side, StopIteration) is StopIteration
    if not side_done:
        for _ in side:
            pass


def _wkv_kernel(rf, vf, nf, kdf, baf, lwf, rb, vb, nb, kdb, bab, lwb, of_ref, ob_ref, state, *, ts, npairs):
    first = lax.broadcasted_iota(jnp.int32, (CHUNK, LANES), 1) < RWKV_HEAD

    @pl.when(pl.program_id(2) == 0)
    def _():
        state[...] = jnp.zeros_like(state)

    nchunk = ts // CHUNK
    names = ('r', 'v', 'kn', 'kd', 'ba', 'lw')
    dirs = ((_wkv_masks(1), (rf, vf, nf, kdf, baf, lwf), of_ref, list(range(nchunk))),
            (_wkv_masks(-1), (rb, vb, nb, kdb, bab, lwb), ob_ref, list(range(nchunk - 1, -1, -1))))
    scans = [(d, p) for d in range(2) for p in range(npairs)]
    out_refs = [(dirs[d][2], p) for d, p in scans]
    steps = []
    for k in range(nchunk):
        row = []
        for d, p in scans:
            mk, ins, _, order = dirs[d]
            sl = pl.ds(order[k] * CHUNK, CHUNK)
            chain = {n: ref[p, sl, :].astype(F32) for n, ref in zip(names, ins)}
            chain.update(mk=mk, sl=sl)
            row.append(chain)
        steps.append(row)
    ys = [state[d, p] for d, p in scans]
    groups = [steps[k:k + WKV_GROUP] for k in range(0, nchunk, WKV_GROUP)]
    pending = None
    for rows in groups:
        _interleave(_wkv_local([s for row in rows for s in row], first), pending, every=2)
        pending = _wkv_seq(rows, ys, out_refs, first)
    _interleave(pending, None, every=1)
    for (d, p), y in zip(scans, ys):
        state[d, p] = y


def _wkv(r, v, kn, kd, ba, lw, batch, seq):
    ts, npairs = TS_WKV, PAIRS_WKV
    nt = seq // ts
    npair = D_MODEL // LANES
    m = batch * seq
    fwd = lambda b, t: b * nt + t
    bwd = lambda b, t: b * nt + nt - 1 - t
    s3 = lambda at: pl.BlockSpec((npairs, ts, LANES), lambda b, g, t: (g, at(b, t), 0))
    s4 = lambda d, at: pl.BlockSpec((None, npairs, ts, LANES), lambda b, g, t: (d, g, at(b, t), 0))
    return pl.pallas_call(
        functools.partial(_wkv_kernel, ts=ts, npairs=npairs),
        grid=(batch, npair // npairs, nt),
        in_specs=[s3(fwd), s3(fwd), s3(fwd), s4(0, fwd), s4(0, fwd), s4(0, fwd),
                  s3(bwd), s3(bwd), s3(bwd), s4(1, bwd), s4(1, bwd), s4(1, bwd)],
        out_specs=(s3(fwd), s3(bwd)),
        out_shape=(jax.ShapeDtypeStruct((npair, m, LANES), BF16),) * 2,
        scratch_shapes=[pltpu.VMEM((2, npairs, CHUNK, LANES), F32)],
        compiler_params=_cparams(("parallel", "parallel", "arbitrary")),
        name="wkv",
    )(r, v, kn, kd, ba, lw, r, v, kn, kd, ba, lw)


def _rwkv_post_kernel(of_ref, ob_ref, bonus_ref, gate_ref, lw_ref, lb_ref, e_ref, wo_ref, x_ref, out_ref):
    y = jnp.concatenate([of_ref[p].astype(F32) + ob_ref[p].astype(F32) for p in range(of_ref.shape[0])], axis=1)
    mean = _head_sum(y, e_ref) * (1.0 / RWKV_HEAD)
    yc = y - mean
    var = _head_sum(yc * yc, e_ref) * (1.0 / RWKV_HEAD)
    yn = yc * lax.rsqrt(var + GN_EPS)
    y2 = yn * lw_ref[...] + lb_ref[...] + bonus_ref[...].astype(F32)
    out_ref[...] = x_ref[...] + _dot((y2 * gate_ref[...].astype(F32)).astype(BF16), wo_ref[...])


def _rwkv_post(x, o_f, o_b, bonus, gate, p):
    m = x.shape[0]
    tm = TM
    row = lambda: pl.BlockSpec((tm, D_MODEL), lambda i: (i, 0))
    pairs = lambda: pl.BlockSpec((D_MODEL // LANES, tm, LANES), lambda i: (0, i, 0))
    return pl.pallas_call(
        _rwkv_post_kernel,
        grid=(m // tm,),
        in_specs=[pairs(), pairs(), row(), row(), _full((1, D_MODEL)), _full((1, D_MODEL)),
                  _full(p['e'].shape), _full((D_MODEL, D_MODEL)), row()],
        out_specs=row(),
        out_shape=jax.ShapeDtypeStruct((m, D_MODEL), F32),
        compiler_params=_cparams(("parallel",)),
        name="rwkv_post",
    )(o_f, o_b, bonus, gate, p['lnx_w'], p['lnx_b'], p['e'], p['wo'], x)


def _prep_odd(o, rw_mu, rw_wr, rw_wk, rw_wv, rw_wo, rw_w0, rw_w1, rw_w2, rw_a0, rw_a1, rw_a2, rw_g1, rw_g2,
              rw_kk, rw_ka, rw_rk, rw_lnx_w, rw_lnx_b):
    bf = lambda a: a.astype(BF16)

    def pad_dir(w):
        z = jnp.zeros_like(w[0])
        return jnp.stack([jnp.concatenate([w[0], z], axis=0), jnp.concatenate([z, w[1]], axis=0)])

    lane = jnp.arange(MXU_WIDTH) // RWKV_HEAD
    return dict(
        mu=rw_mu[o], wr=bf(rw_wr[o]), wk=bf(rw_wk[o]), wv=bf(rw_wv[o]), wo=bf(rw_wo[o]),
        w0=rw_w0[o], w1=bf(jnp.concatenate([rw_w1[o, 0], rw_w1[o, 1]], axis=1)), w2=bf(pad_dir(rw_w2[o])),
        a0=rw_a0[o], a1=bf(jnp.concatenate([rw_a1[o, 0], rw_a1[o, 1]], axis=1)), a2=bf(pad_dir(rw_a2[o])),
        g1=bf(rw_g1[o]), g2=bf(rw_g2[o]),
        kk=rw_kk[o].reshape(1, D_MODEL), ka=rw_ka[o].reshape(1, D_MODEL), rk=rw_rk[o].reshape(1, D_MODEL),
        lnx_w=rw_lnx_w[o].reshape(1, D_MODEL), lnx_b=rw_lnx_b[o].reshape(1, D_MODEL),
        e=(lane[:, None] == lane[None, :]).astype(BF16))


def _even_layer(x, g, w_in, w_out, conv_w, cos, sin, batch, seq):
    z1, z4, z16, zc = _even_proj(x, g, w_in, cos, sin, batch, seq)
    m = x.shape[0]
    o1, l1 = _dilated_branch(z1.reshape(3, batch, seq, A_WIDTH), seq)
    o4, l4 = _dilated_branch(z4.reshape(3, batch * 4, seq // 4, A_WIDTH), seq // 4)
    o16, l16 = _dilated_branch(z16.reshape(3, batch * 16, seq // 16, A_WIDTH), seq // 16)
    o1, l1 = o1.reshape(m, A_WIDTH), l1.reshape(m, A_WIDTH)
    d4 = lambda a: a.reshape(batch, 4, seq // 4, A_WIDTH)
    d16 = lambda a: a.reshape(batch, 16, seq // 16, A_WIDTH)
    return _even_out(x, o1, l1, d4(o4), d4(l4), d16(o16), d16(l16), zc, conv_w, w_out, batch, seq)


def _odd_layer(x, g, p, batch, seq):
    r, v, kn, kd, ba, lw, gate, bonus = _rwkv_pre(x, g, p, batch, seq)
    o_f, o_b = _wkv(r, v, kn, kd, ba, lw, batch, seq)
    return _rwkv_post(x, o_f, o_b, bonus, gate, p)


def _trunk(x, mem, w):
    batch, seq, _ = x.shape
    n_mem = mem.shape[1]
    x = x.reshape(batch * seq, D_MODEL)
    mem = mem.reshape(batch * n_mem, D_MODEL)
    cos, sin = _rope_tables(seq)
    for l in range(DEPTH):
        if l % 2 == 0:
            e = l // 2
            x = _even_layer(x, w['norm_mix'][l], w['ab_w_in'][e], w['ab_w_out'][e], w['ab_conv'][e],
                            cos, sin, batch, seq)
        else:
            x = _odd_layer(x, w['norm_mix'][l], w['odd'][l // 2], batch, seq)
        kv = _norm_matmul(mem, w['norm_mem'][l], w['ca_wkv'][l], n_mem, D_MODEL, BF16)
        x = _cross_attn(x, w['norm_cross'][l], kv, w['ca_wq'][l], w['ca_wo'][l], batch, seq, n_mem)
        x = _ffn(x, w['norm_ffn'][l], w['ffn_wgu'][l], w['ffn_wdown'][l], w['norm_final'], l == DEPTH - 1)
    return x.reshape(batch, seq, D_MODEL)


def kernel(x_prompt, x_sample, mem_prompt, mem_sample, norm_mix, norm_cross, norm_mem, norm_ffn, norm_final,
           ab_w_in, ab_w_out, ab_conv, rw_mu, rw_wr, rw_wk, rw_wv, rw_wo, rw_w0, rw_w1, rw_w2, rw_a0, rw_a1,
           rw_a2, rw_g1, rw_g2, rw_kk, rw_ka, rw_rk, rw_lnx_w, rw_lnx_b, ca_wq, ca_wkv, ca_wo, ffn_wgu,
           ffn_wdown):
    bf = lambda a: a.astype(BF16)
    w = dict(norm_mix=norm_mix, norm_cross=norm_cross, norm_mem=norm_mem, norm_ffn=norm_ffn,
             norm_final=norm_final, ab_w_in=bf(ab_w_in), ab_w_out=bf(ab_w_out), ab_conv=ab_conv,
             ca_wq=bf(ca_wq), ca_wkv=bf(ca_wkv), ca_wo=bf(ca_wo), ffn_wgu=bf(ffn_wgu), ffn_wdown=bf(ffn_wdown),
             odd=[_prep_odd(o, rw_mu, rw_wr, rw_wk, rw_wv, rw_wo, rw_w0, rw_w1, rw_w2, rw_a0, rw_a1, rw_a2,
                            rw_g1, rw_g2, rw_kk, rw_ka, rw_rk, rw_lnx_w, rw_lnx_b)
                  for o in range(rw_mu.shape[0])])
    return _trunk(x_prompt, mem_prompt, w), _trunk(x_sample, mem_sample, w)
```

```python
import functools

import jax
import jax.numpy as jnp
from jax import lax
from jax.experimental import pallas as pl
from jax.experimental.pallas import tpu as pltpu

F32 = jnp.float32
BF16 = jnp.bfloat16

D_MODEL = 1024
DEPTH = 4
HEAD_DIM = 64
A_WIDTH = 512
B_WIDTH = 512
DILATIONS = (1, 4, 16)
BAND = 64
ROPE_THETA = 500000.0
ROPE_DIM = 16
RWKV_HEAD = 64
GN_EPS = 64e-5
DECAY_SCALE = 0.6065306597126334
CA_HEADS = 4
CA_HEAD_DIM = 256
D_FF = 2816
RMS_EPS = 1e-6
NEG_INF = -1e30
CHUNK = 64
LANES = 128
MXU_WIDTH = 256

VMEM_LIMIT = 48 * 1024 * 1024

TM = 512
TM_RW = 256
TB_DIL = 512
TQ_DIL = 128
DIL_LAG = 1
TF_FFN = 256
TS_WKV = 512
PAIRS_WKV = 4
WKV_GROUP = 2


def _cparams(sem):
    return pltpu.CompilerParams(dimension_semantics=sem, vmem_limit_bytes=VMEM_LIMIT)


def _full(shape):
    n = len(shape)
    return pl.BlockSpec(shape, lambda *_: (0,) * n)


def _rms(x, g):
    ms = jnp.mean(x * x, axis=-1, keepdims=True)
    return x * lax.rsqrt(ms + RMS_EPS) * g


def _dot(a, b):
    return jnp.dot(a, b, preferred_element_type=F32)


def _dot_nt(a, b):
    return lax.dot_general(a, b, (((1,), (1,)), ((), ())), preferred_element_type=F32)


def _dot_tn(a, b):
    return lax.dot_general(a, b, (((0,), (0,)), ((), ())), preferred_element_type=F32)


def _split(x):
    hi = x.astype(BF16)
    lo = (x - hi.astype(F32)).astype(BF16)
    return hi, lo


def _norm_matmul_kernel(x_ref, g_ref, w_ref, o_ref, xn_ref):
    @pl.when(pl.program_id(1) == 0)
    def _():
        xn_ref[...] = _rms(x_ref[...], g_ref[...]).astype(BF16)

    o_ref[...] = _dot(xn_ref[...], w_ref[...]).astype(o_ref.dtype)


def _norm_matmul(x, g, w, tm, tn, out_dtype):
    m, k = x.shape
    n = w.shape[1]
    return pl.pallas_call(
        _norm_matmul_kernel,
        grid=(m // tm, n // tn),
        in_specs=[pl.BlockSpec((tm, k), lambda i, j: (i, 0)),
                  pl.BlockSpec((1, k), lambda i, j: (0, 0)),
                  pl.BlockSpec((k, tn), lambda i, j: (0, j))],
        out_specs=pl.BlockSpec((tm, tn), lambda i, j: (i, j)),
        out_shape=jax.ShapeDtypeStruct((m, n), out_dtype),
        scratch_shapes=[pltpu.VMEM((tm, k), BF16)],
        compiler_params=_cparams(("parallel", "arbitrary")),
        name="norm_matmul",
    )(x, g.reshape(1, k), w)


def _even_proj_kernel(x_ref, g_ref, w_ref, cos_ref, sin_ref, p4_ref, p16_ref,
                      o1_ref, o4_ref, o16_ref, oc_ref, *, tm):
    xn = _rms(x_ref[...], g_ref[...]).astype(BF16)
    reps = A_WIDTH // LANES
    cos = jnp.concatenate([cos_ref[...]] * reps, axis=1)
    sin = jnp.concatenate([sin_ref[...]] * reps, axis=1)
    lane = lax.broadcasted_iota(jnp.int32, (tm, A_WIDTH), 1) & (HEAD_DIM - 1)
    half = ROPE_DIM // 2
    nparts = 3
    acc = None
    for j in range(nparts + 1):
        nxt = _dot(xn, w_ref[:, j * A_WIDTH:(j + 1) * A_WIDTH]) if j < nparts else None
        if j == nparts:
            oc_ref[...] = _dot(xn, w_ref[:, 3 * A_WIDTH:]).astype(BF16)
        if j > 0:
            a, part = acc, j - 1
            if part < 2:
                partner = jnp.where(lane < half, pltpu.roll(a, A_WIDTH - half, 1), pltpu.roll(a, half, 1))
                a = a * cos + partner * sin
                if part == 0:
                    a = a * (HEAD_DIM ** -0.5)
            a = a.astype(BF16)
            o1_ref[part] = a
            for d, p_ref, o_ref in ((4, p4_ref, o4_ref), (16, p16_ref, o16_ref)):
                perm = _dot(p_ref[...], a).astype(BF16)
                for r in range(d):
                    o_ref[part, r] = perm[r * (tm // d):(r + 1) * (tm // d)]
        acc = nxt


def _dedilate_perm(tm, d):
    i = jnp.arange(tm)
    src = (i % (tm // d)) * d + i // (tm // d)
    return (src[:, None] == jnp.arange(tm)[None, :]).astype(BF16)


def _even_proj(x, g, w_in, cos, sin, batch, seq):
    m = x.shape[0]
    tm = TM
    nts = seq // tm
    kern = functools.partial(_even_proj_kernel, tm=tm)
    out_shape = (jax.ShapeDtypeStruct((3, m, A_WIDTH), BF16),
                 jax.ShapeDtypeStruct((3, batch, 4, seq // 4, A_WIDTH), BF16),
                 jax.ShapeDtypeStruct((3, batch, 16, seq // 16, A_WIDTH), BF16),
                 jax.ShapeDtypeStruct((m, 3 * B_WIDTH), BF16))
    return pl.pallas_call(
        kern,
        grid=(m // tm,),
        in_specs=[pl.BlockSpec((tm, D_MODEL), lambda i: (i, 0)),
                  _full((1, D_MODEL)),
                  _full(w_in.shape),
                  pl.BlockSpec((tm, LANES), lambda i: (i % nts, 0)),
                  pl.BlockSpec((tm, LANES), lambda i: (i % nts, 0)),
                  _full((tm, tm)), _full((tm, tm))],
        out_specs=(pl.BlockSpec((3, tm, A_WIDTH), lambda i: (0, i, 0)),
                   pl.BlockSpec((3, None, 4, tm // 4, A_WIDTH), lambda i: (0, i // nts, 0, i % nts, 0)),
                   pl.BlockSpec((3, None, 16, tm // 16, A_WIDTH), lambda i: (0, i // nts, 0, i % nts, 0)),
                   pl.BlockSpec((tm, 3 * B_WIDTH), lambda i: (i, 0))),
        out_shape=out_shape,
        compiler_params=_cparams(("parallel",)),
        name="even_proj",
    )(x, g.reshape(1, D_MODEL), w_in, cos, sin, _dedilate_perm(tm, 4), _dedilate_perm(tm, 16))


def _rope_tables(seq):
    half = ROPE_DIM // 2
    inv = ROPE_THETA ** (-2.0 * jnp.arange(half, dtype=F32) / ROPE_DIM)
    ang = jnp.arange(seq, dtype=F32)[:, None] * inv[None, :]
    cos, sin = jnp.cos(ang), jnp.sin(ang)
    rest = HEAD_DIM - ROPE_DIM
    cos_h = jnp.concatenate([cos, cos, jnp.ones((seq, rest), F32)], axis=1)
    sin_h = jnp.concatenate([-sin, sin, jnp.zeros((seq, rest), F32)], axis=1)
    reps = LANES // HEAD_DIM
    return jnp.tile(cos_h, (1, reps)), jnp.tile(sin_h, (1, reps))


def _dil_kernel(q_ref, kp_ref, kc_ref, kn_ref, vp_ref, vc_ref, vn_ref, o_ref, lse_ref, *, tb, tq, length):
    qi = pl.program_id(1)
    span = tq + 2 * BAND
    k = jnp.concatenate([kp_ref[...], kc_ref[...], kn_ref[...]], axis=0)
    v = jnp.concatenate([vp_ref[...], vc_ref[...], vn_ref[...]], axis=0)
    row = lax.broadcasted_iota(jnp.int32, (tq, span), 0)
    col = lax.broadcasted_iota(jnp.int32, (tq, span), 1)
    rel = col - BAND - row
    band = (rel <= BAND) & (rel >= -BAND)
    first = lax.broadcasted_iota(jnp.int32, (tq, LANES), 1) < HEAD_DIM
    units = [(s_idx, p, hh) for s_idx in range(tb // tq) for p in range(A_WIDTH // LANES) for hh in range(2)]
    lag = DIL_LAG
    scores, maxes, probs, results = {}, {}, {}, {}
    for n in range(len(units) + 3 * lag):
        if n < len(units):
            s_idx, p, hh = units[n]
            sl = slice(p * LANES, (p + 1) * LANES)
            qp = q_ref[s_idx * tq:(s_idx + 1) * tq, sl]
            sel = first if hh == 0 else jnp.logical_not(first)
            key_pos = qi * tb + s_idx * tq - BAND + col
            valid = band & (key_pos >= 0) & (key_pos < length)
            s = _dot_nt(jnp.where(sel, qp, jnp.zeros_like(qp)), k[s_idx * tq:s_idx * tq + span, sl])
            scores[n] = jnp.where(valid, s, NEG_INF)
        m = n - lag
        if 0 <= m < len(units):
            maxes[m] = jnp.max(scores[m], axis=-1, keepdims=True)
        m = n - 2 * lag
        if 0 <= m < len(units):
            mx = maxes.pop(m)
            e = jnp.exp(scores.pop(m) - mx)
            den = jnp.sum(e, axis=-1, keepdims=True)
            probs[m] = (e.astype(BF16), den, mx)
        m = n - 3 * lag
        if 0 <= m < len(units):
            s_idx, p, hh = units[m]
            sl = slice(p * LANES, (p + 1) * LANES)
            e, den, mx = probs.pop(m)
            results[hh] = (_dot(e, v[s_idx * tq:s_idx * tq + span, sl]) * (1.0 / den), mx + jnp.log(den))
            if hh == 1:
                rows = slice(s_idx * tq, (s_idx + 1) * tq)
                o_ref[rows, sl] = jnp.where(first, results[0][0], results[1][0]).astype(BF16)
                lse_ref[rows, sl] = jnp.where(first, results[0][1], results[1][1])


def _dilated_branch(zd, length):
    g = zd.shape[1]
    tb = min(TB_DIL, length)
    tq = min(TQ_DIL, length)
    nb = length // BAND
    r = tb // BAND
    kern = functools.partial(_dil_kernel, tb=tb, tq=tq, length=length)

    def cur(which):
        return pl.BlockSpec((None, None, tb, A_WIDTH), lambda b, i: (which, b, i, 0))

    def prev(which):
        return pl.BlockSpec((None, None, BAND, A_WIDTH), lambda b, i: (which, b, jnp.maximum(i * r - 1, 0), 0))

    def nxt(which):
        return pl.BlockSpec((None, None, BAND, A_WIDTH), lambda b, i: (which, b, jnp.minimum((i + 1) * r, nb - 1), 0))

    return pl.pallas_call(
        kern,
        grid=(g, length // tb),
        in_specs=[cur(0), prev(1), cur(1), nxt(1), prev(2), cur(2), nxt(2)],
        out_specs=(pl.BlockSpec((None, tb, A_WIDTH), lambda b, i: (b, i, 0)),
                   pl.BlockSpec((None, tb, A_WIDTH), lambda b, i: (b, i, 0))),
        out_shape=(jax.ShapeDtypeStruct((g, length, A_WIDTH), BF16),
                   jax.ShapeDtypeStruct((g, length, A_WIDTH), F32)),
        compiler_params=_cparams(("parallel", "parallel")),
        name="dilated_attn",
    )(zd, zd, zd, zd, zd, zd, zd)


def _even_out_kernel(o1_ref, l1_ref, o4_ref, l4_ref, o16_ref, l16_ref, bg_ref, cg_ref, h_ref,
                     cgp_ref, hp_ref, cgn_ref, hn_ref, cw_ref, wa_ref, wb_ref, x_ref, out_ref,
                     s4o, s4l, s16o, s16l, *, tm, nts):
    it = pl.program_id(0) % nts
    nslab = A_WIDTH // LANES
    for c in range(nslab):
        sl = slice(c * LANES, (c + 1) * LANES)
        for r in range(4):
            s4o[c, pl.ds(r, tm // 4, stride=4), :] = o4_ref[r, :, sl].astype(F32)
            s4l[c, pl.ds(r, tm // 4, stride=4), :] = l4_ref[r, :, sl]
        for r in range(16):
            s16o[c, pl.ds(r, tm // 16, stride=16), :] = o16_ref[r, :, sl].astype(F32)
            s16l[c, pl.ds(r, tm // 16, stride=16), :] = l16_ref[r, :, sl]
    wide = lambda ref: jnp.concatenate([ref[c] for c in range(nslab)], axis=1)
    l1, l4, l16 = l1_ref[...], wide(s4l), wide(s16l)
    mx = jnp.maximum(jnp.maximum(l1, l4), l16)
    e1, e4, e16 = jnp.exp(l1 - mx), jnp.exp(l4 - mx), jnp.exp(l16 - mx)
    ya = (e1 * o1_ref[...].astype(F32) + e4 * wide(s4o) + e16 * wide(s16o)) * (1.0 / (e1 + e4 + e16))

    u = cg_ref[...].astype(F32) * h_ref[...].astype(F32)
    last = cgp_ref.shape[0] - 1
    u_before = (cgp_ref[...].astype(F32) * hp_ref[...].astype(F32))[last:last + 1]
    u_after = (cgn_ref[...].astype(F32) * hn_ref[...].astype(F32))[0:1]
    u_before = u_before * (it > 0).astype(F32)
    u_after = u_after * (it < nts - 1).astype(F32)
    row = lax.broadcasted_iota(jnp.int32, u.shape, 0)
    u_prev = jnp.where(row == 0, u_before, pltpu.roll(u, 1, 0))
    u_next = jnp.where(row == tm - 1, u_after, pltpu.roll(u, tm - 1, 0))
    cw = cw_ref[...]
    yb = bg_ref[...].astype(F32) * (cw[0:1] * u_prev + cw[1:2] * u + cw[2:3] * u_next)

    out_ref[...] = x_ref[...] + _dot(ya.astype(BF16), wa_ref[...]) + _dot(yb.astype(BF16), wb_ref[...])


def _even_tail_kernel(*refs, tm, nts, nsub):
    mixer_in, cross_in, out_ref, scratch = refs[:17], refs[17:22], refs[22], refs[23:]
    x_mid = scratch[4]
    _even_out_kernel(*mixer_in, x_mid, *scratch[:4], tm=tm, nts=nts)
    _cross_kernel(x_mid, *cross_in, out_ref, nsub=nsub)


def _cross_specs(seq, n_mem, tm):
    nts = seq // tm
    return [_full((1, D_MODEL)),
            pl.BlockSpec((n_mem, D_MODEL), lambda i: (i // nts, 0)),
            pl.BlockSpec((n_mem, D_MODEL), lambda i: (i // nts, 1)),
            _full((D_MODEL, D_MODEL)),
            _full((D_MODEL, D_MODEL))]


def _even_tail(x, o1, l1, o4, l4, o16, l16, zc, conv_w, w_out, cross, batch, seq):
    g_cross, kv, wq, wo, n_mem = cross
    m = x.shape[0]
    tm = TM
    nts = seq // tm
    hb = 16
    nhb = m // hb
    kern = functools.partial(_even_tail_kernel, tm=tm, nts=nts, nsub=2)
    nat = lambda: pl.BlockSpec((tm, A_WIDTH), lambda i: (i, 0))
    dil = lambda d: pl.BlockSpec((None, d, tm // d, A_WIDTH), lambda i: (i // nts, 0, i % nts, 0))
    col = lambda c: pl.BlockSpec((tm, B_WIDTH), lambda i: (i, c))
    before = lambda c: pl.BlockSpec((hb, B_WIDTH), lambda i: (jnp.maximum(i * (tm // hb) - 1, 0), c))
    after = lambda c: pl.BlockSpec((hb, B_WIDTH), lambda i: (jnp.minimum((i + 1) * (tm // hb), nhb - 1), c))
    return pl.pallas_call(
        kern,
        grid=(m // tm,),
        in_specs=[nat(), nat(), dil(4), dil(4), dil(16), dil(16),
                  col(0), col(1), col(2), before(1), before(2), after(1), after(2),
                  _full((3, B_WIDTH)),
                  pl.BlockSpec((A_WIDTH, D_MODEL), lambda i: (0, 0)),
                  pl.BlockSpec((B_WIDTH, D_MODEL), lambda i: (1, 0)),
                  pl.BlockSpec((tm, D_MODEL), lambda i: (i, 0))] + _cross_specs(seq, n_mem, tm),
        out_specs=pl.BlockSpec((tm, D_MODEL), lambda i: (i, 0)),
        out_shape=jax.ShapeDtypeStruct((m, D_MODEL), F32),
        scratch_shapes=[pltpu.VMEM((A_WIDTH // LANES, tm, LANES), F32)] * 4 + [pltpu.VMEM((tm, D_MODEL), F32)],
        compiler_params=_cparams(("parallel",)),
        name="even_tail",
    )(o1, l1, o4, l4, o16, l16, zc, zc, zc, zc, zc, zc, zc, conv_w, w_out, w_out, x,
      g_cross.reshape(1, D_MODEL), kv, kv, wq, wo)


def _cross_kernel(x_ref, g_ref, k_ref, v_ref, wq_ref, wo_ref, out_ref, *, nsub):
    rows = x_ref.shape[0] // nsub
    g = g_ref[...]
    heads = [slice(h * CA_HEAD_DIM, (h + 1) * CA_HEAD_DIM) for h in range(CA_HEADS)]
    k = k_ref[...]
    v = v_ref[...]
    qs = []
    for i in range(nsub):
        xn = _rms(x_ref[i * rows:(i + 1) * rows, :], g).astype(BF16)
        qs.append((_dot(xn, wq_ref[...]) * (CA_HEAD_DIM ** -0.5)).astype(BF16))
    units = [(i, h) for i in range(nsub) for h in range(CA_HEADS)]
    scores, probs, outs = {}, {}, {}
    for step in range(len(units) + 2):
        if step < len(units):
            i, h = units[step]
            scores[step] = _dot_nt(qs[i][:, heads[h]], k[:, heads[h]])
        if 0 <= step - 1 < len(units):
            s = scores.pop(step - 1)
            e = jnp.exp(s - jnp.max(s, axis=-1, keepdims=True))
            probs[step - 1] = (e * (1.0 / jnp.sum(e, axis=-1, keepdims=True))).astype(BF16)
        if 0 <= step - 2 < len(units):
            i, h = units[step - 2]
            outs[i, h] = _dot(probs.pop(step - 2), v[:, heads[h]]).astype(BF16)
            if h == CA_HEADS - 1:
                o = jnp.concatenate([outs.pop((i, hh)) for hh in range(CA_HEADS)], axis=1)
                out_ref[i * rows:(i + 1) * rows, :] = x_ref[i * rows:(i + 1) * rows, :] + _dot(o, wo_ref[...])


def _ffn_kernel(x_ref, g_ref, wgu_ref, wd_ref, gf_ref, out_ref, *, final_norm, tf, nsub):
    rows = x_ref.shape[0] // nsub
    g = g_ref[...]
    accs = [x_ref[i * rows:(i + 1) * rows, :] for i in range(nsub)]
    xns = [_rms(x, g).astype(BF16) for x in accs]
    nf = D_FF // tf
    acts = None
    for f in range(nf + 1):
        new_acts = []
        if f < nf:
            for i in range(nsub):
                gate = _dot(xns[i], wgu_ref[:, f * tf:(f + 1) * tf])
                up = _dot(xns[i], wgu_ref[:, D_FF + f * tf:D_FF + (f + 1) * tf])
                new_acts.append((gate * _sigmoid(gate) * up).astype(BF16))
        if f > 0:
            for i in range(nsub):
                accs[i] = accs[i] + _dot(acts[i], wd_ref[(f - 1) * tf:f * tf, :])
        acts = new_acts
    for i in range(nsub):
        y = _rms(accs[i], gf_ref[...]) if final_norm else accs[i]
        out_ref[i * rows:(i + 1) * rows, :] = y


def _ffn(x, g, w_gu, w_down, g_final, final_norm):
    m = x.shape[0]
    tm = TM
    kern = functools.partial(_ffn_kernel, final_norm=final_norm, tf=TF_FFN, nsub=2)
    resident = lambda shape: pl.BlockSpec(shape, lambda i: (0, 0), pipeline_mode=pl.Buffered(1))
    return pl.pallas_call(
        kern,
        grid=(m // tm,),
        in_specs=[pl.BlockSpec((tm, D_MODEL), lambda i: (i, 0)),
                  _full((1, D_MODEL)),
                  resident(w_gu.shape),
                  resident(w_down.shape),
                  _full((1, D_MODEL))],
        out_specs=pl.BlockSpec((tm, D_MODEL), lambda i: (i, 0)),
        out_shape=jax.ShapeDtypeStruct((m, D_MODEL), F32),
        compiler_params=_cparams(("parallel",)),
        name="swiglu",
    )(x, g.reshape(1, D_MODEL), w_gu, w_down, g_final.reshape(1, D_MODEL))


def _head_sum(x, e_ref):
    x = x.astype(BF16)
    e = e_ref[...]
    width = e.shape[0]
    return jnp.concatenate([_dot(x[:, c * width:(c + 1) * width], e) for c in range(x.shape[1] // width)],
                           axis=1)


def _sigmoid(x):
    return 0.5 * jnp.tanh(0.5 * x) + 0.5


def _store_pairs(ref, val, *lead):
    for p in range(val.shape[1] // LANES):
        ref[(*lead, p)] = val[:, p * LANES:(p + 1) * LANES].astype(ref.dtype)


def _rwkv_pre_kernel(x_ref, xb_ref, xa_ref, g_ref, mu_ref, wr_ref, wk_ref, wv_ref, w1_ref, w2_ref, w0_ref,
                     a1_ref, a2_ref, a0_ref, g1_ref, g2_ref, kk_ref, ka_ref, rk_ref, e_ref,
                     sm_ref, r_out, v_out, kn_out, kd_out, ba_out, lw_out, gate_out, bonus_out, *, nts):
    it = pl.program_id(0) % nts
    g = g_ref[...]
    xn_bf = _rms(x_ref[...], g).astype(BF16)
    before = (_rms(xb_ref[...], g) * (it > 0).astype(F32)).astype(BF16)
    after = (_rms(xa_ref[...], g) * (it < nts - 1).astype(F32)).astype(BF16)
    xx_bf = _dot(sm_ref[...], jnp.concatenate([before, xn_bf, after], axis=0)).astype(BF16)
    mu = mu_ref[...].astype(BF16)
    mix = lambda i: xn_bf + xx_bf * mu[i:i + 1]

    tw = jnp.tanh(_dot(mix(1), w1_ref[...])).astype(BF16)
    ta = _dot(mix(4), a1_ref[...]).astype(BF16)
    tg = _sigmoid(_dot(mix(5), g1_ref[...])).astype(BF16)
    k_lin = _dot(mix(2), wk_ref[...])
    wls = [w0_ref[d:d + 1] + _dot(tw, w2_ref[d]) for d in range(2)]
    r = _dot(mix(0), wr_ref[...])
    for d in range(2):
        _store_pairs(lw_out, -DECAY_SCALE * _sigmoid(wls[d]), d)
    kn = k_lin * kk_ref[...]
    kn = kn * lax.rsqrt(jnp.maximum(_head_sum(kn * kn, e_ref), 1e-24))
    _store_pairs(kn_out, kn)
    als = [a0_ref[d:d + 1] + _dot(ta, a2_ref[d]) for d in range(2)]
    v = _dot(mix(3), wv_ref[...])
    ka = ka_ref[...]
    kd_sum = jnp.zeros_like(k_lin)
    for d in range(2):
        a = _sigmoid(als[d])
        kd = k_lin * (1.0 + (a - 1.0) * ka)
        kd_sum = kd_sum + kd
        _store_pairs(kd_out, kd, d)
        _store_pairs(ba_out, kn * a, d)
    gate_out[...] = _dot(tg, g2_ref[...]).astype(BF16)
    _store_pairs(r_out, r)
    _store_pairs(v_out, v)
    bonus_out[...] = (_head_sum(r * kd_sum * rk_ref[...], e_ref) * v).astype(BF16)


def _rwkv_pre(x, g, p, batch, seq):
    m = x.shape[0]
    tm = TM_RW
    nts = seq // tm
    hb = 16
    nhb = m // hb
    kern = functools.partial(_rwkv_pre_kernel, nts=nts)
    row = lambda: pl.BlockSpec((tm, D_MODEL), lambda i: (i, 0))
    npair = D_MODEL // LANES
    pairs = lambda: pl.BlockSpec((npair, tm, LANES), lambda i: (0, i, 0))
    pairs2 = lambda: pl.BlockSpec((2, npair, tm, LANES), lambda i: (0, 0, i, 0))
    t = jnp.arange(tm)[:, None]
    c = jnp.arange(tm + 2 * hb)[None, :] - hb
    shift = (jnp.where(c == t, -1.0, 0.0) + jnp.where(jnp.abs(c - t) == 1, 0.5, 0.0)).astype(BF16)
    args = (x, x, x, g.reshape(1, D_MODEL), p['mu'], p['wr'], p['wk'], p['wv'], p['w1'], p['w2'], p['w0'],
            p['a1'], p['a2'], p['a0'], p['g1'], p['g2'], p['kk'], p['ka'], p['rk'], p['e'], shift)
    in_specs = [row(),
                pl.BlockSpec((hb, D_MODEL), lambda i: (jnp.maximum(i * (tm // hb) - 1, 0), 0)),
                pl.BlockSpec((hb, D_MODEL), lambda i: (jnp.minimum((i + 1) * (tm // hb), nhb - 1), 0))]
    in_specs += [_full(a.shape) for a in args[3:]]
    out_shape = (jax.ShapeDtypeStruct((npair, m, LANES), BF16),) * 3 + (
        jax.ShapeDtypeStruct((2, npair, m, LANES), BF16), jax.ShapeDtypeStruct((2, npair, m, LANES), BF16),
        jax.ShapeDtypeStruct((2, npair, m, LANES), F32),
        jax.ShapeDtypeStruct((m, D_MODEL), BF16), jax.ShapeDtypeStruct((m, D_MODEL), BF16))
    out_specs = (pairs(), pairs(), pairs(), pairs2(), pairs2(), pairs2(), row(), row())
    return pl.pallas_call(
        kern, grid=(m // tm,), in_specs=in_specs, out_specs=out_specs, out_shape=out_shape,
        compiler_params=_cparams(("parallel",)), name="rwkv_pre",
    )(*args)


def _bd(x, first):
    z = jnp.zeros_like(x)
    return jnp.concatenate([jnp.where(first, x, z), jnp.where(first, z, x)], axis=0)


def _compact(full, first):
    return jnp.where(first, full[:CHUNK], full[CHUNK:])


def _wkv_masks(sgn):
    c = CHUNK
    row = lax.broadcasted_iota(jnp.int32, (c, c), 0)
    col = lax.broadcasted_iota(jnp.int32, (c, c), 1)
    tri = jnp.where((row - col) * sgn >= 0, 1.0, 0.0).astype(BF16)
    prow = lax.broadcasted_iota(jnp.int32, (c, LANES), 0)
    pcol = lax.broadcasted_iota(jnp.int32, (c, LANES), 1) & (c - 1)
    dlt = (prow - pcol) * sgn
    strict = dlt > 0
    same = lambda sh: (prow >> sh) == (pcol >> sh)
    levels = [jnp.where(strict & same(1), 1.0, 0.0)]
    sh = 1
    while (1 << sh) < c:
        levels.append(jnp.where(strict & same(sh + 1) & jnp.logical_not(same(sh)), 1.0, 0.0))
        sh += 1
    return dict(tri=tri, strict=strict, incl=dlt >= 0, eye=jnp.where(prow == pcol, 1.0, 0.0), levels=levels)


def _wkv_local(chains, first):
    c = CHUNK
    for s in chains:
        hl = _dot(s['mk']['tri'], jnp.concatenate(_split(s['lw']), axis=1))
        s['cum'] = hl[:, :LANES] + hl[:, LANES:]
    yield
    for s in chains:
        cum, lw = s['cum'], s['lw']
        total = jnp.sum(lw, axis=0, keepdims=True)
        p_inv = jnp.exp(-cum)
        p_hat = jnp.exp(total - cum)
        s['at'] = (-s['kn'] * jnp.exp(cum - lw)).astype(BF16)
        s['rt'] = (s['r'] * jnp.exp(cum)).astype(BF16)
        s['bt'] = (s['ba'] * p_inv).astype(BF16)
        s['kt'] = (s['kd'] * p_inv).astype(BF16)
        s['bh'] = (s['ba'] * p_hat).astype(BF16)
        s['kh'] = (s['kd'] * p_hat).astype(BF16)
        s['v_bf'] = s['v'].astype(BF16)
        s['p_end'] = jnp.exp(total)
        s['lhs'] = jnp.concatenate([s['at'], s['rt']], axis=0)
    yield
    for s in chains:
        s['sbk'] = _dot_nt(s['lhs'], jnp.concatenate([_bd(s['bt'], first), _bd(s['kt'], first)], axis=0))
    yield
    for s in chains:
        mk = s['mk']
        sb, sk = s['sbk'][:, :LANES], s['sbk'][:, LANES:]
        s['a_ab'] = jnp.where(mk['strict'], sb[:c], 0.0)
        s['a_rb'] = jnp.where(mk['incl'], sb[c:], 0.0).astype(BF16)
        s['a_kk'] = jnp.concatenate([jnp.where(mk['strict'], sk[:c], 0.0),
                                     jnp.where(mk['incl'], sk[c:], 0.0)], axis=0).astype(BF16)
        s['inv'] = mk['eye'] + s['a_ab'] * mk['levels'][0]
    yield
    for s in chains:
        s['av'] = _dot(s['a_kk'], _bd(s['v_bf'], first))
        s['y0'] = _compact(_dot_tn(s['v_bf'], s['kh']), first)
    yield
    for lvl in range(1, len(chains[0]['mk']['levels'])):
        for s in chains:
            s['inv_bf'] = s['inv'].astype(BF16)
            s['inner'] = _dot((s['a_ab'] * s['mk']['levels'][lvl]).astype(BF16), _bd(s['inv_bf'], first))
        yield
        for s in chains:
            s['inv'] = s['inv'] + _dot(s['inv_bf'], _bd(s['inner'].astype(BF16), first))
        yield
    for s in chains:
        rhs = jnp.concatenate([_bd(s['at'], first), _bd(s['av'][:c].astype(BF16), first)], axis=1)
        wu = _dot(s['inv'].astype(BF16), rhs)
        s['w1'] = wu[:, :LANES].astype(BF16)
        s['u0'] = wu[:, LANES:]
        s['o0'] = s['av'][c:]


def _wkv_seq(rows, ys, out_refs, first):
    c = CHUNK
    for row in rows:
        wrs = [_dot_nt(jnp.concatenate([s['w1'], s['rt']], axis=0), _bd(y.astype(BF16), first))
               for s, y in zip(row, ys)]
        yield
        us = [(wr[:c] + s['u0']).astype(BF16) for s, wr in zip(row, wrs)]
        for s, wr, u, (o_ref, p) in zip(row, wrs, us, out_refs):
            o_ref[p, s['sl'], :] = (wr[c:] + _dot(s['a_rb'], _bd(u, first)) + s['o0']).astype(o_ref.dtype)
        ys[:] = [s['p_end'] * y + _compact(_dot_tn(u, s['bh']), first) + s['y0'] for s, y, u in zip(row, ys, us)]
        yield


def _interleave(main, side, every):
    side_done = side is None
    for i, _ in enumerate(main):
        if not side_done and i % every == every - 1:
            side_done = next(side, StopIteration) is StopIteration
    if not side_done:
        for _ in side:
            pass


def _wkv_kernel(rf, vf, nf, kdf, baf, lwf, rb, vb, nb, kdb, bab, lwb, of_ref, ob_ref, state, *, ts, npairs):
    first = lax.broadcasted_iota(jnp.int32, (CHUNK, LANES), 1) < RWKV_HEAD

    @pl.when(pl.program_id(2) == 0)
    def _():
        state[...] = jnp.zeros_like(state)

    nchunk = ts // CHUNK
    names = ('r', 'v', 'kn', 'kd', 'ba', 'lw')
    dirs = ((_wkv_masks(1), (rf, vf, nf, kdf, baf, lwf), of_ref, list(range(nchunk))),
            (_wkv_masks(-1), (rb, vb, nb, kdb, bab, lwb), ob_ref, list(range(nchunk - 1, -1, -1))))
    scans = [(d, p) for d in range(2) for p in range(npairs)]
    out_refs = [(dirs[d][2], p) for d, p in scans]
    steps = []
    for k in range(nchunk):
        row = []
        for d, p in scans:
            mk, ins, _, order = dirs[d]
            sl = pl.ds(order[k] * CHUNK, CHUNK)
            chain = {n: ref[p, sl, :].astype(F32) for n, ref in zip(names, ins)}
            chain.update(mk=mk, sl=sl)
            row.append(chain)
        steps.append(row)
    ys = [state[d, p] for d, p in scans]
    groups = [steps[k:k + WKV_GROUP] for k in range(0, nchunk, WKV_GROUP)]
    pending = None
    for rows in groups:
        _interleave(_wkv_local([s for row in rows for s in row], first), pending, every=2)
        pending = _wkv_seq(rows, ys, out_refs, first)
    _interleave(pending, None, every=1)
    for (d, p), y in zip(scans, ys):
        state[d, p] = y


def _wkv(r, v, kn, kd, ba, lw, batch, seq):
    ts, npairs = TS_WKV, PAIRS_WKV
    nt = seq // ts
    npair = D_MODEL // LANES
    m = batch * seq
    fwd = lambda b, t: b * nt + t
    bwd = lambda b, t: b * nt + nt - 1 - t
    s3 = lambda at: pl.BlockSpec((npairs, ts, LANES), lambda b, g, t: (g, at(b, t), 0))
    s4 = lambda d, at: pl.BlockSpec((None, npairs, ts, LANES), lambda b, g, t: (d, g, at(b, t), 0))
    return pl.pallas_call(
        functools.partial(_wkv_kernel, ts=ts, npairs=npairs),
        grid=(batch, npair // npairs, nt),
        in_specs=[s3(fwd), s3(fwd), s3(fwd), s4(0, fwd), s4(0, fwd), s4(0, fwd),
                  s3(bwd), s3(bwd), s3(bwd), s4(1, bwd), s4(1, bwd), s4(1, bwd)],
        out_specs=(s3(fwd), s3(bwd)),
        out_shape=(jax.ShapeDtypeStruct((npair, m, LANES), BF16),) * 2,
        scratch_shapes=[pltpu.VMEM((2, npairs, CHUNK, LANES), F32)],
        compiler_params=_cparams(("parallel", "parallel", "arbitrary")),
        name="wkv",
    )(r, v, kn, kd, ba, lw, r, v, kn, kd, ba, lw)


def _rwkv_post_kernel(of_ref, ob_ref, bonus_ref, gate_ref, lw_ref, lb_ref, e_ref, wo_ref, x_ref, out_ref):
    y = jnp.concatenate([of_ref[p].astype(F32) + ob_ref[p].astype(F32) for p in range(of_ref.shape[0])], axis=1)
    mean = _head_sum(y, e_ref) * (1.0 / RWKV_HEAD)
    yc = y - mean
    var = _head_sum(yc * yc, e_ref) * (1.0 / RWKV_HEAD)
    yn = yc * lax.rsqrt(var + GN_EPS)
    y2 = yn * lw_ref[...] + lb_ref[...] + bonus_ref[...].astype(F32)
    out_ref[...] = x_ref[...] + _dot((y2 * gate_ref[...].astype(F32)).astype(BF16), wo_ref[...])


def _odd_tail_kernel(*refs, nsub):
    mixer_in, cross_in, out_ref, x_mid = refs[:9], refs[9:14], refs[14], refs[15]
    _rwkv_post_kernel(*mixer_in, x_mid)
    _cross_kernel(x_mid, *cross_in, out_ref, nsub=nsub)


def _odd_tail(x, o_f, o_b, bonus, gate, p, cross, seq):
    g_cross, kv, wq, wo, n_mem = cross
    m = x.shape[0]
    tm = TM
    row = lambda: pl.BlockSpec((tm, D_MODEL), lambda i: (i, 0))
    pairs = lambda: pl.BlockSpec((D_MODEL // LANES, tm, LANES), lambda i: (0, i, 0))
    return pl.pallas_call(
        functools.partial(_odd_tail_kernel, nsub=2),
        grid=(m // tm,),
        in_specs=[pairs(), pairs(), row(), row(), _full((1, D_MODEL)), _full((1, D_MODEL)),
                  _full(p['e'].shape), _full((D_MODEL, D_MODEL)), row()] + _cross_specs(seq, n_mem, tm),
        out_specs=row(),
        out_shape=jax.ShapeDtypeStruct((m, D_MODEL), F32),
        scratch_shapes=[pltpu.VMEM((tm, D_MODEL), F32)],
        compiler_params=_cparams(("parallel",)),
        name="odd_tail",
    )(o_f, o_b, bonus, gate, p['lnx_w'], p['lnx_b'], p['e'], p['wo'], x,
      g_cross.reshape(1, D_MODEL), kv, kv, wq, wo)


def _prep_odd(o, rw_mu, rw_wr, rw_wk, rw_wv, rw_wo, rw_w0, rw_w1, rw_w2, rw_a0, rw_a1, rw_a2, rw_g1, rw_g2,
              rw_kk, rw_ka, rw_rk, rw_lnx_w, rw_lnx_b):
    bf = lambda a: a.astype(BF16)

    def pad_dir(w):
        z = jnp.zeros_like(w[0])
        return jnp.stack([jnp.concatenate([w[0], z], axis=0), jnp.concatenate([z, w[1]], axis=0)])

    lane = jnp.arange(MXU_WIDTH) // RWKV_HEAD
    return dict(
        mu=rw_mu[o], wr=bf(rw_wr[o]), wk=bf(rw_wk[o]), wv=bf(rw_wv[o]), wo=bf(rw_wo[o]),
        w0=rw_w0[o], w1=bf(jnp.concatenate([rw_w1[o, 0], rw_w1[o, 1]], axis=1)), w2=bf(pad_dir(rw_w2[o])),
        a0=rw_a0[o], a1=bf(jnp.concatenate([rw_a1[o, 0], rw_a1[o, 1]], axis=1)), a2=bf(pad_dir(rw_a2[o])),
        g1=bf(rw_g1[o]), g2=bf(rw_g2[o]),
        kk=rw_kk[o].reshape(1, D_MODEL), ka=rw_ka[o].reshape(1, D_MODEL), rk=rw_rk[o].reshape(1, D_MODEL),
        lnx_w=rw_lnx_w[o].reshape(1, D_MODEL), lnx_b=rw_lnx_b[o].reshape(1, D_MODEL),
        e=(lane[:, None] == lane[None, :]).astype(BF16))


def _even_layer(x, g, w_in, w_out, conv_w, cos, sin, cross, batch, seq):
    z1, z4, z16, zc = _even_proj(x, g, w_in, cos, sin, batch, seq)
    m = x.shape[0]
    o1, l1 = _dilated_branch(z1.reshape(3, batch, seq, A_WIDTH), seq)
    o4, l4 = _dilated_branch(z4.reshape(3, batch * 4, seq // 4, A_WIDTH), seq // 4)
    o16, l16 = _dilated_branch(z16.reshape(3, batch * 16, seq // 16, A_WIDTH), seq // 16)
    o1, l1 = o1.reshape(m, A_WIDTH), l1.reshape(m, A_WIDTH)
    d4 = lambda a: a.reshape(batch, 4, seq // 4, A_WIDTH)
    d16 = lambda a: a.reshape(batch, 16, seq // 16, A_WIDTH)
    return _even_tail(x, o1, l1, d4(o4), d4(l4), d16(o16), d16(l16), zc, conv_w, w_out, cross, batch, seq)


def _odd_layer(x, g, p, cross, batch, seq):
    r, v, kn, kd, ba, lw, gate, bonus = _rwkv_pre(x, g, p, batch, seq)
    o_f, o_b = _wkv(r, v, kn, kd, ba, lw, batch, seq)
    return _odd_tail(x, o_f, o_b, bonus, gate, p, cross, seq)


def _trunk(x, mem, w):
    batch, seq, _ = x.shape
    n_mem = mem.shape[1]
    x = x.reshape(batch * seq, D_MODEL)
    mem = mem.reshape(batch * n_mem, D_MODEL)
    cos, sin = _rope_tables(seq)
    for l in range(DEPTH):
        kv = _norm_matmul(mem, w['norm_mem'][l], w['ca_wkv'][l], n_mem, D_MODEL, BF16)
        cross = (w['norm_cross'][l], kv, w['ca_wq'][l], w['ca_wo'][l], n_mem)
        if l % 2 == 0:
            e = l // 2
            x = _even_layer(x, w['norm_mix'][l], w['ab_w_in'][e], w['ab_w_out'][e], w['ab_conv'][e],
                            cos, sin, cross, batch, seq)
        else:
            x = _odd_layer(x, w['norm_mix'][l], w['odd'][l // 2], cross, batch, seq)
        x = _ffn(x, w['norm_ffn'][l], w['ffn_wgu'][l], w['ffn_wdown'][l], w['norm_final'], l == DEPTH - 1)
    return x.reshape(batch, seq, D_MODEL)


def kernel(x_prompt, x_sample, mem_prompt, mem_sample, norm_mix, norm_cross, norm_mem, norm_ffn, norm_final,
           ab_w_in, ab_w_out, ab_conv, rw_mu, rw_wr, rw_wk, rw_wv, rw_wo, rw_w0, rw_w1, rw_w2, rw_a0, rw_a1,
           rw_a2, rw_g1, rw_g2, rw_kk, rw_ka, rw_rk, rw_lnx_w, rw_lnx_b, ca_wq, ca_wkv, ca_wo, ffn_wgu,
           ffn_wdown):
    bf = lambda a: a.astype(BF16)
    w = dict(norm_mix=norm_mix, norm_cross=norm_cross, norm_mem=norm_mem, norm_ffn=norm_ffn,
             norm_final=norm_final, ab_w_in=bf(ab_w_in), ab_w_out=bf(ab_w_out), ab_conv=ab_conv,
             ca_wq=bf(ca_wq), ca_wkv=bf(ca_wkv), ca_wo=bf(ca_wo), ffn_wgu=bf(ffn_wgu), ffn_wdown=bf(ffn_wdown),
             odd=[_prep_odd(o, rw_mu, rw_wr, rw_wk, rw_wv, rw_wo, rw_w0, rw_w1, rw_w2, rw_a0, rw_a1, rw_a2,
                            rw_g1, rw_g2, rw_kk, rw_ka, rw_rk, rw_lnx_w, rw_lnx_b)
                  for o in range(rw_mu.shape[0])])
    return _trunk(x_prompt, mem_prompt, w), _trunk(x_sample, mem_sample, w)
```

```python
import functools

import jax
import jax.numpy as jnp
from jax import lax
from jax.experimental import pallas as pl
from jax.experimental.pallas import tpu as pltpu

F32 = jnp.float32
BF16 = jnp.bfloat16

D_MODEL = 1024
DEPTH = 4
HEAD_DIM = 64
A_WIDTH = 512
B_WIDTH = 512
DILATIONS = (1, 4, 16)
BAND = 64
ROPE_THETA = 500000.0
ROPE_DIM = 16
RWKV_HEAD = 64
GN_EPS = 64e-5
DECAY_SCALE = 0.6065306597126334
CA_HEADS = 4
CA_HEAD_DIM = 256
D_FF = 2816
RMS_EPS = 1e-6
NEG_INF = -1e30
CHUNK = 64
LANES = 128
MXU_WIDTH = 256

VMEM_LIMIT = 48 * 1024 * 1024

TM = 512
TM_RW = 256
TB_DIL = 512
TQ_DIL = 128
DIL_LAG = 1
TF_FFN = 256
TS_WKV = 512
PAIRS_WKV = 4
WKV_GROUP = 2
WKV_STAGGER = 8


def _cparams(sem):
    return pltpu.CompilerParams(dimension_semantics=sem, vmem_limit_bytes=VMEM_LIMIT)


def _full(shape):
    n = len(shape)
    return pl.BlockSpec(shape, lambda *_: (0,) * n)


def _rms(x, g):
    ms = jnp.mean(x * x, axis=-1, keepdims=True)
    return x * lax.rsqrt(ms + RMS_EPS) * g


def _dot(a, b):
    return jnp.dot(a, b, preferred_element_type=F32)


def _dot_nt(a, b):
    return lax.dot_general(a, b, (((1,), (1,)), ((), ())), preferred_element_type=F32)


def _dot_tn(a, b):
    return lax.dot_general(a, b, (((0,), (0,)), ((), ())), preferred_element_type=F32)


def _split(x):
    hi = x.astype(BF16)
    lo = (x - hi.astype(F32)).astype(BF16)
    return hi, lo


def _norm_matmul_kernel(x_ref, g_ref, w_ref, o_ref, xn_ref):
    @pl.when(pl.program_id(1) == 0)
    def _():
        xn_ref[...] = _rms(x_ref[...], g_ref[...]).astype(BF16)

    o_ref[...] = _dot(xn_ref[...], w_ref[...]).astype(o_ref.dtype)


def _norm_matmul(x, g, w, tm, tn, out_dtype):
    m, k = x.shape
    n = w.shape[1]
    return pl.pallas_call(
        _norm_matmul_kernel,
        grid=(m // tm, n // tn),
        in_specs=[pl.BlockSpec((tm, k), lambda i, j: (i, 0)),
                  pl.BlockSpec((1, k), lambda i, j: (0, 0)),
                  pl.BlockSpec((k, tn), lambda i, j: (0, j))],
        out_specs=pl.BlockSpec((tm, tn), lambda i, j: (i, j)),
        out_shape=jax.ShapeDtypeStruct((m, n), out_dtype),
        scratch_shapes=[pltpu.VMEM((tm, k), BF16)],
        compiler_params=_cparams(("parallel", "arbitrary")),
        name="norm_matmul",
    )(x, g.reshape(1, k), w)


def _even_proj_kernel(x_ref, g_ref, w_ref, cos_ref, sin_ref, p4_ref, p16_ref,
                      o1_ref, o4_ref, o16_ref, oc_ref, *, tm):
    xn = _rms(x_ref[...], g_ref[...]).astype(BF16)
    reps = A_WIDTH // LANES
    cos = jnp.concatenate([cos_ref[...]] * reps, axis=1)
    sin = jnp.concatenate([sin_ref[...]] * reps, axis=1)
    lane = lax.broadcasted_iota(jnp.int32, (tm, A_WIDTH), 1) & (HEAD_DIM - 1)
    half = ROPE_DIM // 2
    nparts = 3
    acc = None
    for j in range(nparts + 1):
        nxt = _dot(xn, w_ref[:, j * A_WIDTH:(j + 1) * A_WIDTH]) if j < nparts else None
        if j == nparts:
            oc_ref[...] = _dot(xn, w_ref[:, 3 * A_WIDTH:]).astype(BF16)
        if j > 0:
            a, part = acc, j - 1
            if part < 2:
                partner = jnp.where(lane < half, pltpu.roll(a, A_WIDTH - half, 1), pltpu.roll(a, half, 1))
                a = a * cos + partner * sin
                if part == 0:
                    a = a * (HEAD_DIM ** -0.5)
            a = a.astype(BF16)
            o1_ref[part] = a
            for d, p_ref, o_ref in ((4, p4_ref, o4_ref), (16, p16_ref, o16_ref)):
                perm = _dot(p_ref[...], a).astype(BF16)
                for r in range(d):
                    o_ref[part, r] = perm[r * (tm // d):(r + 1) * (tm // d)]
        acc = nxt


def _dedilate_perm(tm, d):
    i = jnp.arange(tm)
    src = (i % (tm // d)) * d + i // (tm // d)
    return (src[:, None] == jnp.arange(tm)[None, :]).astype(BF16)


def _even_proj(x, g, w_in, cos, sin, batch, seq):
    m = x.shape[0]
    tm = TM
    nts = seq // tm
    kern = functools.partial(_even_proj_kernel, tm=tm)
    out_shape = (jax.ShapeDtypeStruct((3, m, A_WIDTH), BF16),
                 jax.ShapeDtypeStruct((3, batch, 4, seq // 4, A_WIDTH), BF16),
                 jax.ShapeDtypeStruct((3, batch, 16, seq // 16, A_WIDTH), BF16),
                 jax.ShapeDtypeStruct((m, 3 * B_WIDTH), BF16))
    return pl.pallas_call(
        kern,
        grid=(m // tm,),
        in_specs=[pl.BlockSpec((tm, D_MODEL), lambda i: (i, 0)),
                  _full((1, D_MODEL)),
                  _full(w_in.shape),
                  pl.BlockSpec((tm, LANES), lambda i: (i % nts, 0)),
                  pl.BlockSpec((tm, LANES), lambda i: (i % nts, 0)),
                  _full((tm, tm)), _full((tm, tm))],
        out_specs=(pl.BlockSpec((3, tm, A_WIDTH), lambda i: (0, i, 0)),
                   pl.BlockSpec((3, None, 4, tm // 4, A_WIDTH), lambda i: (0, i // nts, 0, i % nts, 0)),
                   pl.BlockSpec((3, None, 16, tm // 16, A_WIDTH), lambda i: (0, i // nts, 0, i % nts, 0)),
                   pl.BlockSpec((tm, 3 * B_WIDTH), lambda i: (i, 0))),
        out_shape=out_shape,
        compiler_params=_cparams(("parallel",)),
        name="even_proj",
    )(x, g.reshape(1, D_MODEL), w_in, cos, sin, _dedilate_perm(tm, 4), _dedilate_perm(tm, 16))


def _rope_tables(seq):
    half = ROPE_DIM // 2
    inv = ROPE_THETA ** (-2.0 * jnp.arange(half, dtype=F32) / ROPE_DIM)
    ang = jnp.arange(seq, dtype=F32)[:, None] * inv[None, :]
    cos, sin = jnp.cos(ang), jnp.sin(ang)
    rest = HEAD_DIM - ROPE_DIM
    cos_h = jnp.concatenate([cos, cos, jnp.ones((seq, rest), F32)], axis=1)
    sin_h = jnp.concatenate([-sin, sin, jnp.zeros((seq, rest), F32)], axis=1)
    reps = LANES // HEAD_DIM
    return jnp.tile(cos_h, (1, reps)), jnp.tile(sin_h, (1, reps))


def _dil_kernel(q_ref, kp_ref, kc_ref, kn_ref, vp_ref, vc_ref, vn_ref, o_ref, lse_ref, *, tb, tq, length):
    qi = pl.program_id(1)
    span = tq + 2 * BAND
    k = jnp.concatenate([kp_ref[...], kc_ref[...], kn_ref[...]], axis=0)
    v = jnp.concatenate([vp_ref[...], vc_ref[...], vn_ref[...]], axis=0)
    row = lax.broadcasted_iota(jnp.int32, (tq, span), 0)
    col = lax.broadcasted_iota(jnp.int32, (tq, span), 1)
    rel = col - BAND - row
    band = (rel <= BAND) & (rel >= -BAND)
    first = lax.broadcasted_iota(jnp.int32, (tq, LANES), 1) < HEAD_DIM
    units = [(s_idx, p, hh) for s_idx in range(tb // tq) for p in range(A_WIDTH // LANES) for hh in range(2)]
    lag = DIL_LAG
    scores, maxes, probs, results = {}, {}, {}, {}
    for n in range(len(units) + 3 * lag):
        if n < len(units):
            s_idx, p, hh = units[n]
            sl = slice(p * LANES, (p + 1) * LANES)
            qp = q_ref[s_idx * tq:(s_idx + 1) * tq, sl]
            sel = first if hh == 0 else jnp.logical_not(first)
            key_pos = qi * tb + s_idx * tq - BAND + col
            valid = band & (key_pos >= 0) & (key_pos < length)
            s = _dot_nt(jnp.where(sel, qp, jnp.zeros_like(qp)), k[s_idx * tq:s_idx * tq + span, sl])
            scores[n] = jnp.where(valid, s, NEG_INF)
        m = n - lag
        if 0 <= m < len(units):
            maxes[m] = jnp.max(scores[m], axis=-1, keepdims=True)
        m = n - 2 * lag
        if 0 <= m < len(units):
            mx = maxes.pop(m)
            e = jnp.exp(scores.pop(m) - mx)
            den = jnp.sum(e, axis=-1, keepdims=True)
            probs[m] = (e.astype(BF16), den, mx)
        m = n - 3 * lag
        if 0 <= m < len(units):
            s_idx, p, hh = units[m]
            sl = slice(p * LANES, (p + 1) * LANES)
            e, den, mx = probs.pop(m)
            results[hh] = (_dot(e, v[s_idx * tq:s_idx * tq + span, sl]) * (1.0 / den), mx + jnp.log(den))
            if hh == 1:
                rows = slice(s_idx * tq, (s_idx + 1) * tq)
                o_ref[rows, sl] = jnp.where(first, results[0][0], results[1][0]).astype(BF16)
                lse_ref[rows, sl] = jnp.where(first, results[0][1], results[1][1])


def _dilated_branch(zd, length):
    g = zd.shape[1]
    tb = min(TB_DIL, length)
    tq = min(TQ_DIL, length)
    nb = length // BAND
    r = tb // BAND
    kern = functools.partial(_dil_kernel, tb=tb, tq=tq, length=length)

    def cur(which):
        return pl.BlockSpec((None, None, tb, A_WIDTH), lambda b, i: (which, b, i, 0))

    def prev(which):
        return pl.BlockSpec((None, None, BAND, A_WIDTH), lambda b, i: (which, b, jnp.maximum(i * r - 1, 0), 0))

    def nxt(which):
        return pl.BlockSpec((None, None, BAND, A_WIDTH), lambda b, i: (which, b, jnp.minimum((i + 1) * r, nb - 1), 0))

    return pl.pallas_call(
        kern,
        grid=(g, length // tb),
        in_specs=[cur(0), prev(1), cur(1), nxt(1), prev(2), cur(2), nxt(2)],
        out_specs=(pl.BlockSpec((None, tb, A_WIDTH), lambda b, i: (b, i, 0)),
                   pl.BlockSpec((None, tb, A_WIDTH), lambda b, i: (b, i, 0))),
        out_shape=(jax.ShapeDtypeStruct((g, length, A_WIDTH), BF16),
                   jax.ShapeDtypeStruct((g, length, A_WIDTH), F32)),
        compiler_params=_cparams(("parallel", "parallel")),
        name="dilated_attn",
    )(zd, zd, zd, zd, zd, zd, zd)


def _even_out_kernel(o1_ref, l1_ref, o4_ref, l4_ref, o16_ref, l16_ref, bg_ref, cg_ref, h_ref,
                     cgp_ref, hp_ref, cgn_ref, hn_ref, cw_ref, wa_ref, wb_ref, x_ref, out_ref,
                     s4o, s4l, s16o, s16l, *, tm, nts):
    it = pl.program_id(0) % nts
    nslab = A_WIDTH // LANES
    for c in range(nslab):
        sl = slice(c * LANES, (c + 1) * LANES)
        for r in range(4):
            s4o[c, pl.ds(r, tm // 4, stride=4), :] = o4_ref[r, :, sl].astype(F32)
            s4l[c, pl.ds(r, tm // 4, stride=4), :] = l4_ref[r, :, sl]
        for r in range(16):
            s16o[c, pl.ds(r, tm // 16, stride=16), :] = o16_ref[r, :, sl].astype(F32)
            s16l[c, pl.ds(r, tm // 16, stride=16), :] = l16_ref[r, :, sl]
    wide = lambda ref: jnp.concatenate([ref[c] for c in range(nslab)], axis=1)
    l1, l4, l16 = l1_ref[...], wide(s4l), wide(s16l)
    mx = jnp.maximum(jnp.maximum(l1, l4), l16)
    e1, e4, e16 = jnp.exp(l1 - mx), jnp.exp(l4 - mx), jnp.exp(l16 - mx)
    ya = (e1 * o1_ref[...].astype(F32) + e4 * wide(s4o) + e16 * wide(s16o)) * (1.0 / (e1 + e4 + e16))

    u = cg_ref[...].astype(F32) * h_ref[...].astype(F32)
    last = cgp_ref.shape[0] - 1
    u_before = (cgp_ref[...].astype(F32) * hp_ref[...].astype(F32))[last:last + 1]
    u_after = (cgn_ref[...].astype(F32) * hn_ref[...].astype(F32))[0:1]
    u_before = u_before * (it > 0).astype(F32)
    u_after = u_after * (it < nts - 1).astype(F32)
    row = lax.broadcasted_iota(jnp.int32, u.shape, 0)
    u_prev = jnp.where(row == 0, u_before, pltpu.roll(u, 1, 0))
    u_next = jnp.where(row == tm - 1, u_after, pltpu.roll(u, tm - 1, 0))
    cw = cw_ref[...]
    yb = bg_ref[...].astype(F32) * (cw[0:1] * u_prev + cw[1:2] * u + cw[2:3] * u_next)

    out_ref[...] = x_ref[...] + _dot(ya.astype(BF16), wa_ref[...]) + _dot(yb.astype(BF16), wb_ref[...])


def _even_tail_kernel(*refs, tm, nts, nsub):
    mixer_in, cross_in, out_ref, scratch = refs[:17], refs[17:22], refs[22], refs[23:]
    x_mid = scratch[4]
    _even_out_kernel(*mixer_in, x_mid, *scratch[:4], tm=tm, nts=nts)
    _cross_kernel(x_mid, *cross_in, out_ref, nsub=nsub)


def _cross_specs(seq, n_mem, tm):
    nts = seq // tm
    return [_full((1, D_MODEL)),
            pl.BlockSpec((n_mem, D_MODEL), lambda i: (i // nts, 0)),
            pl.BlockSpec((n_mem, D_MODEL), lambda i: (i // nts, 1)),
            _full((D_MODEL, D_MODEL)),
            _full((D_MODEL, D_MODEL))]


def _even_tail(x, o1, l1, o4, l4, o16, l16, zc, conv_w, w_out, cross, batch, seq):
    g_cross, kv, wq, wo, n_mem = cross
    m = x.shape[0]
    tm = TM
    nts = seq // tm
    hb = 16
    nhb = m // hb
    kern = functools.partial(_even_tail_kernel, tm=tm, nts=nts, nsub=2)
    nat = lambda: pl.BlockSpec((tm, A_WIDTH), lambda i: (i, 0))
    dil = lambda d: pl.BlockSpec((None, d, tm // d, A_WIDTH), lambda i: (i // nts, 0, i % nts, 0))
    col = lambda c: pl.BlockSpec((tm, B_WIDTH), lambda i: (i, c))
    before = lambda c: pl.BlockSpec((hb, B_WIDTH), lambda i: (jnp.maximum(i * (tm // hb) - 1, 0), c))
    after = lambda c: pl.BlockSpec((hb, B_WIDTH), lambda i: (jnp.minimum((i + 1) * (tm // hb), nhb - 1), c))
    return pl.pallas_call(
        kern,
        grid=(m // tm,),
        in_specs=[nat(), nat(), dil(4), dil(4), dil(16), dil(16),
                  col(0), col(1), col(2), before(1), before(2), after(1), after(2),
                  _full((3, B_WIDTH)),
                  pl.BlockSpec((A_WIDTH, D_MODEL), lambda i: (0, 0)),
                  pl.BlockSpec((B_WIDTH, D_MODEL), lambda i: (1, 0)),
                  pl.BlockSpec((tm, D_MODEL), lambda i: (i, 0))] + _cross_specs(seq, n_mem, tm),
        out_specs=pl.BlockSpec((tm, D_MODEL), lambda i: (i, 0)),
        out_shape=jax.ShapeDtypeStruct((m, D_MODEL), F32),
        scratch_shapes=[pltpu.VMEM((A_WIDTH // LANES, tm, LANES), F32)] * 4 + [pltpu.VMEM((tm, D_MODEL), F32)],
        compiler_params=_cparams(("parallel",)),
        name="even_tail",
    )(o1, l1, o4, l4, o16, l16, zc, zc, zc, zc, zc, zc, zc, conv_w, w_out, w_out, x,
      g_cross.reshape(1, D_MODEL), kv, kv, wq, wo)


def _cross_kernel(x_ref, g_ref, k_ref, v_ref, wq_ref, wo_ref, out_ref, *, nsub):
    rows = x_ref.shape[0] // nsub
    g = g_ref[...]
    heads = [slice(h * CA_HEAD_DIM, (h + 1) * CA_HEAD_DIM) for h in range(CA_HEADS)]
    k = k_ref[...]
    v = v_ref[...]
    qs = []
    for i in range(nsub):
        xn = _rms(x_ref[i * rows:(i + 1) * rows, :], g).astype(BF16)
        qs.append((_dot(xn, wq_ref[...]) * (CA_HEAD_DIM ** -0.5)).astype(BF16))
    units = [(i, h) for i in range(nsub) for h in range(CA_HEADS)]
    scores, probs, outs = {}, {}, {}
    for step in range(len(units) + 2):
        if step < len(units):
            i, h = units[step]
            scores[step] = _dot_nt(qs[i][:, heads[h]], k[:, heads[h]])
        if 0 <= step - 1 < len(units):
            s = scores.pop(step - 1)
            e = jnp.exp(s - jnp.max(s, axis=-1, keepdims=True))
            probs[step - 1] = (e * (1.0 / jnp.sum(e, axis=-1, keepdims=True))).astype(BF16)
        if 0 <= step - 2 < len(units):
            i, h = units[step - 2]
            outs[i, h] = _dot(probs.pop(step - 2), v[:, heads[h]]).astype(BF16)
            if h == CA_HEADS - 1:
                o = jnp.concatenate([outs.pop((i, hh)) for hh in range(CA_HEADS)], axis=1)
                out_ref[i * rows:(i + 1) * rows, :] = x_ref[i * rows:(i + 1) * rows, :] + _dot(o, wo_ref[...])


def _ffn_kernel(x_ref, g_ref, wgu_ref, wd_ref, gf_ref, out_ref, *, final_norm, tf, nsub):
    rows = x_ref.shape[0] // nsub
    g = g_ref[...]
    accs = [x_ref[i * rows:(i + 1) * rows, :] for i in range(nsub)]
    xns = [_rms(x, g).astype(BF16) for x in accs]
    nf = D_FF // tf
    acts = None
    for f in range(nf + 1):
        new_acts = []
        if f < nf:
            for i in range(nsub):
                gate = _dot(xns[i], wgu_ref[:, f * tf:(f + 1) * tf])
                up = _dot(xns[i], wgu_ref[:, D_FF + f * tf:D_FF + (f + 1) * tf])
                new_acts.append((gate * _sigmoid(gate) * up).astype(BF16))
        if f > 0:
            for i in range(nsub):
                accs[i] = accs[i] + _dot(acts[i], wd_ref[(f - 1) * tf:f * tf, :])
        acts = new_acts
    for i in range(nsub):
        y = _rms(accs[i], gf_ref[...]) if final_norm else accs[i]
        out_ref[i * rows:(i + 1) * rows, :] = y


def _ffn(x, g, w_gu, w_down, g_final, final_norm):
    m = x.shape[0]
    tm = TM
    kern = functools.partial(_ffn_kernel, final_norm=final_norm, tf=TF_FFN, nsub=2)
    resident = lambda shape: pl.BlockSpec(shape, lambda i: (0, 0), pipeline_mode=pl.Buffered(1))
    return pl.pallas_call(
        kern,
        grid=(m // tm,),
        in_specs=[pl.BlockSpec((tm, D_MODEL), lambda i: (i, 0)),
                  _full((1, D_MODEL)),
                  resident(w_gu.shape),
                  resident(w_down.shape),
                  _full((1, D_MODEL))],
        out_specs=pl.BlockSpec((tm, D_MODEL), lambda i: (i, 0)),
        out_shape=jax.ShapeDtypeStruct((m, D_MODEL), F32),
        compiler_params=_cparams(("parallel",)),
        name="swiglu",
    )(x, g.reshape(1, D_MODEL), w_gu, w_down, g_final.reshape(1, D_MODEL))


def _head_sum(x, e_ref):
    x = x.astype(BF16)
    e = e_ref[...]
    width = e.shape[0]
    return jnp.concatenate([_dot(x[:, c * width:(c + 1) * width], e) for c in range(x.shape[1] // width)],
                           axis=1)


def _sigmoid(x):
    return 0.5 * jnp.tanh(0.5 * x) + 0.5


def _store_pairs(ref, val, *lead):
    for p in range(val.shape[1] // LANES):
        ref[(*lead, p)] = val[:, p * LANES:(p + 1) * LANES].astype(ref.dtype)


def _rwkv_pre_kernel(x_ref, xb_ref, xa_ref, g_ref, mu_ref, wr_ref, wk_ref, wv_ref, w1_ref, w2_ref, w0_ref,
                     a1_ref, a2_ref, a0_ref, g1_ref, g2_ref, kk_ref, ka_ref, rk_ref, e_ref,
                     sm_ref, r_out, v_out, kn_out, kd_out, ba_out, lw_out, gate_out, bonus_out, *, nts):
    it = pl.program_id(0) % nts
    g = g_ref[...]
    xn_bf = _rms(x_ref[...], g).astype(BF16)
    before = (_rms(xb_ref[...], g) * (it > 0).astype(F32)).astype(BF16)
    after = (_rms(xa_ref[...], g) * (it < nts - 1).astype(F32)).astype(BF16)
    xx_bf = _dot(sm_ref[...], jnp.concatenate([before, xn_bf, after], axis=0)).astype(BF16)
    mu = mu_ref[...].astype(BF16)
    mix = lambda i: xn_bf + xx_bf * mu[i:i + 1]

    tw = jnp.tanh(_dot(mix(1), w1_ref[...])).astype(BF16)
    ta = _dot(mix(4), a1_ref[...]).astype(BF16)
    tg = _sigmoid(_dot(mix(5), g1_ref[...])).astype(BF16)
    k_lin = _dot(mix(2), wk_ref[...])
    wls = [w0_ref[d:d + 1] + _dot(tw, w2_ref[d]) for d in range(2)]
    r = _dot(mix(0), wr_ref[...])
    for d in range(2):
        _store_pairs(lw_out, (-0.5 * DECAY_SCALE) * jnp.tanh(0.5 * wls[d]) - 0.5 * DECAY_SCALE, d)
    kn = k_lin * kk_ref[...]
    kn = kn * lax.rsqrt(jnp.maximum(_head_sum(kn * kn, e_ref), 1e-24))
    _store_pairs(kn_out, kn)
    als = [a0_ref[d:d + 1] + _dot(ta, a2_ref[d]) for d in range(2)]
    v = _dot(mix(3), wv_ref[...])
    hk = k_lin * (0.5 * ka_ref[...])
    k_base = k_lin - hk
    hn = 0.5 * kn
    kd_sum = jnp.zeros_like(k_lin)
    for d in range(2):
        t = jnp.tanh(0.5 * als[d])
        kd = k_base + hk * t
        kd_sum = kd_sum + kd
        _store_pairs(kd_out, kd, d)
        _store_pairs(ba_out, hn + hn * t, d)
    gate_out[...] = _dot(tg, g2_ref[...]).astype(BF16)
    _store_pairs(r_out, r)
    _store_pairs(v_out, v)
    bonus_out[...] = (_head_sum(r * kd_sum * rk_ref[...], e_ref) * v).astype(BF16)


def _rwkv_pre(x, g, p, batch, seq):
    m = x.shape[0]
    tm = TM_RW
    nts = seq // tm
    hb = 16
    nhb = m // hb
    kern = functools.partial(_rwkv_pre_kernel, nts=nts)
    row = lambda: pl.BlockSpec((tm, D_MODEL), lambda i: (i, 0))
    npair = D_MODEL // LANES
    pairs = lambda: pl.BlockSpec((npair, tm, LANES), lambda i: (0, i, 0))
    pairs2 = lambda: pl.BlockSpec((2, npair, tm, LANES), lambda i: (0, 0, i, 0))
    t = jnp.arange(tm)[:, None]
    c = jnp.arange(tm + 2 * hb)[None, :] - hb
    shift = (jnp.where(c == t, -1.0, 0.0) + jnp.where(jnp.abs(c - t) == 1, 0.5, 0.0)).astype(BF16)
    args = (x, x, x, g.reshape(1, D_MODEL), p['mu'], p['wr'], p['wk'], p['wv'], p['w1'], p['w2'], p['w0'],
            p['a1'], p['a2'], p['a0'], p['g1'], p['g2'], p['kk'], p['ka'], p['rk'], p['e'], shift)
    in_specs = [row(),
                pl.BlockSpec((hb, D_MODEL), lambda i: (jnp.maximum(i * (tm // hb) - 1, 0), 0)),
                pl.BlockSpec((hb, D_MODEL), lambda i: (jnp.minimum((i + 1) * (tm // hb), nhb - 1), 0))]
    in_specs += [_full(a.shape) for a in args[3:]]
    out_shape = (jax.ShapeDtypeStruct((npair, m, LANES), BF16),) * 3 + (
        jax.ShapeDtypeStruct((2, npair, m, LANES), BF16), jax.ShapeDtypeStruct((2, npair, m, LANES), BF16),
        jax.ShapeDtypeStruct((2, npair, m, LANES), F32),
        jax.ShapeDtypeStruct((m, D_MODEL), BF16), jax.ShapeDtypeStruct((m, D_MODEL), BF16))
    out_specs = (pairs(), pairs(), pairs(), pairs2(), pairs2(), pairs2(), row(), row())
    return pl.pallas_call(
        kern, grid=(m // tm,), in_specs=in_specs, out_specs=out_specs, out_shape=out_shape,
        compiler_params=_cparams(("parallel",)), name="rwkv_pre",
    )(*args)


def _bd(x, first):
    z = jnp.zeros_like(x)
    return jnp.concatenate([jnp.where(first, x, z), jnp.where(first, z, x)], axis=0)


def _compact(full, first):
    return jnp.where(first, full[:CHUNK], full[CHUNK:])


def _wkv_masks(sgn):
    c = CHUNK
    row = lax.broadcasted_iota(jnp.int32, (c, c), 0)
    col = lax.broadcasted_iota(jnp.int32, (c, c), 1)
    tri = jnp.where((row - col) * sgn >= 0, 1.0, 0.0).astype(BF16)
    prow = lax.broadcasted_iota(jnp.int32, (c, LANES), 0)
    pcol = lax.broadcasted_iota(jnp.int32, (c, LANES), 1) & (c - 1)
    dlt = (prow - pcol) * sgn
    strict = dlt > 0
    same = lambda sh: (prow >> sh) == (pcol >> sh)
    levels = []
    sh = 2
    while (1 << sh) < c:
        levels.append(jnp.where(strict & same(sh + 1) & jnp.logical_not(same(sh)), 1.0, 0.0))
        sh += 1
    in4 = same(2)
    return dict(tri=tri, strict=strict, incl=dlt >= 0, eye=jnp.where(prow == pcol, 1.0, 0.0), levels=levels,
                tri4=jnp.where(strict & in4, 1.0, 0.0),
                off2=jnp.where(in4 & (dlt == 2), 1.0, 0.0), off3=jnp.where(in4 & (dlt == 3), 1.0, 0.0),
                row1=sgn % c, row2=(2 * sgn) % c, lane1=(-sgn) % LANES, lane2=(-2 * sgn) % LANES)


def _inverse4(a_ab, mk):
    a = a_ab * mk['tri4']
    up1 = pltpu.roll(a, mk['row1'], 0)
    up2 = pltpu.roll(a, mk['row2'], 0)
    left1 = pltpu.roll(a, mk['lane1'], 1)
    left2 = pltpu.roll(a, mk['lane2'], 1)
    sq = left1 * up1 * mk['off2'] + (left1 * up2 + left2 * up1) * mk['off3']
    cube = pltpu.roll(sq, mk['lane1'], 1) * up2 * mk['off3']
    return mk['eye'] + a + sq + cube


def _wkv_local(chains, first):
    c = CHUNK
    for s in chains:
        hl = _dot(s['mk']['tri'], jnp.concatenate(_split(s['lw']), axis=1))
        s['cum'] = hl[:, :LANES] + hl[:, LANES:]
    yield
    for s in chains:
        cum, lw = s['cum'], s['lw']
        total = jnp.sum(lw, axis=0, keepdims=True)
        p_inv = jnp.exp(-cum)
        p_hat = jnp.exp(total - cum)
        s['at'] = (-s['kn'] * jnp.exp(cum - lw)).astype(BF16)
        s['rt'] = (s['r'] * jnp.exp(cum)).astype(BF16)
        s['bt'] = (s['ba'] * p_inv).astype(BF16)
        s['kt'] = (s['kd'] * p_inv).astype(BF16)
        s['bh'] = (s['ba'] * p_hat).astype(BF16)
        s['kh'] = (s['kd'] * p_hat).astype(BF16)
        s['v_bf'] = s['v'].astype(BF16)
        s['p_end'] = jnp.exp(total)
        s['lhs'] = jnp.concatenate([s['at'], s['rt']], axis=0)
    yield
    for s in chains:
        s['sbk'] = _dot_nt(s['lhs'], jnp.concatenate([_bd(s['bt'], first), _bd(s['kt'], first)], axis=0))
    yield
    for s in chains:
        mk = s['mk']
        sb, sk = s['sbk'][:, :LANES], s['sbk'][:, LANES:]
        s['a_ab'] = jnp.where(mk['strict'], sb[:c], 0.0)
        s['a_rb'] = jnp.where(mk['incl'], sb[c:], 0.0).astype(BF16)
        s['a_kk'] = jnp.concatenate([jnp.where(mk['strict'], sk[:c], 0.0),
                                     jnp.where(mk['incl'], sk[c:], 0.0)], axis=0).astype(BF16)
        s['inv'] = _inverse4(s['a_ab'], mk)
    yield
    for s in chains:
        s['av'] = _dot(s['a_kk'], _bd(s['v_bf'], first))
        s['y0'] = _compact(_dot_tn(s['v_bf'], s['kh']), first)
    yield
    for lvl in range(len(chains[0]['mk']['levels'])):
        for s in chains:
            s['inv_bf'] = s['inv'].astype(BF16)
            s['inner'] = _dot((s['a_ab'] * s['mk']['levels'][lvl]).astype(BF16), _bd(s['inv_bf'], first))
        yield
        for s in chains:
            s['inv'] = s['inv'] + _dot(s['inv_bf'], _bd(s['inner'].astype(BF16), first))
        yield
    for s in chains:
        rhs = jnp.concatenate([_bd(s['at'], first), _bd(s['av'][:c].astype(BF16), first)], axis=1)
        wu = _dot(s['inv'].astype(BF16), rhs)
        s['w1'] = wu[:, :LANES].astype(BF16)
        s['u0'] = wu[:, LANES:]
        s['o0'] = s['av'][c:]


def _wkv_seq(rows, ys, out_refs, first):
    c = CHUNK
    for row in rows:
        wrs = [_dot_nt(jnp.concatenate([s['w1'], s['rt']], axis=0), _bd(y.astype(BF16), first))
               for s, y in zip(row, ys)]
        yield
        us = [(wr[:c] + s['u0']).astype(BF16) for s, wr in zip(row, wrs)]
        for s, wr, u, (o_ref, p) in zip(row, wrs, us, out_refs):
            o_ref[p, s['sl'], :] = (wr[c:] + _dot(s['a_rb'], _bd(u, first)) + s['o0']).astype(o_ref.dtype)
        ys[:] = [s['p_end'] * y + _compact(_dot_tn(u, s['bh']), first) + s['y0'] for s, y, u in zip(row, ys, us)]
        yield


def _run_staggered(local_gens, make_seq, period):
    end = object()
    started, finished, tick = 0, 0, 0
    active, updates = [], []
    while finished < len(local_gens) or updates:
        if started < len(local_gens) and tick >= started * period:
            active.append(started)
            started += 1
        for g in list(active):
            if next(local_gens[g], end) is end:
                active.remove(g)
                finished += 1
                updates.append(make_seq(g))
        if updates and next(updates[0], end) is end:
            updates.pop(0)
        tick += 1


def _wkv_kernel(rf, vf, nf, kdf, baf, lwf, rb, vb, nb, kdb, bab, lwb, of_ref, ob_ref, state, *, ts, npairs):
    first = lax.broadcasted_iota(jnp.int32, (CHUNK, LANES), 1) < RWKV_HEAD

    @pl.when(pl.program_id(2) == 0)
    def _():
        state[...] = jnp.zeros_like(state)

    nchunk = ts // CHUNK
    names = ('r', 'v', 'kn', 'kd', 'ba', 'lw')
    dirs = ((_wkv_masks(1), (rf, vf, nf, kdf, baf, lwf), of_ref, list(range(nchunk))),
            (_wkv_masks(-1), (rb, vb, nb, kdb, bab, lwb), ob_ref, list(range(nchunk - 1, -1, -1))))
    scans = [(d, p) for d in range(2) for p in range(npairs)]
    out_refs = [(dirs[d][2], p) for d, p in scans]
    steps = []
    for k in range(nchunk):
        row = []
        for d, p in scans:
            mk, ins, _, order = dirs[d]
            sl = pl.ds(order[k] * CHUNK, CHUNK)
            chain = {n: ref[p, sl, :].astype(F32) for n, ref in zip(names, ins)}
            chain.update(mk=mk, sl=sl)
            row.append(chain)
        steps.append(row)
    ys = [state[d, p] for d, p in scans]
    groups = [steps[k:k + WKV_GROUP] for k in range(0, nchunk, WKV_GROUP)]
    _run_staggered([_wkv_local([s for row in rows for s in row], first) for rows in groups],
                   lambda g: _wkv_seq(groups[g], ys, out_refs, first), WKV_STAGGER)
    for (d, p), y in zip(scans, ys):
        state[d, p] = y


def _wkv(r, v, kn, kd, ba, lw, batch, seq):
    ts, npairs = TS_WKV, PAIRS_WKV
    nt = seq // ts
    npair = D_MODEL // LANES
    m = batch * seq
    fwd = lambda b, t: b * nt + t
    bwd = lambda b, t: b * nt + nt - 1 - t
    s3 = lambda at: pl.BlockSpec((npairs, ts, LANES), lambda b, g, t: (g, at(b, t), 0))
    s4 = lambda d, at: pl.BlockSpec((None, npairs, ts, LANES), lambda b, g, t: (d, g, at(b, t), 0))
    return pl.pallas_call(
        functools.partial(_wkv_kernel, ts=ts, npairs=npairs),
        grid=(batch, npair // npairs, nt),
        in_specs=[s3(fwd), s3(fwd), s3(fwd), s4(0, fwd), s4(0, fwd), s4(0, fwd),
                  s3(bwd), s3(bwd), s3(bwd), s4(1, bwd), s4(1, bwd), s4(1, bwd)],
        out_specs=(s3(fwd), s3(bwd)),
        out_shape=(jax.ShapeDtypeStruct((npair, m, LANES), BF16),) * 2,
        scratch_shapes=[pltpu.VMEM((2, npairs, CHUNK, LANES), F32)],
        compiler_params=_cparams(("parallel", "parallel", "arbitrary")),
        name="wkv",
    )(r, v, kn, kd, ba, lw, r, v, kn, kd, ba, lw)


def _rwkv_post_kernel(of_ref, ob_ref, bonus_ref, gate_ref, lw_ref, lb_ref, e_ref, wo_ref, x_ref, out_ref):
    y = jnp.concatenate([of_ref[p].astype(F32) + ob_ref[p].astype(F32) for p in range(of_ref.shape[0])], axis=1)
    mean = _head_sum(y, e_ref) * (1.0 / RWKV_HEAD)
    yc = y - mean
    var = _head_sum(yc * yc, e_ref) * (1.0 / RWKV_HEAD)
    yn = yc * lax.rsqrt(var + GN_EPS)
    y2 = yn * lw_ref[...] + lb_ref[...] + bonus_ref[...].astype(F32)
    out_ref[...] = x_ref[...] + _dot((y2 * gate_ref[...].astype(F32)).astype(BF16), wo_ref[...])


def _odd_tail_kernel(*refs, nsub):
    mixer_in, cross_in, out_ref, x_mid = refs[:9], refs[9:14], refs[14], refs[15]
    _rwkv_post_kernel(*mixer_in, x_mid)
    _cross_kernel(x_mid, *cross_in, out_ref, nsub=nsub)


def _odd_tail(x, o_f, o_b, bonus, gate, p, cross, seq):
    g_cross, kv, wq, wo, n_mem = cross
    m = x.shape[0]
    tm = TM
    row = lambda: pl.BlockSpec((tm, D_MODEL), lambda i: (i, 0))
    pairs = lambda: pl.BlockSpec((D_MODEL // LANES, tm, LANES), lambda i: (0, i, 0))
    return pl.pallas_call(
        functools.partial(_odd_tail_kernel, nsub=2),
        grid=(m // tm,),
        in_specs=[pairs(), pairs(), row(), row(), _full((1, D_MODEL)), _full((1, D_MODEL)),
                  _full(p['e'].shape), _full((D_MODEL, D_MODEL)), row()] + _cross_specs(seq, n_mem, tm),
        out_specs=row(),
        out_shape=jax.ShapeDtypeStruct((m, D_MODEL), F32),
        scratch_shapes=[pltpu.VMEM((tm, D_MODEL), F32)],
        compiler_params=_cparams(("parallel",)),
        name="odd_tail",
    )(o_f, o_b, bonus, gate, p['lnx_w'], p['lnx_b'], p['e'], p['wo'], x,
      g_cross.reshape(1, D_MODEL), kv, kv, wq, wo)


def _prep_odd(o, rw_mu, rw_wr, rw_wk, rw_wv, rw_wo, rw_w0, rw_w1, rw_w2, rw_a0, rw_a1, rw_a2, rw_g1, rw_g2,
              rw_kk, rw_ka, rw_rk, rw_lnx_w, rw_lnx_b):
    bf = lambda a: a.astype(BF16)

    def pad_dir(w):
        z = jnp.zeros_like(w[0])
        return jnp.stack([jnp.concatenate([w[0], z], axis=0), jnp.concatenate([z, w[1]], axis=0)])

    lane = jnp.arange(MXU_WIDTH) // RWKV_HEAD
    return dict(
        mu=rw_mu[o], wr=bf(rw_wr[o]), wk=bf(rw_wk[o]), wv=bf(rw_wv[o]), wo=bf(rw_wo[o]),
        w0=rw_w0[o], w1=bf(jnp.concatenate([rw_w1[o, 0], rw_w1[o, 1]], axis=1)), w2=bf(pad_dir(rw_w2[o])),
        a0=rw_a0[o], a1=bf(jnp.concatenate([rw_a1[o, 0], rw_a1[o, 1]], axis=1)), a2=bf(pad_dir(rw_a2[o])),
        g1=bf(rw_g1[o]), g2=bf(rw_g2[o]),
        kk=rw_kk[o].reshape(1, D_MODEL), ka=rw_ka[o].reshape(1, D_MODEL), rk=rw_rk[o].reshape(1, D_MODEL),
        lnx_w=rw_lnx_w[o].reshape(1, D_MODEL), lnx_b=rw_lnx_b[o].reshape(1, D_MODEL),
        e=(lane[:, None] == lane[None, :]).astype(BF16))


def _even_layer(x, g, w_in, w_out, conv_w, cos, sin, cross, batch, seq):
    z1, z4, z16, zc = _even_proj(x, g, w_in, cos, sin, batch, seq)
    m = x.shape[0]
    o1, l1 = _dilated_branch(z1.reshape(3, batch, seq, A_WIDTH), seq)
    o4, l4 = _dilated_branch(z4.reshape(3, batch * 4, seq // 4, A_WIDTH), seq // 4)
    o16, l16 = _dilated_branch(z16.reshape(3, batch * 16, seq // 16, A_WIDTH), seq // 16)
    o1, l1 = o1.reshape(m, A_WIDTH), l1.reshape(m, A_WIDTH)
    d4 = lambda a: a.reshape(batch, 4, seq // 4, A_WIDTH)
    d16 = lambda a: a.reshape(batch, 16, seq // 16, A_WIDTH)
    return _even_tail(x, o1, l1, d4(o4), d4(l4), d16(o16), d16(l16), zc, conv_w, w_out, cross, batch, seq)


def _odd_layer(x, g, p, cross, batch, seq):
    r, v, kn, kd, ba, lw, gate, bonus = _rwkv_pre(x, g, p, batch, seq)
    o_f, o_b = _wkv(r, v, kn, kd, ba, lw, batch, seq)
    return _odd_tail(x, o_f, o_b, bonus, gate, p, cross, seq)


def _trunk(x, mem, w):
    batch, seq, _ = x.shape
    n_mem = mem.shape[1]
    x = x.reshape(batch * seq, D_MODEL)
    mem = mem.reshape(batch * n_mem, D_MODEL)
    cos, sin = _rope_tables(seq)
    for l in range(DEPTH):
        kv = _norm_matmul(mem, w['norm_mem'][l], w['ca_wkv'][l], n_mem, D_MODEL, BF16)
        cross = (w['norm_cross'][l], kv, w['ca_wq'][l], w['ca_wo'][l], n_mem)
        if l % 2 == 0:
            e = l // 2
            x = _even_layer(x, w['norm_mix'][l], w['ab_w_in'][e], w['ab_w_out'][e], w['ab_conv'][e],
                            cos, sin, cross, batch, seq)
        else:
            x = _odd_layer(x, w['norm_mix'][l], w['odd'][l // 2], cross, batch, seq)
        x = _ffn(x, w['norm_ffn'][l], w['ffn_wgu'][l], w['ffn_wdown'][l], w['norm_final'], l == DEPTH - 1)
    return x.reshape(batch, seq, D_MODEL)


def kernel(x_prompt, x_sample, mem_prompt, mem_sample, norm_mix, norm_cross, norm_mem, norm_ffn, norm_final,
           ab_w_in, ab_w_out, ab_conv, rw_mu, rw_wr, rw_wk, rw_wv, rw_wo, rw_w0, rw_w1, rw_w2, rw_a0, rw_a1,
           rw_a2, rw_g1, rw_g2, rw_kk, rw_ka, rw_rk, rw_lnx_w, rw_lnx_b, ca_wq, ca_wkv, ca_wo, ffn_wgu,
           ffn_wdown):
    bf = lambda a: a.astype(BF16)
    w = dict(norm_mix=norm_mix, norm_cross=norm_cross, norm_mem=norm_mem, norm_ffn=norm_ffn,
             norm_final=norm_final, ab_w_in=bf(ab_w_in), ab_w_out=bf(ab_w_out), ab_conv=ab_conv,
             ca_wq=bf(ca_wq), ca_wkv=bf(ca_wkv), ca_wo=bf(ca_wo), ffn_wgu=bf(ffn_wgu), ffn_wdown=bf(ffn_wdown),
             odd=[_prep_odd(o, rw_mu, rw_wr, rw_wk, rw_wv, rw_wo, rw_w0, rw_w1, rw_w2, rw_a0, rw_a1, rw_a2,
                            rw_g1, rw_g2, rw_kk, rw_ka, rw_rk, rw_lnx_w, rw_lnx_b)
                  for o in range(rw_mu.shape[0])])
    return _trunk(x_prompt, mem_prompt, w), _trunk(x_sample, mem_sample, w)
```

```python
import functools

import jax
import jax.numpy as jnp
from jax import lax
from jax.experimental import pallas as pl
from jax.experimental.pallas import tpu as pltpu

F32 = jnp.float32
BF16 = jnp.bfloat16

D_MODEL = 1024
DEPTH = 4
HEAD_DIM = 64
A_WIDTH = 512
B_WIDTH = 512
DILATIONS = (1, 4, 16)
BAND = 64
ROPE_THETA = 500000.0
ROPE_DIM = 16
RWKV_HEAD = 64
GN_EPS = 64e-5
DECAY_SCALE = 0.6065306597126334
CA_HEADS = 4
CA_HEAD_DIM = 256
D_FF = 2816
RMS_EPS = 1e-6
NEG_INF = -1e30
CHUNK = 64
LANES = 128
MXU_WIDTH = 256

VMEM_LIMIT = 48 * 1024 * 1024

TM = 512
TM_RW = 256
TB_DIL = 512
TQ_DIL = 128
DIL_LAG = 1
TF_FFN = 256
TS_WKV = 512
PAIRS_WKV = 4
WKV_GROUP = 2
WKV_STAGGER = 8


def _cparams(sem):
    return pltpu.CompilerParams(dimension_semantics=sem, vmem_limit_bytes=VMEM_LIMIT)


def _full(shape):
    n = len(shape)
    return pl.BlockSpec(shape, lambda *_: (0,) * n)


def _rms(x, g):
    ms = jnp.mean(x * x, axis=-1, keepdims=True)
    return x * lax.rsqrt(ms + RMS_EPS) * g


def _dot(a, b):
    return jnp.dot(a, b, preferred_element_type=F32)


def _dot_nt(a, b):
    return lax.dot_general(a, b, (((1,), (1,)), ((), ())), preferred_element_type=F32)


def _dot_tn(a, b):
    return lax.dot_general(a, b, (((0,), (0,)), ((), ())), preferred_element_type=F32)


def _split(x):
    hi = x.astype(BF16)
    lo = (x - hi.astype(F32)).astype(BF16)
    return hi, lo


def _norm_matmul_kernel(x_ref, g_ref, w_ref, o_ref, xn_ref):
    @pl.when(pl.program_id(1) == 0)
    def _():
        xn_ref[...] = _rms(x_ref[...], g_ref[...]).astype(BF16)

    o_ref[...] = _dot(xn_ref[...], w_ref[...]).astype(o_ref.dtype)


def _norm_matmul(x, g, w, tm, tn, out_dtype):
    m, k = x.shape
    n = w.shape[1]
    return pl.pallas_call(
        _norm_matmul_kernel,
        grid=(m // tm, n // tn),
        in_specs=[pl.BlockSpec((tm, k), lambda i, j: (i, 0)),
                  pl.BlockSpec((1, k), lambda i, j: (0, 0)),
                  pl.BlockSpec((k, tn), lambda i, j: (0, j))],
        out_specs=pl.BlockSpec((tm, tn), lambda i, j: (i, j)),
        out_shape=jax.ShapeDtypeStruct((m, n), out_dtype),
        scratch_shapes=[pltpu.VMEM((tm, k), BF16)],
        compiler_params=_cparams(("parallel", "arbitrary")),
        name="norm_matmul",
    )(x, g.reshape(1, k), w)


def _even_proj_kernel(x_ref, g_ref, w_ref, cos_ref, sin_ref, p4_ref, p16_ref,
                      o1_ref, o4_ref, o16_ref, oc_ref, *, tm):
    xn = _rms(x_ref[...], g_ref[...]).astype(BF16)
    reps = A_WIDTH // LANES
    cos = jnp.concatenate([cos_ref[...]] * reps, axis=1)
    sin = jnp.concatenate([sin_ref[...]] * reps, axis=1)
    lane = lax.broadcasted_iota(jnp.int32, (tm, A_WIDTH), 1) & (HEAD_DIM - 1)
    half = ROPE_DIM // 2
    nparts = 3
    acc = None
    for j in range(nparts + 1):
        nxt = _dot(xn, w_ref[:, j * A_WIDTH:(j + 1) * A_WIDTH]) if j < nparts else None
        if j == nparts:
            oc_ref[...] = _dot(xn, w_ref[:, 3 * A_WIDTH:]).astype(BF16)
        if j > 0:
            a, part = acc, j - 1
            if part < 2:
                partner = jnp.where(lane < half, pltpu.roll(a, A_WIDTH - half, 1), pltpu.roll(a, half, 1))
                a = a * cos + partner * sin
                if part == 0:
                    a = a * (HEAD_DIM ** -0.5)
            a = a.astype(BF16)
            o1_ref[part] = a
            for d, p_ref, o_ref in ((4, p4_ref, o4_ref), (16, p16_ref, o16_ref)):
                perm = _dot(p_ref[...], a).astype(BF16)
                for r in range(d):
                    o_ref[part, r] = perm[r * (tm // d):(r + 1) * (tm // d)]
        acc = nxt


def _dedilate_perm(tm, d):
    i = jnp.arange(tm)
    src = (i % (tm // d)) * d + i // (tm // d)
    return (src[:, None] == jnp.arange(tm)[None, :]).astype(BF16)


def _even_proj(x, g, w_in, cos, sin, batch, seq):
    m = x.shape[0]
    tm = TM
    nts = seq // tm
    kern = functools.partial(_even_proj_kernel, tm=tm)
    out_shape = (jax.ShapeDtypeStruct((3, m, A_WIDTH), BF16),
                 jax.ShapeDtypeStruct((3, batch, 4, seq // 4, A_WIDTH), BF16),
                 jax.ShapeDtypeStruct((3, batch, 16, seq // 16, A_WIDTH), BF16),
                 jax.ShapeDtypeStruct((m, 3 * B_WIDTH), BF16))
    return pl.pallas_call(
        kern,
        grid=(m // tm,),
        in_specs=[pl.BlockSpec((tm, D_MODEL), lambda i: (i, 0)),
                  _full((1, D_MODEL)),
                  _full(w_in.shape),
                  pl.BlockSpec((tm, LANES), lambda i: (i % nts, 0)),
                  pl.BlockSpec((tm, LANES), lambda i: (i % nts, 0)),
                  _full((tm, tm)), _full((tm, tm))],
        out_specs=(pl.BlockSpec((3, tm, A_WIDTH), lambda i: (0, i, 0)),
                   pl.BlockSpec((3, None, 4, tm // 4, A_WIDTH), lambda i: (0, i // nts, 0, i % nts, 0)),
                   pl.BlockSpec((3, None, 16, tm // 16, A_WIDTH), lambda i: (0, i // nts, 0, i % nts, 0)),
                   pl.BlockSpec((tm, 3 * B_WIDTH), lambda i: (i, 0))),
        out_shape=out_shape,
        compiler_params=_cparams(("parallel",)),
        name="even_proj",
    )(x, g.reshape(1, D_MODEL), w_in, cos, sin, _dedilate_perm(tm, 4), _dedilate_perm(tm, 16))


def _rope_tables(seq):
    half = ROPE_DIM // 2
    inv = ROPE_THETA ** (-2.0 * jnp.arange(half, dtype=F32) / ROPE_DIM)
    ang = jnp.arange(seq, dtype=F32)[:, None] * inv[None, :]
    cos, sin = jnp.cos(ang), jnp.sin(ang)
    rest = HEAD_DIM - ROPE_DIM
    cos_h = jnp.concatenate([cos, cos, jnp.ones((seq, rest), F32)], axis=1)
    sin_h = jnp.concatenate([-sin, sin, jnp.zeros((seq, rest), F32)], axis=1)
    reps = LANES // HEAD_DIM
    return jnp.tile(cos_h, (1, reps)), jnp.tile(sin_h, (1, reps))


def _dil_kernel(q_ref, kp_ref, kc_ref, kn_ref, vp_ref, vc_ref, vn_ref, o_ref, lse_ref, *, tb, tq, length):
    qi = pl.program_id(1)
    span = tq + 2 * BAND
    k = jnp.concatenate([kp_ref[...], kc_ref[...], kn_ref[...]], axis=0)
    v = jnp.concatenate([vp_ref[...], vc_ref[...], vn_ref[...]], axis=0)
    row = lax.broadcasted_iota(jnp.int32, (tq, span), 0)
    col = lax.broadcasted_iota(jnp.int32, (tq, span), 1)
    rel = col - BAND - row
    band = (rel <= BAND) & (rel >= -BAND)
    first = lax.broadcasted_iota(jnp.int32, (tq, LANES), 1) < HEAD_DIM
    units = [(s_idx, p, hh) for s_idx in range(tb // tq) for p in range(A_WIDTH // LANES) for hh in range(2)]
    lag = DIL_LAG
    scores, maxes, probs, results = {}, {}, {}, {}
    for n in range(len(units) + 3 * lag):
        if n < len(units):
            s_idx, p, hh = units[n]
            sl = slice(p * LANES, (p + 1) * LANES)
            qp = q_ref[s_idx * tq:(s_idx + 1) * tq, sl]
            sel = first if hh == 0 else jnp.logical_not(first)
            key_pos = qi * tb + s_idx * tq - BAND + col
            valid = band & (key_pos >= 0) & (key_pos < length)
            s = _dot_nt(jnp.where(sel, qp, jnp.zeros_like(qp)), k[s_idx * tq:s_idx * tq + span, sl])
            scores[n] = jnp.where(valid, s, NEG_INF)
        m = n - lag
        if 0 <= m < len(units):
            maxes[m] = jnp.max(scores[m], axis=-1, keepdims=True)
        m = n - 2 * lag
        if 0 <= m < len(units):
            mx = maxes.pop(m)
            e = jnp.exp(scores.pop(m) - mx)
            den = jnp.sum(e, axis=-1, keepdims=True)
            probs[m] = (e.astype(BF16), den, mx)
        m = n - 3 * lag
        if 0 <= m < len(units):
            s_idx, p, hh = units[m]
            sl = slice(p * LANES, (p + 1) * LANES)
            e, den, mx = probs.pop(m)
            results[hh] = (_dot(e, v[s_idx * tq:s_idx * tq + span, sl]) * (1.0 / den), mx + jnp.log(den))
            if hh == 1:
                rows = slice(s_idx * tq, (s_idx + 1) * tq)
                o_ref[rows, sl] = jnp.where(first, results[0][0], results[1][0]).astype(BF16)
                lse_ref[rows, sl] = jnp.where(first, results[0][1], results[1][1])


def _dilated_branch(zd, length):
    g = zd.shape[1]
    tb = min(TB_DIL, length)
    tq = min(TQ_DIL, length)
    nb = length // BAND
    r = tb // BAND
    kern = functools.partial(_dil_kernel, tb=tb, tq=tq, length=length)

    def cur(which):
        return pl.BlockSpec((None, None, tb, A_WIDTH), lambda b, i: (which, b, i, 0))

    def prev(which):
        return pl.BlockSpec((None, None, BAND, A_WIDTH), lambda b, i: (which, b, jnp.maximum(i * r - 1, 0), 0))

    def nxt(which):
        return pl.BlockSpec((None, None, BAND, A_WIDTH), lambda b, i: (which, b, jnp.minimum((i + 1) * r, nb - 1), 0))

    return pl.pallas_call(
        kern,
        grid=(g, length // tb),
        in_specs=[cur(0), prev(1), cur(1), nxt(1), prev(2), cur(2), nxt(2)],
        out_specs=(pl.BlockSpec((None, tb, A_WIDTH), lambda b, i: (b, i, 0)),
                   pl.BlockSpec((None, tb, A_WIDTH), lambda b, i: (b, i, 0))),
        out_shape=(jax.ShapeDtypeStruct((g, length, A_WIDTH), BF16),
                   jax.ShapeDtypeStruct((g, length, A_WIDTH), F32)),
        compiler_params=_cparams(("parallel", "parallel")),
        name="dilated_attn",
    )(zd, zd, zd, zd, zd, zd, zd)


def _even_out_kernel(o1_ref, l1_ref, o4_ref, l4_ref, o16_ref, l16_ref, bg_ref, cg_ref, h_ref,
                     cgp_ref, hp_ref, cgn_ref, hn_ref, cw_ref, wa_ref, wb_ref, x_ref, out_ref,
                     s4o, s4l, s16o, s16l, *, tm, nts):
    it = pl.program_id(0) % nts
    nslab = A_WIDTH // LANES
    for c in range(nslab):
        sl = slice(c * LANES, (c + 1) * LANES)
        for r in range(4):
            s4o[c, pl.ds(r, tm // 4, stride=4), :] = o4_ref[r, :, sl].astype(F32)
            s4l[c, pl.ds(r, tm // 4, stride=4), :] = l4_ref[r, :, sl]
        for r in range(16):
            s16o[c, pl.ds(r, tm // 16, stride=16), :] = o16_ref[r, :, sl].astype(F32)
            s16l[c, pl.ds(r, tm // 16, stride=16), :] = l16_ref[r, :, sl]
    wide = lambda ref: jnp.concatenate([ref[c] for c in range(nslab)], axis=1)
    l1, l4, l16 = l1_ref[...], wide(s4l), wide(s16l)
    mx = jnp.maximum(jnp.maximum(l1, l4), l16)
    e1, e4, e16 = jnp.exp(l1 - mx), jnp.exp(l4 - mx), jnp.exp(l16 - mx)
    ya = (e1 * o1_ref[...].astype(F32) + e4 * wide(s4o) + e16 * wide(s16o)) * (1.0 / (e1 + e4 + e16))

    u = cg_ref[...].astype(F32) * h_ref[...].astype(F32)
    last = cgp_ref.shape[0] - 1
    u_before = (cgp_ref[...].astype(F32) * hp_ref[...].astype(F32))[last:last + 1]
    u_after = (cgn_ref[...].astype(F32) * hn_ref[...].astype(F32))[0:1]
    u_before = u_before * (it > 0).astype(F32)
    u_after = u_after * (it < nts - 1).astype(F32)
    row = lax.broadcasted_iota(jnp.int32, u.shape, 0)
    u_prev = jnp.where(row == 0, u_before, pltpu.roll(u, 1, 0))
    u_next = jnp.where(row == tm - 1, u_after, pltpu.roll(u, tm - 1, 0))
    cw = cw_ref[...]
    yb = bg_ref[...].astype(F32) * (cw[0:1] * u_prev + cw[1:2] * u + cw[2:3] * u_next)

    out_ref[...] = x_ref[...] + _dot(ya.astype(BF16), wa_ref[...]) + _dot(yb.astype(BF16), wb_ref[...])


def _even_tail_kernel(*refs, tm, nts, nsub):
    mixer_in, cross_in, out_ref, scratch = refs[:17], refs[17:22], refs[22], refs[23:]
    x_mid = scratch[4]
    _even_out_kernel(*mixer_in, x_mid, *scratch[:4], tm=tm, nts=nts)
    _cross_kernel(x_mid, *cross_in, out_ref, nsub=nsub)


def _cross_specs(seq, n_mem, tm):
    nts = seq // tm
    return [_full((1, D_MODEL)),
            pl.BlockSpec((n_mem, D_MODEL), lambda i: (i // nts, 0)),
            pl.BlockSpec((n_mem, D_MODEL), lambda i: (i // nts, 1)),
            _full((D_MODEL, D_MODEL)),
            _full((D_MODEL, D_MODEL))]


def _even_tail(x, o1, l1, o4, l4, o16, l16, zc, conv_w, w_out, cross, batch, seq):
    g_cross, kv, wq, wo, n_mem = cross
    m = x.shape[0]
    tm = TM
    nts = seq // tm
    hb = 16
    nhb = m // hb
    kern = functools.partial(_even_tail_kernel, tm=tm, nts=nts, nsub=2)
    nat = lambda: pl.BlockSpec((tm, A_WIDTH), lambda i: (i, 0))
    dil = lambda d: pl.BlockSpec((None, d, tm // d, A_WIDTH), lambda i: (i // nts, 0, i % nts, 0))
    col = lambda c: pl.BlockSpec((tm, B_WIDTH), lambda i: (i, c))
    before = lambda c: pl.BlockSpec((hb, B_WIDTH), lambda i: (jnp.maximum(i * (tm // hb) - 1, 0), c))
    after = lambda c: pl.BlockSpec((hb, B_WIDTH), lambda i: (jnp.minimum((i + 1) * (tm // hb), nhb - 1), c))
    return pl.pallas_call(
        kern,
        grid=(m // tm,),
        in_specs=[nat(), nat(), dil(4), dil(4), dil(16), dil(16),
                  col(0), col(1), col(2), before(1), before(2), after(1), after(2),
                  _full((3, B_WIDTH)),
                  pl.BlockSpec((A_WIDTH, D_MODEL), lambda i: (0, 0)),
                  pl.BlockSpec((B_WIDTH, D_MODEL), lambda i: (1, 0)),
                  pl.BlockSpec((tm, D_MODEL), lambda i: (i, 0))] + _cross_specs(seq, n_mem, tm),
        out_specs=pl.BlockSpec((tm, D_MODEL), lambda i: (i, 0)),
        out_shape=jax.ShapeDtypeStruct((m, D_MODEL), F32),
        scratch_shapes=[pltpu.VMEM((A_WIDTH // LANES, tm, LANES), F32)] * 4 + [pltpu.VMEM((tm, D_MODEL), F32)],
        compiler_params=_cparams(("parallel",)),
        name="even_tail",
    )(o1, l1, o4, l4, o16, l16, zc, zc, zc, zc, zc, zc, zc, conv_w, w_out, w_out, x,
      g_cross.reshape(1, D_MODEL), kv, kv, wq, wo)


def _cross_kernel(x_ref, g_ref, k_ref, v_ref, wq_ref, wo_ref, out_ref, *, nsub):
    rows = x_ref.shape[0] // nsub
    g = g_ref[...]
    heads = [slice(h * CA_HEAD_DIM, (h + 1) * CA_HEAD_DIM) for h in range(CA_HEADS)]
    k = k_ref[...]
    v = v_ref[...]
    qs = []
    for i in range(nsub):
        xn = _rms(x_ref[i * rows:(i + 1) * rows, :], g).astype(BF16)
        qs.append((_dot(xn, wq_ref[...]) * (CA_HEAD_DIM ** -0.5)).astype(BF16))
    units = [(i, h) for i in range(nsub) for h in range(CA_HEADS)]
    scores, probs, outs = {}, {}, {}
    for step in range(len(units) + 2):
        if step < len(units):
            i, h = units[step]
            scores[step] = _dot_nt(qs[i][:, heads[h]], k[:, heads[h]])
        if 0 <= step - 1 < len(units):
            s = scores.pop(step - 1)
            e = jnp.exp(s - jnp.max(s, axis=-1, keepdims=True))
            probs[step - 1] = (e * (1.0 / jnp.sum(e, axis=-1, keepdims=True))).astype(BF16)
        if 0 <= step - 2 < len(units):
            i, h = units[step - 2]
            outs[i, h] = _dot(probs.pop(step - 2), v[:, heads[h]]).astype(BF16)
            if h == CA_HEADS - 1:
                o = jnp.concatenate([outs.pop((i, hh)) for hh in range(CA_HEADS)], axis=1)
                out_ref[i * rows:(i + 1) * rows, :] = x_ref[i * rows:(i + 1) * rows, :] + _dot(o, wo_ref[...])


def _ffn_kernel(x_ref, g_ref, wgu_ref, wd_ref, gf_ref, out_ref, *, final_norm, tf, nsub):
    rows = x_ref.shape[0] // nsub
    g = g_ref[...]
    accs = [x_ref[i * rows:(i + 1) * rows, :] for i in range(nsub)]
    xns = [_rms(x, g).astype(BF16) for x in accs]
    nf = D_FF // tf
    acts = None
    for f in range(nf + 1):
        new_acts = []
        if f < nf:
            for i in range(nsub):
                gate = _dot(xns[i], wgu_ref[:, f * tf:(f + 1) * tf])
                up = _dot(xns[i], wgu_ref[:, D_FF + f * tf:D_FF + (f + 1) * tf])
                new_acts.append((gate * _sigmoid(gate) * up).astype(BF16))
        if f > 0:
            for i in range(nsub):
                accs[i] = accs[i] + _dot(acts[i], wd_ref[(f - 1) * tf:f * tf, :])
        acts = new_acts
    for i in range(nsub):
        y = _rms(accs[i], gf_ref[...]) if final_norm else accs[i]
        out_ref[i * rows:(i + 1) * rows, :] = y


def _ffn(x, g, w_gu, w_down, g_final, final_norm):
    m = x.shape[0]
    tm = TM
    kern = functools.partial(_ffn_kernel, final_norm=final_norm, tf=TF_FFN, nsub=2)
    resident = lambda shape: pl.BlockSpec(shape, lambda i: (0, 0), pipeline_mode=pl.Buffered(1))
    return pl.pallas_call(
        kern,
        grid=(m // tm,),
        in_specs=[pl.BlockSpec((tm, D_MODEL), lambda i: (i, 0)),
                  _full((1, D_MODEL)),
                  resident(w_gu.shape),
                  resident(w_down.shape),
                  _full((1, D_MODEL))],
        out_specs=pl.BlockSpec((tm, D_MODEL), lambda i: (i, 0)),
        out_shape=jax.ShapeDtypeStruct((m, D_MODEL), F32),
        compiler_params=_cparams(("parallel",)),
        name="swiglu",
    )(x, g.reshape(1, D_MODEL), w_gu, w_down, g_final.reshape(1, D_MODEL))


def _head_sum(x, e_ref):
    x = x.astype(BF16)
    e = e_ref[...]
    width = e.shape[0]
    return jnp.concatenate([_dot(x[:, c * width:(c + 1) * width], e) for c in range(x.shape[1] // width)],
                           axis=1)


def _sigmoid(x):
    return 0.5 * jnp.tanh(0.5 * x) + 0.5


def _store_pairs(ref, val, *lead):
    for p in range(val.shape[1] // LANES):
        ref[(*lead, p)] = val[:, p * LANES:(p + 1) * LANES].astype(ref.dtype)


def _rwkv_pre_kernel(x_ref, xb_ref, xa_ref, g_ref, mu_ref, wr_ref, wk_ref, wv_ref, w1_ref, w2_ref, w0_ref,
                     a1_ref, a2_ref, a0_ref, g1_ref, g2_ref, kk_ref, ka_ref, rk_ref, e_ref,
                     sm_ref, r_out, v_out, kn_out, kd_out, ba_out, lw_out, gate_out, bonus_out, *, nts):
    it = pl.program_id(0) % nts
    g = g_ref[...]
    xn_bf = _rms(x_ref[...], g).astype(BF16)
    before = (_rms(xb_ref[...], g) * (it > 0).astype(F32)).astype(BF16)
    after = (_rms(xa_ref[...], g) * (it < nts - 1).astype(F32)).astype(BF16)
    xx_bf = _dot(sm_ref[...], jnp.concatenate([before, xn_bf, after], axis=0)).astype(BF16)
    mu = mu_ref[...].astype(BF16)
    mix = lambda i: xn_bf + xx_bf * mu[i:i + 1]

    tw = jnp.tanh(_dot(mix(1), w1_ref[...])).astype(BF16)
    ta = _dot(mix(4), a1_ref[...]).astype(BF16)
    tg = _sigmoid(_dot(mix(5), g1_ref[...])).astype(BF16)
    k_lin = _dot(mix(2), wk_ref[...])
    wls = [w0_ref[d:d + 1] + _dot(tw, w2_ref[d]) for d in range(2)]
    r = _dot(mix(0), wr_ref[...])
    for d in range(2):
        _store_pairs(lw_out, (-0.5 * DECAY_SCALE) * jnp.tanh(0.5 * wls[d]) - 0.5 * DECAY_SCALE, d)
    kn = k_lin * kk_ref[...]
    kn = kn * lax.rsqrt(jnp.maximum(_head_sum(kn * kn, e_ref), 1e-24))
    _store_pairs(kn_out, kn)
    als = [a0_ref[d:d + 1] + _dot(ta, a2_ref[d]) for d in range(2)]
    v = _dot(mix(3), wv_ref[...])
    hk = k_lin * (0.5 * ka_ref[...])
    k_base = k_lin - hk
    hn = 0.5 * kn
    kd_sum = jnp.zeros_like(k_lin)
    for d in range(2):
        t = jnp.tanh(0.5 * als[d])
        kd = k_base + hk * t
        kd_sum = kd_sum + kd
        _store_pairs(kd_out, kd, d)
        _store_pairs(ba_out, hn + hn * t, d)
    gate_out[...] = _dot(tg, g2_ref[...]).astype(BF16)
    _store_pairs(r_out, r)
    _store_pairs(v_out, v)
    bonus_out[...] = (_head_sum(r * kd_sum * rk_ref[...], e_ref) * v).astype(BF16)


def _rwkv_pre(x, g, p, batch, seq):
    m = x.shape[0]
    tm = TM_RW
    nts = seq // tm
    hb = 16
    nhb = m // hb
    kern = functools.partial(_rwkv_pre_kernel, nts=nts)
    row = lambda: pl.BlockSpec((tm, D_MODEL), lambda i: (i, 0))
    npair = D_MODEL // LANES
    pairs = lambda: pl.BlockSpec((npair, tm, LANES), lambda i: (0, i, 0))
    pairs2 = lambda: pl.BlockSpec((2, npair, tm, LANES), lambda i: (0, 0, i, 0))
    t = jnp.arange(tm)[:, None]
    c = jnp.arange(tm + 2 * hb)[None, :] - hb
    shift = (jnp.where(c == t, -1.0, 0.0) + jnp.where(jnp.abs(c - t) == 1, 0.5, 0.0)).astype(BF16)
    args = (x, x, x, g.reshape(1, D_MODEL), p['mu'], p['wr'], p['wk'], p['wv'], p['w1'], p['w2'], p['w0'],
            p['a1'], p['a2'], p['a0'], p['g1'], p['g2'], p['kk'], p['ka'], p['rk'], p['e'], shift)
    in_specs = [row(),
                pl.BlockSpec((hb, D_MODEL), lambda i: (jnp.maximum(i * (tm // hb) - 1, 0), 0)),
                pl.BlockSpec((hb, D_MODEL), lambda i: (jnp.minimum((i + 1) * (tm // hb), nhb - 1), 0))]
    in_specs += [_full(a.shape) for a in args[3:]]
    out_shape = (jax.ShapeDtypeStruct((npair, m, LANES), BF16),) * 3 + (
        jax.ShapeDtypeStruct((2, npair, m, LANES), BF16), jax.ShapeDtypeStruct((2, npair, m, LANES), BF16),
        jax.ShapeDtypeStruct((2, npair, m, LANES), F32),
        jax.ShapeDtypeStruct((m, D_MODEL), BF16), jax.ShapeDtypeStruct((m, D_MODEL), BF16))
    out_specs = (pairs(), pairs(), pairs(), pairs2(), pairs2(), pairs2(), row(), row())
    return pl.pallas_call(
        kern, grid=(m // tm,), in_specs=in_specs, out_specs=out_specs, out_shape=out_shape,
        compiler_params=_cparams(("parallel",)), name="rwkv_pre",
    )(*args)


def _bd(x, first):
    z = jnp.zeros_like(x)
    return jnp.concatenate([jnp.where(first, x, z), jnp.where(first, z, x)], axis=0)


def _compact(full, first):
    return jnp.where(first, full[:CHUNK], full[CHUNK:])


def _wkv_masks(sgn):
    c = CHUNK
    row = lax.broadcasted_iota(jnp.int32, (c, c), 0)
    col = lax.broadcasted_iota(jnp.int32, (c, c), 1)
    tri = jnp.where((row - col) * sgn >= 0, 1.0, 0.0).astype(BF16)
    prow = lax.broadcasted_iota(jnp.int32, (c, LANES), 0)
    pcol = lax.broadcasted_iota(jnp.int32, (c, LANES), 1) & (c - 1)
    dlt = (prow - pcol) * sgn
    strict = dlt > 0
    same = lambda sh: (prow >> sh) == (pcol >> sh)
    levels = []
    sh = 2
    while (1 << sh) < c:
        levels.append(jnp.where(strict & same(sh + 1) & jnp.logical_not(same(sh)), 1.0, 0.0))
        sh += 1
    in4 = same(2)
    return dict(tri=tri, strict=strict, incl=dlt >= 0, eye=jnp.where(prow == pcol, 1.0, 0.0), levels=levels,
                tri4=jnp.where(strict & in4, 1.0, 0.0),
                off2=jnp.where(in4 & (dlt == 2), 1.0, 0.0), off3=jnp.where(in4 & (dlt == 3), 1.0, 0.0),
                row1=sgn % c, row2=(2 * sgn) % c, lane1=(-sgn) % LANES, lane2=(-2 * sgn) % LANES)


def _inverse4(a_ab, mk):
    a = a_ab * mk['tri4']
    up1 = pltpu.roll(a, mk['row1'], 0)
    up2 = pltpu.roll(a, mk['row2'], 0)
    left1 = pltpu.roll(a, mk['lane1'], 1)
    left2 = pltpu.roll(a, mk['lane2'], 1)
    sq = left1 * up1 * mk['off2'] + (left1 * up2 + left2 * up1) * mk['off3']
    cube = pltpu.roll(sq, mk['lane1'], 1) * up2 * mk['off3']
    return mk['eye'] + a + sq + cube


def _wkv_local(chains, first):
    c = CHUNK
    for s in chains:
        hl = _dot(s['mk']['tri'], jnp.concatenate(_split(s['lw']), axis=1))
        s['cum'] = hl[:, :LANES] + hl[:, LANES:]
    yield
    for s in chains:
        cum, lw = s['cum'], s['lw']
        total = jnp.sum(lw, axis=0, keepdims=True)
        p_inv = jnp.exp(-cum)
        p_hat = jnp.exp(total - cum)
        s['at'] = (-s['kn'] * jnp.exp(cum - lw)).astype(BF16)
        s['rt'] = (s['r'] * jnp.exp(cum)).astype(BF16)
        s['bt'] = (s['ba'] * p_inv).astype(BF16)
        s['kt'] = (s['kd'] * p_inv).astype(BF16)
        s['bh'] = (s['ba'] * p_hat).astype(BF16)
        s['kh'] = (s['kd'] * p_hat).astype(BF16)
        s['v_bf'] = s['v'].astype(BF16)
        s['p_end'] = jnp.exp(total)
        s['lhs'] = jnp.concatenate([s['at'], s['rt']], axis=0)
    yield
    for s in chains:
        s['sbk'] = _dot_nt(s['lhs'], jnp.concatenate([_bd(s['bt'], first), _bd(s['kt'], first)], axis=0))
    yield
    for s in chains:
        mk = s['mk']
        sb, sk = s['sbk'][:, :LANES], s['sbk'][:, LANES:]
        s['a_ab'] = jnp.where(mk['strict'], sb[:c], 0.0)
        s['a_rb'] = jnp.where(mk['incl'], sb[c:], 0.0).astype(BF16)
        s['a_kk'] = jnp.concatenate([jnp.where(mk['strict'], sk[:c], 0.0),
                                     jnp.where(mk['incl'], sk[c:], 0.0)], axis=0).astype(BF16)
        s['inv'] = _inverse4(s['a_ab'], mk)
    yield
    for s in chains:
        s['av'] = _dot(s['a_kk'], _bd(s['v_bf'], first))
        s['y0'] = _compact(_dot_tn(s['v_bf'], s['kh']), first)
    yield
    for lvl in range(len(chains[0]['mk']['levels'])):
        for s in chains:
            s['inv_bf'] = s['inv'].astype(BF16)
            s['inner'] = _dot((s['a_ab'] * s['mk']['levels'][lvl]).astype(BF16), _bd(s['inv_bf'], first))
        yield
        for s in chains:
            s['inv'] = s['inv'] + _dot(s['inv_bf'], _bd(s['inner'].astype(BF16), first))
        yield
    for s in chains:
        rhs = jnp.concatenate([_bd(s['at'], first), _bd(s['av'][:c].astype(BF16), first)], axis=1)
        wu = _dot(s['inv'].astype(BF16), rhs)
        s['w1'] = wu[:, :LANES].astype(BF16)
        s['u0'] = wu[:, LANES:]
        s['o0'] = s['av'][c:]


def _wkv_seq(rows, ys, out_refs, first):
    c = CHUNK
    for row in rows:
        wrs = [_dot_nt(jnp.concatenate([s['w1'], s['rt']], axis=0), _bd(y.astype(BF16), first))
               for s, y in zip(row, ys)]
        yield
        us = [(wr[:c] + s['u0']).astype(BF16) for s, wr in zip(row, wrs)]
        for s, wr, u, (o_ref, p) in zip(row, wrs, us, out_refs):
            o_ref[p, s['sl'], :] = (wr[c:] + _dot(s['a_rb'], _bd(u, first)) + s['o0']).astype(o_ref.dtype)
        ys[:] = [s['p_end'] * y + _compact(_dot_tn(u, s['bh']), first) + s['y0'] for s, y, u in zip(row, ys, us)]
        yield


def _run_staggered(local_gens, make_seq, period):
    end = object()
    started, finished, tick = 0, 0, 0
    active, updates = [], []
    while finished < len(local_gens) or updates:
        if started < len(local_gens) and tick >= started * period:
            active.append(started)
            started += 1
        for g in list(active):
            if next(local_gens[g], end) is end:
                active.remove(g)
                finished += 1
                updates.append(make_seq(g))
        if updates and next(updates[0], end) is end:
            updates.pop(0)
        tick += 1


def _wkv_kernel(rf, vf, nf, kdf, baf, lwf, rb, vb, nb, kdb, bab, lwb, of_ref, ob_ref, state, *, ts, npairs):
    first = lax.broadcasted_iota(jnp.int32, (CHUNK, LANES), 1) < RWKV_HEAD

    @pl.when(pl.program_id(2) == 0)
    def _():
        state[...] = jnp.zeros_like(state)

    nchunk = ts // CHUNK
    names = ('r', 'v', 'kn', 'kd', 'ba', 'lw')
    dirs = ((_wkv_masks(1), (rf, vf, nf, kdf, baf, lwf), of_ref, list(range(nchunk))),
            (_wkv_masks(-1), (rb, vb, nb, kdb, bab, lwb), ob_ref, list(range(nchunk - 1, -1, -1))))
    scans = [(d, p) for d in range(2) for p in range(npairs)]
    out_refs = [(dirs[d][2], p) for d, p in scans]
    steps = []
    for k in range(nchunk):
        row = []
        for d, p in scans:
            mk, ins, _, order = dirs[d]
            sl = pl.ds(order[k] * CHUNK, CHUNK)
            chain = {n: ref[p, sl, :].astype(F32) for n, ref in zip(names, ins)}
            chain.update(mk=mk, sl=sl)
            row.append(chain)
        steps.append(row)
    ys = [state[d, p] for d, p in scans]
    groups = [steps[k:k + WKV_GROUP] for k in range(0, nchunk, WKV_GROUP)]
    _run_staggered([_wkv_local([s for row in rows for s in row], first) for rows in groups],
                   lambda g: _wkv_seq(groups[g], ys, out_refs, first), WKV_STAGGER)
    for (d, p), y in zip(scans, ys):
        state[d, p] = y


def _wkv(r, v, kn, kd, ba, lw, batch, seq):
    ts, npairs = TS_WKV, PAIRS_WKV
    nt = seq // ts
    npair = D_MODEL // LANES
    m = batch * seq
    fwd = lambda b, t: b * nt + t
    bwd = lambda b, t: b * nt + nt - 1 - t
    s3 = lambda at: pl.BlockSpec((npairs, ts, LANES), lambda b, g, t: (g, at(b, t), 0))
    s4 = lambda d, at: pl.BlockSpec((None, npairs, ts, LANES), lambda b, g, t: (d, g, at(b, t), 0))
    return pl.pallas_call(
        functools.partial(_wkv_kernel, ts=ts, npairs=npairs),
        grid=(batch, npair // npairs, nt),
        in_specs=[s3(fwd), s3(fwd), s3(fwd), s4(0, fwd), s4(0, fwd), s4(0, fwd),
                  s3(bwd), s3(bwd), s3(bwd), s4(1, bwd), s4(1, bwd), s4(1, bwd)],
        out_specs=(s3(fwd), s3(bwd)),
        out_shape=(jax.ShapeDtypeStruct((npair, m, LANES), BF16),) * 2,
        scratch_shapes=[pltpu.VMEM((2, npairs, CHUNK, LANES), F32)],
        compiler_params=_cparams(("parallel", "parallel", "arbitrary")),
        name="wkv",
    )(r, v, kn, kd, ba, lw, r, v, kn, kd, ba, lw)


def _rwkv_post_kernel(of_ref, ob_ref, bonus_ref, gate_ref, lw_ref, lb_ref, e_ref, wo_ref, x_ref, out_ref):
    y = jnp.concatenate([of_ref[p].astype(F32) + ob_ref[p].astype(F32) for p in range(of_ref.shape[0])], axis=1)
    mean = _head_sum(y, e_ref) * (1.0 / RWKV_HEAD)
    yc = y - mean
    var = _head_sum(yc * yc, e_ref) * (1.0 / RWKV_HEAD)
    yn = yc * lax.rsqrt(var + GN_EPS)
    y2 = yn * lw_ref[...] + lb_ref[...] + bonus_ref[...].astype(F32)
    out_ref[...] = x_ref[...] + _dot((y2 * gate_ref[...].astype(F32)).astype(BF16), wo_ref[...])


def _odd_tail_kernel(*refs, nsub):
    mixer_in, cross_in, out_ref, x_mid = refs[:9], refs[9:14], refs[14], refs[15]
    _rwkv_post_kernel(*mixer_in, x_mid)
    _cross_kernel(x_mid, *cross_in, out_ref, nsub=nsub)


def _odd_tail(x, o_f, o_b, bonus, gate, p, cross, seq):
    g_cross, kv, wq, wo, n_mem = cross
    m = x.shape[0]
    tm = TM
    row = lambda: pl.BlockSpec((tm, D_MODEL), lambda i: (i, 0))
    pairs = lambda: pl.BlockSpec((D_MODEL // LANES, tm, LANES), lambda i: (0, i, 0))
    return pl.pallas_call(
        functools.partial(_odd_tail_kernel, nsub=2),
        grid=(m // tm,),
        in_specs=[pairs(), pairs(), row(), row(), _full((1, D_MODEL)), _full((1, D_MODEL)),
                  _full(p['e'].shape), _full((D_MODEL, D_MODEL)), row()] + _cross_specs(seq, n_mem, tm),
        out_specs=row(),
        out_shape=jax.ShapeDtypeStruct((m, D_MODEL), F32),
        scratch_shapes=[pltpu.VMEM((tm, D_MODEL), F32)],
        compiler_params=_cparams(("parallel",)),
        name="odd_tail",
    )(o_f, o_b, bonus, gate, p['lnx_w'], p['lnx_b'], p['e'], p['wo'], x,
      g_cross.reshape(1, D_MODEL), kv, kv, wq, wo)


def _prep_odd(o, rw_mu, rw_wr, rw_wk, rw_wv, rw_wo, rw_w0, rw_w1, rw_w2, rw_a0, rw_a1, rw_a2, rw_g1, rw_g2,
              rw_kk, rw_ka, rw_rk, rw_lnx_w, rw_lnx_b):
    bf = lambda a: a.astype(BF16)

    def pad_dir(w):
        z = jnp.zeros_like(w[0])
        return jnp.stack([jnp.concatenate([w[0], z], axis=0), jnp.concatenate([z, w[1]], axis=0)])

    lane = jnp.arange(MXU_WIDTH) // RWKV_HEAD
    return dict(
        mu=rw_mu[o], wr=bf(rw_wr[o]), wk=bf(rw_wk[o]), wv=bf(rw_wv[o]), wo=bf(rw_wo[o]),
        w0=rw_w0[o], w1=bf(jnp.concatenate([rw_w1[o, 0], rw_w1[o, 1]], axis=1)), w2=bf(pad_dir(rw_w2[o])),
        a0=rw_a0[o], a1=bf(jnp.concatenate([rw_a1[o, 0], rw_a1[o, 1]], axis=1)), a2=bf(pad_dir(rw_a2[o])),
        g1=bf(rw_g1[o]), g2=bf(rw_g2[o]),
        kk=rw_kk[o].reshape(1, D_MODEL), ka=rw_ka[o].reshape(1, D_MODEL), rk=rw_rk[o].reshape(1, D_MODEL),
        lnx_w=rw_lnx_w[o].reshape(1, D_MODEL), lnx_b=rw_lnx_b[o].reshape(1, D_MODEL),
        e=(lane[:, None] == lane[None, :]).astype(BF16))


def _even_layer(x, g, w_in, w_out, conv_w, cos, sin, cross, batch, seq):
    z1, z4, z16, zc = _even_proj(x, g, w_in, cos, sin, batch, seq)
    m = x.shape[0]
    o1, l1 = _dilated_branch(z1.reshape(3, batch, seq, A_WIDTH), seq)
    o4, l4 = _dilated_branch(z4.reshape(3, batch * 4, seq // 4, A_WIDTH), seq // 4)
    o16, l16 = _dilated_branch(z16.reshape(3, batch * 16, seq // 16, A_WIDTH), seq // 16)
    o1, l1 = o1.reshape(m, A_WIDTH), l1.reshape(m, A_WIDTH)
    d4 = lambda a: a.reshape(batch, 4, seq // 4, A_WIDTH)
    d16 = lambda a: a.reshape(batch, 16, seq // 16, A_WIDTH)
    return _even_tail(x, o1, l1, d4(o4), d4(l4), d16(o16), d16(l16), zc, conv_w, w_out, cross, batch, seq)


def _odd_layer(x, g, p, cross, batch, seq):
    r, v, kn, kd, ba, lw, gate, bonus = _rwkv_pre(x, g, p, batch, seq)
    o_f, o_b = _wkv(r, v, kn, kd, ba, lw, batch, seq)
    return _odd_tail(x, o_f, o_b, bonus, gate, p, cross, seq)


def _trunk(x, mem, w):
    batch, seq, _ = x.shape
    n_mem = mem.shape[1]
    x = x.reshape(batch * seq, D_MODEL)
    mem = mem.reshape(batch * n_mem, D_MODEL)
    cos, sin = _rope_tables(seq)
    for l in range(DEPTH):
        kv = _norm_matmul(mem, w['norm_mem'][l], w['ca_wkv'][l], TM, 2 * D_MODEL, BF16)
        cross = (w['norm_cross'][l], kv, w['ca_wq'][l], w['ca_wo'][l], n_mem)
        if l % 2 == 0:
            e = l // 2
            x = _even_layer(x, w['norm_mix'][l], w['ab_w_in'][e], w['ab_w_out'][e], w['ab_conv'][e],
                            cos, sin, cross, batch, seq)
        else:
            x = _odd_layer(x, w['norm_mix'][l], w['odd'][l // 2], cross, batch, seq)
        x = _ffn(x, w['norm_ffn'][l], w['ffn_wgu'][l], w['ffn_wdown'][l], w['norm_final'], l == DEPTH - 1)
    return x.reshape(batch, seq, D_MODEL)


def kernel(x_prompt, x_sample, mem_prompt, mem_sample, norm_mix, norm_cross, norm_mem, norm_ffn, norm_final,
           ab_w_in, ab_w_out, ab_conv, rw_mu, rw_wr, rw_wk, rw_wv, rw_wo, rw_w0, rw_w1, rw_w2, rw_a0, rw_a1,
           rw_a2, rw_g1, rw_g2, rw_kk, rw_ka, rw_rk, rw_lnx_w, rw_lnx_b, ca_wq, ca_wkv, ca_wo, ffn_wgu,
           ffn_wdown):
    bf = lambda a: a.astype(BF16)
    w = dict(norm_mix=norm_mix, norm_cross=norm_cross, norm_mem=norm_mem, norm_ffn=norm_ffn,
             norm_final=norm_final, ab_w_in=bf(ab_w_in), ab_w_out=bf(ab_w_out), ab_conv=ab_conv,
             ca_wq=bf(ca_wq), ca_wkv=bf(ca_wkv), ca_wo=bf(ca_wo), ffn_wgu=bf(ffn_wgu), ffn_wdown=bf(ffn_wdown),
             odd=[_prep_odd(o, rw_mu, rw_wr, rw_wk, rw_wv, rw_wo, rw_w0, rw_w1, rw_w2, rw_a0, rw_a1, rw_a2,
                            rw_g1, rw_g2, rw_kk, rw_ka, rw_rk, rw_lnx_w, rw_lnx_b)
                  for o in range(rw_mu.shape[0])])
    return _trunk(x_prompt, mem_prompt, w), _trunk(x_sample, mem_sample, w)
```

```python
import functools

import jax
import jax.numpy as jnp
from jax import lax
from jax.experimental import pallas as pl
from jax.experimental.pallas import tpu as pltpu

F32 = jnp.float32
BF16 = jnp.bfloat16

D_MODEL = 1024
DEPTH = 4
HEAD_DIM = 64
A_WIDTH = 512
B_WIDTH = 512
BAND = 64
ROPE_THETA = 500000.0
ROPE_DIM = 16
RWKV_HEAD = 64
GN_EPS = 64e-5
DECAY_SCALE = 0.6065306597126334
CA_HEADS = 4
CA_HEAD_DIM = 256
D_FF = 2816
RMS_EPS = 1e-6
NEG_INF = -1e30
CHUNK = 64
LANES = 128
MXU_WIDTH = 256

VMEM_LIMIT = 48 * 1024 * 1024

TM = 512
TM_RW = 256
TB_DIL = 512
TQ_DIL = 128
DIL_LAG = 1
TF_FFN = 256
TS_WKV = 512
PAIRS_WKV = 4
WKV_GROUP = 2
WKV_STAGGER = 8


def _cparams(sem):
    return pltpu.CompilerParams(dimension_semantics=sem, vmem_limit_bytes=VMEM_LIMIT)


def _full(shape):
    n = len(shape)
    return pl.BlockSpec(shape, lambda *_: (0,) * n)


def _rms(x, g):
    ms = jnp.mean(x * x, axis=-1, keepdims=True)
    return x * lax.rsqrt(ms + RMS_EPS) * g


def _dot(a, b):
    return jnp.dot(a, b, preferred_element_type=F32)


def _dot_nt(a, b):
    return lax.dot_general(a, b, (((1,), (1,)), ((), ())), preferred_element_type=F32)


def _dot_tn(a, b):
    return lax.dot_general(a, b, (((0,), (0,)), ((), ())), preferred_element_type=F32)


def _split(x):
    hi = x.astype(BF16)
    lo = (x - hi.astype(F32)).astype(BF16)
    return hi, lo


def _norm_matmul_kernel(x_ref, g_ref, w_ref, o_ref, xn_ref):
    @pl.when(pl.program_id(1) == 0)
    def _():
        xn_ref[...] = _rms(x_ref[...], g_ref[...]).astype(BF16)

    o_ref[...] = _dot(xn_ref[...], w_ref[...]).astype(o_ref.dtype)


def _norm_matmul(x, g, w, tm, tn, out_dtype):
    m, k = x.shape
    n = w.shape[1]
    return pl.pallas_call(
        _norm_matmul_kernel,
        grid=(m // tm, n // tn),
        in_specs=[pl.BlockSpec((tm, k), lambda i, j: (i, 0)),
                  pl.BlockSpec((1, k), lambda i, j: (0, 0)),
                  pl.BlockSpec((k, tn), lambda i, j: (0, j))],
        out_specs=pl.BlockSpec((tm, tn), lambda i, j: (i, j)),
        out_shape=jax.ShapeDtypeStruct((m, n), out_dtype),
        scratch_shapes=[pltpu.VMEM((tm, k), BF16)],
        compiler_params=_cparams(("parallel", "arbitrary")),
        name="norm_matmul",
    )(x, g.reshape(1, k), w)


def _even_proj_kernel(x_ref, g_ref, w_ref, cos_ref, sin_ref, p4_ref, p16_ref,
                      o1_ref, o4_ref, o16_ref, oc_ref, *, tm):
    xn = _rms(x_ref[...], g_ref[...]).astype(BF16)
    reps = A_WIDTH // LANES
    cos = jnp.concatenate([cos_ref[...]] * reps, axis=1)
    sin = jnp.concatenate([sin_ref[...]] * reps, axis=1)
    lane = lax.broadcasted_iota(jnp.int32, (tm, A_WIDTH), 1) & (HEAD_DIM - 1)
    half = ROPE_DIM // 2
    nparts = 3
    acc = None
    for j in range(nparts + 1):
        nxt = _dot(xn, w_ref[:, j * A_WIDTH:(j + 1) * A_WIDTH]) if j < nparts else None
        if j == nparts:
            oc_ref[...] = _dot(xn, w_ref[:, 3 * A_WIDTH:]).astype(BF16)
        if j > 0:
            a, part = acc, j - 1
            if part < 2:
                partner = jnp.where(lane < half, pltpu.roll(a, A_WIDTH - half, 1), pltpu.roll(a, half, 1))
                a = a * cos + partner * sin
                if part == 0:
                    a = a * (HEAD_DIM ** -0.5)
            a = a.astype(BF16)
            o1_ref[part] = a
            for d, p_ref, o_ref in ((4, p4_ref, o4_ref), (16, p16_ref, o16_ref)):
                perm = _dot(p_ref[...], a).astype(BF16)
                for r in range(d):
                    o_ref[part, r] = perm[r * (tm // d):(r + 1) * (tm // d)]
        acc = nxt


def _dedilate_perm(tm, d):
    i = jnp.arange(tm)
    src = (i % (tm // d)) * d + i // (tm // d)
    return (src[:, None] == jnp.arange(tm)[None, :]).astype(BF16)


def _even_proj(x, g, w_in, cos, sin, batch, seq):
    m = x.shape[0]
    tm = TM
    nts = seq // tm
    kern = functools.partial(_even_proj_kernel, tm=tm)
    out_shape = (jax.ShapeDtypeStruct((3, m, A_WIDTH), BF16),
                 jax.ShapeDtypeStruct((3, batch, 4, seq // 4, A_WIDTH), BF16),
                 jax.ShapeDtypeStruct((3, batch, 16, seq // 16, A_WIDTH), BF16),
                 jax.ShapeDtypeStruct((m, 3 * B_WIDTH), BF16))
    return pl.pallas_call(
        kern,
        grid=(m // tm,),
        in_specs=[pl.BlockSpec((tm, D_MODEL), lambda i: (i, 0)),
                  _full((1, D_MODEL)),
                  _full(w_in.shape),
                  pl.BlockSpec((tm, LANES), lambda i: (i % nts, 0)),
                  pl.BlockSpec((tm, LANES), lambda i: (i % nts, 0)),
                  _full((tm, tm)), _full((tm, tm))],
        out_specs=(pl.BlockSpec((3, tm, A_WIDTH), lambda i: (0, i, 0)),
                   pl.BlockSpec((3, None, 4, tm // 4, A_WIDTH), lambda i: (0, i // nts, 0, i % nts, 0)),
                   pl.BlockSpec((3, None, 16, tm // 16, A_WIDTH), lambda i: (0, i // nts, 0, i % nts, 0)),
                   pl.BlockSpec((tm, 3 * B_WIDTH), lambda i: (i, 0))),
        out_shape=out_shape,
        compiler_params=_cparams(("parallel",)),
        name="even_proj",
    )(x, g.reshape(1, D_MODEL), w_in, cos, sin, _dedilate_perm(tm, 4), _dedilate_perm(tm, 16))


def _rope_tables(seq):
    half = ROPE_DIM // 2
    inv = ROPE_THETA ** (-2.0 * jnp.arange(half, dtype=F32) / ROPE_DIM)
    ang = jnp.arange(seq, dtype=F32)[:, None] * inv[None, :]
    cos, sin = jnp.cos(ang), jnp.sin(ang)
    rest = HEAD_DIM - ROPE_DIM
    cos_h = jnp.concatenate([cos, cos, jnp.ones((seq, rest), F32)], axis=1)
    sin_h = jnp.concatenate([-sin, sin, jnp.zeros((seq, rest), F32)], axis=1)
    reps = LANES // HEAD_DIM
    return jnp.tile(cos_h, (1, reps)), jnp.tile(sin_h, (1, reps))


def _dil_kernel(q_ref, kp_ref, kc_ref, kn_ref, vp_ref, vc_ref, vn_ref, o_ref, lse_ref, *, gs, tb, tq, length):
    qi = pl.program_id(1)
    span = tq + 2 * BAND
    ks = [jnp.concatenate([kp_ref[g], kc_ref[g], kn_ref[g]], axis=0) for g in range(gs)]
    vs = [jnp.concatenate([vp_ref[g], vc_ref[g], vn_ref[g]], axis=0) for g in range(gs)]
    row = lax.broadcasted_iota(jnp.int32, (tq, span), 0)
    col = lax.broadcasted_iota(jnp.int32, (tq, span), 1)
    rel = col - BAND - row
    band = (rel <= BAND) & (rel >= -BAND)
    first = lax.broadcasted_iota(jnp.int32, (tq, LANES), 1) < HEAD_DIM
    units = [(g, s_idx, p, hh) for g in range(gs) for s_idx in range(tb // tq)
             for p in range(A_WIDTH // LANES) for hh in range(2)]
    lag = DIL_LAG
    scores, maxes, probs, results = {}, {}, {}, {}
    for n in range(len(units) + 3 * lag):
        if n < len(units):
            g, s_idx, p, hh = units[n]
            sl = slice(p * LANES, (p + 1) * LANES)
            qp = q_ref[g, s_idx * tq:(s_idx + 1) * tq, sl]
            sel = first if hh == 0 else jnp.logical_not(first)
            key_pos = qi * tb + s_idx * tq - BAND + col
            valid = band & (key_pos >= 0) & (key_pos < length)
            s = _dot_nt(jnp.where(sel, qp, jnp.zeros_like(qp)), ks[g][s_idx * tq:s_idx * tq + span, sl])
            scores[n] = jnp.where(valid, s, NEG_INF)
        m = n - lag
        if 0 <= m < len(units):
            maxes[m] = jnp.max(scores[m], axis=-1, keepdims=True)
        m = n - 2 * lag
        if 0 <= m < len(units):
            mx = maxes.pop(m)
            e = jnp.exp(scores.pop(m) - mx)
            den = jnp.sum(e, axis=-1, keepdims=True)
            probs[m] = (e.astype(BF16), den, mx)
        m = n - 3 * lag
        if 0 <= m < len(units):
            g, s_idx, p, hh = units[m]
            sl = slice(p * LANES, (p + 1) * LANES)
            e, den, mx = probs.pop(m)
            results[hh] = (_dot(e, vs[g][s_idx * tq:s_idx * tq + span, sl]) * (1.0 / den), mx + jnp.log(den))
            if hh == 1:
                rows = slice(s_idx * tq, (s_idx + 1) * tq)
                o_ref[g, rows, sl] = jnp.where(first, results[0][0], results[1][0]).astype(BF16)
                lse_ref[g, rows, sl] = jnp.where(first, results[0][1], results[1][1])


def _dilated_branch(zd, length):
    g = zd.shape[1]
    tb = min(TB_DIL, length)
    tq = min(TQ_DIL, length)
    gs = TB_DIL // tb
    nb = length // BAND
    r = tb // BAND
    kern = functools.partial(_dil_kernel, gs=gs, tb=tb, tq=tq, length=length)

    def cur(which):
        return pl.BlockSpec((None, gs, tb, A_WIDTH), lambda b, i: (which, b, i, 0))

    def prev(which):
        return pl.BlockSpec((None, gs, BAND, A_WIDTH), lambda b, i: (which, b, jnp.maximum(i * r - 1, 0), 0))

    def nxt(which):
        return pl.BlockSpec((None, gs, BAND, A_WIDTH), lambda b, i: (which, b, jnp.minimum((i + 1) * r, nb - 1), 0))

    return pl.pallas_call(
        kern,
        grid=(g // gs, length // tb),
        in_specs=[cur(0), prev(1), cur(1), nxt(1), prev(2), cur(2), nxt(2)],
        out_specs=(pl.BlockSpec((gs, tb, A_WIDTH), lambda b, i: (b, i, 0)),
                   pl.BlockSpec((gs, tb, A_WIDTH), lambda b, i: (b, i, 0))),
        out_shape=(jax.ShapeDtypeStruct((g, length, A_WIDTH), BF16),
                   jax.ShapeDtypeStruct((g, length, A_WIDTH), F32)),
        compiler_params=_cparams(("parallel", "parallel")),
        name="dilated_attn",
    )(zd, zd, zd, zd, zd, zd, zd)


def _even_out_kernel(o1_ref, l1_ref, o4_ref, l4_ref, o16_ref, l16_ref, bg_ref, cg_ref, h_ref,
                     cgp_ref, hp_ref, cgn_ref, hn_ref, cw_ref, wa_ref, wb_ref, x_ref, out_ref,
                     s4o, s4l, s16o, s16l, *, tm, nts):
    it = pl.program_id(0) % nts
    nslab = A_WIDTH // LANES
    for c in range(nslab):
        sl = slice(c * LANES, (c + 1) * LANES)
        for r in range(4):
            s4o[c, pl.ds(r, tm // 4, stride=4), :] = o4_ref[r, :, sl].astype(F32)
            s4l[c, pl.ds(r, tm // 4, stride=4), :] = l4_ref[r, :, sl]
        for r in range(16):
            s16o[c, pl.ds(r, tm // 16, stride=16), :] = o16_ref[r, :, sl].astype(F32)
            s16l[c, pl.ds(r, tm // 16, stride=16), :] = l16_ref[r, :, sl]
    wide = lambda ref: jnp.concatenate([ref[c] for c in range(nslab)], axis=1)
    l1, l4, l16 = l1_ref[...], wide(s4l), wide(s16l)
    mx = jnp.maximum(jnp.maximum(l1, l4), l16)
    e1, e4, e16 = jnp.exp(l1 - mx), jnp.exp(l4 - mx), jnp.exp(l16 - mx)
    ya = (e1 * o1_ref[...].astype(F32) + e4 * wide(s4o) + e16 * wide(s16o)) * (1.0 / (e1 + e4 + e16))

    u = cg_ref[...].astype(F32) * h_ref[...].astype(F32)
    last = cgp_ref.shape[0] - 1
    u_before = (cgp_ref[...].astype(F32) * hp_ref[...].astype(F32))[last:last + 1]
    u_after = (cgn_ref[...].astype(F32) * hn_ref[...].astype(F32))[0:1]
    u_before = u_before * (it > 0).astype(F32)
    u_after = u_after * (it < nts - 1).astype(F32)
    row = lax.broadcasted_iota(jnp.int32, u.shape, 0)
    u_prev = jnp.where(row == 0, u_before, pltpu.roll(u, 1, 0))
    u_next = jnp.where(row == tm - 1, u_after, pltpu.roll(u, tm - 1, 0))
    cw = cw_ref[...]
    yb = bg_ref[...].astype(F32) * (cw[0:1] * u_prev + cw[1:2] * u + cw[2:3] * u_next)

    out_ref[...] = x_ref[...] + _dot(ya.astype(BF16), wa_ref[...]) + _dot(yb.astype(BF16), wb_ref[...])


def _even_tail_kernel(*refs, tm, nts, nsub):
    mixer_in, cross_in, out_ref, scratch = refs[:17], refs[17:22], refs[22], refs[23:]
    x_mid = scratch[4]
    _even_out_kernel(*mixer_in, x_mid, *scratch[:4], tm=tm, nts=nts)
    _cross_kernel(x_mid, *cross_in, out_ref, nsub=nsub)


def _cross_specs(seq, n_mem, tm):
    nts = seq // tm
    return [_full((1, D_MODEL)),
            pl.BlockSpec((n_mem, D_MODEL), lambda i: (i // nts, 0)),
            pl.BlockSpec((n_mem, D_MODEL), lambda i: (i // nts, 1)),
            _full((D_MODEL, D_MODEL)),
            _full((D_MODEL, D_MODEL))]


def _even_tail(x, o1, l1, o4, l4, o16, l16, zc, conv_w, w_out, cross, batch, seq):
    g_cross, kv, wq, wo, n_mem = cross
    m = x.shape[0]
    tm = TM
    nts = seq // tm
    hb = 16
    nhb = m // hb
    kern = functools.partial(_even_tail_kernel, tm=tm, nts=nts, nsub=2)
    nat = lambda: pl.BlockSpec((tm, A_WIDTH), lambda i: (i, 0))
    dil = lambda d: pl.BlockSpec((None, d, tm // d, A_WIDTH), lambda i: (i // nts, 0, i % nts, 0))
    col = lambda c: pl.BlockSpec((tm, B_WIDTH), lambda i: (i, c))
    before = lambda c: pl.BlockSpec((hb, B_WIDTH), lambda i: (jnp.maximum(i * (tm // hb) - 1, 0), c))
    after = lambda c: pl.BlockSpec((hb, B_WIDTH), lambda i: (jnp.minimum((i + 1) * (tm // hb), nhb - 1), c))
    return pl.pallas_call(
        kern,
        grid=(m // tm,),
        in_specs=[nat(), nat(), dil(4), dil(4), dil(16), dil(16),
                  col(0), col(1), col(2), before(1), before(2), after(1), after(2),
                  _full((3, B_WIDTH)),
                  pl.BlockSpec((A_WIDTH, D_MODEL), lambda i: (0, 0)),
                  pl.BlockSpec((B_WIDTH, D_MODEL), lambda i: (1, 0)),
                  pl.BlockSpec((tm, D_MODEL), lambda i: (i, 0))] + _cross_specs(seq, n_mem, tm),
        out_specs=pl.BlockSpec((tm, D_MODEL), lambda i: (i, 0)),
        out_shape=jax.ShapeDtypeStruct((m, D_MODEL), F32),
        scratch_shapes=[pltpu.VMEM((A_WIDTH // LANES, tm, LANES), F32)] * 4 + [pltpu.VMEM((tm, D_MODEL), F32)],
        compiler_params=_cparams(("parallel",)),
        name="even_tail",
    )(o1, l1, o4, l4, o16, l16, zc, zc, zc, zc, zc, zc, zc, conv_w, w_out, w_out, x,
      g_cross.reshape(1, D_MODEL), kv, kv, wq, wo)


def _cross_kernel(x_ref, g_ref, k_ref, v_ref, wq_ref, wo_ref, out_ref, *, nsub):
    rows = x_ref.shape[0] // nsub
    g = g_ref[...]
    heads = [slice(h * CA_HEAD_DIM, (h + 1) * CA_HEAD_DIM) for h in range(CA_HEADS)]
    k = k_ref[...]
    v = v_ref[...]
    qs = []
    for i in range(nsub):
        xn = _rms(x_ref[i * rows:(i + 1) * rows, :], g).astype(BF16)
        qs.append((_dot(xn, wq_ref[...]) * (CA_HEAD_DIM ** -0.5)).astype(BF16))
    units = [(i, h) for i in range(nsub) for h in range(CA_HEADS)]
    scores, probs, outs = {}, {}, {}
    for step in range(len(units) + 2):
        if step < len(units):
            i, h = units[step]
            scores[step] = _dot_nt(qs[i][:, heads[h]], k[:, heads[h]])
        if 0 <= step - 1 < len(units):
            s = scores.pop(step - 1)
            e = jnp.exp(s - jnp.max(s, axis=-1, keepdims=True))
            probs[step - 1] = (e * (1.0 / jnp.sum(e, axis=-1, keepdims=True))).astype(BF16)
        if 0 <= step - 2 < len(units):
            i, h = units[step - 2]
            outs[i, h] = _dot(probs.pop(step - 2), v[:, heads[h]]).astype(BF16)
            if h == CA_HEADS - 1:
                o = jnp.concatenate([outs.pop((i, hh)) for hh in range(CA_HEADS)], axis=1)
                out_ref[i * rows:(i + 1) * rows, :] = x_ref[i * rows:(i + 1) * rows, :] + _dot(o, wo_ref[...])


def _ffn_kernel(x_ref, g_ref, wgu_ref, wd_ref, gf_ref, out_ref, *, final_norm, tf, nsub):
    rows = x_ref.shape[0] // nsub
    g = g_ref[...]
    accs = [x_ref[i * rows:(i + 1) * rows, :] for i in range(nsub)]
    xns = [_rms(x, g).astype(BF16) for x in accs]
    nf = D_FF // tf
    acts = None
    for f in range(nf + 1):
        new_acts = []
        if f < nf:
            for i in range(nsub):
                gate = _dot(xns[i], wgu_ref[:, f * tf:(f + 1) * tf])
                up = _dot(xns[i], wgu_ref[:, D_FF + f * tf:D_FF + (f + 1) * tf])
                new_acts.append((gate * _sigmoid(gate) * up).astype(BF16))
        if f > 0:
            for i in range(nsub):
                accs[i] = accs[i] + _dot(acts[i], wd_ref[(f - 1) * tf:f * tf, :])
        acts = new_acts
    for i in range(nsub):
        y = _rms(accs[i], gf_ref[...]) if final_norm else accs[i]
        out_ref[i * rows:(i + 1) * rows, :] = y


def _ffn(x, g, w_gu, w_down, g_final, final_norm):
    m = x.shape[0]
    tm = TM
    kern = functools.partial(_ffn_kernel, final_norm=final_norm, tf=TF_FFN, nsub=2)
    resident = lambda shape: pl.BlockSpec(shape, lambda i: (0, 0), pipeline_mode=pl.Buffered(1))
    return pl.pallas_call(
        kern,
        grid=(m // tm,),
        in_specs=[pl.BlockSpec((tm, D_MODEL), lambda i: (i, 0)),
                  _full((1, D_MODEL)),
                  resident(w_gu.shape),
                  resident(w_down.shape),
                  _full((1, D_MODEL))],
        out_specs=pl.BlockSpec((tm, D_MODEL), lambda i: (i, 0)),
        out_shape=jax.ShapeDtypeStruct((m, D_MODEL), F32),
        compiler_params=_cparams(("parallel",)),
        name="swiglu",
    )(x, g.reshape(1, D_MODEL), w_gu, w_down, g_final.reshape(1, D_MODEL))


def _head_sum(x, e_ref):
    x = x.astype(BF16)
    e = e_ref[...]
    width = e.shape[0]
    return jnp.concatenate([_dot(x[:, c * width:(c + 1) * width], e) for c in range(x.shape[1] // width)],
                           axis=1)


def _sigmoid(x):
    return 0.5 * jnp.tanh(0.5 * x) + 0.5


def _store_pairs(ref, val, *lead):
    for p in range(val.shape[1] // LANES):
        ref[(*lead, p)] = val[:, p * LANES:(p + 1) * LANES].astype(ref.dtype)


def _rwkv_pre_kernel(x_ref, xb_ref, xa_ref, g_ref, mu_ref, wr_ref, wk_ref, wv_ref, w1_ref, w2_ref, w0_ref,
                     a1_ref, a2_ref, a0_ref, g1_ref, g2_ref, kk_ref, ka_ref, rk_ref, e_ref,
                     sm_ref, r_out, v_out, kn_out, kd_out, ba_out, lw_out, gate_out, bonus_out, *, nts):
    it = pl.program_id(0) % nts
    g = g_ref[...]
    xn_bf = _rms(x_ref[...], g).astype(BF16)
    before = (_rms(xb_ref[...], g) * (it > 0).astype(F32)).astype(BF16)
    after = (_rms(xa_ref[...], g) * (it < nts - 1).astype(F32)).astype(BF16)
    xx_bf = _dot(sm_ref[...], jnp.concatenate([before, xn_bf, after], axis=0)).astype(BF16)
    mu = mu_ref[...].astype(BF16)
    mix = lambda i: xn_bf + xx_bf * mu[i:i + 1]

    tw = jnp.tanh(_dot(mix(1), w1_ref[...])).astype(BF16)
    ta = _dot(mix(4), a1_ref[...]).astype(BF16)
    tg = _sigmoid(_dot(mix(5), g1_ref[...])).astype(BF16)
    k_lin = _dot(mix(2), wk_ref[...])
    wls = [w0_ref[d:d + 1] + _dot(tw, w2_ref[d]) for d in range(2)]
    r = _dot(mix(0), wr_ref[...])
    for d in range(2):
        _store_pairs(lw_out, (-0.5 * DECAY_SCALE) * jnp.tanh(0.5 * wls[d]) - 0.5 * DECAY_SCALE, d)
    kn = k_lin * kk_ref[...]
    kn = kn * lax.rsqrt(jnp.maximum(_head_sum(kn * kn, e_ref), 1e-24))
    _store_pairs(kn_out, kn)
    als = [a0_ref[d:d + 1] + _dot(ta, a2_ref[d]) for d in range(2)]
    v = _dot(mix(3), wv_ref[...])
    hk = k_lin * (0.5 * ka_ref[...])
    k_base = k_lin - hk
    hn = 0.5 * kn
    kd_sum = jnp.zeros_like(k_lin)
    for d in range(2):
        t = jnp.tanh(0.5 * als[d])
        kd = k_base + hk * t
        kd_sum = kd_sum + kd
        _store_pairs(kd_out, kd, d)
        _store_pairs(ba_out, hn + hn * t, d)
    gate_out[...] = _dot(tg, g2_ref[...]).astype(BF16)
    _store_pairs(r_out, r)
    _store_pairs(v_out, v)
    bonus_out[...] = (_head_sum(r * kd_sum * rk_ref[...], e_ref) * v).astype(BF16)


def _rwkv_pre(x, g, p, batch, seq):
    m = x.shape[0]
    tm = TM_RW
    nts = seq // tm
    hb = 16
    nhb = m // hb
    kern = functools.partial(_rwkv_pre_kernel, nts=nts)
    row = lambda: pl.BlockSpec((tm, D_MODEL), lambda i: (i, 0))
    npair = D_MODEL // LANES
    pairs = lambda: pl.BlockSpec((npair, tm, LANES), lambda i: (0, i, 0))
    pairs2 = lambda: pl.BlockSpec((2, npair, tm, LANES), lambda i: (0, 0, i, 0))
    t = jnp.arange(tm)[:, None]
    c = jnp.arange(tm + 2 * hb)[None, :] - hb
    shift = (jnp.where(c == t, -1.0, 0.0) + jnp.where(jnp.abs(c - t) == 1, 0.5, 0.0)).astype(BF16)
    args = (x, x, x, g.reshape(1, D_MODEL), p['mu'], p['wr'], p['wk'], p['wv'], p['w1'], p['w2'], p['w0'],
            p['a1'], p['a2'], p['a0'], p['g1'], p['g2'], p['kk'], p['ka'], p['rk'], p['e'], shift)
    in_specs = [row(),
                pl.BlockSpec((hb, D_MODEL), lambda i: (jnp.maximum(i * (tm // hb) - 1, 0), 0)),
                pl.BlockSpec((hb, D_MODEL), lambda i: (jnp.minimum((i + 1) * (tm // hb), nhb - 1), 0))]
    in_specs += [_full(a.shape) for a in args[3:]]
    out_shape = (jax.ShapeDtypeStruct((npair, m, LANES), BF16),) * 3 + (
        jax.ShapeDtypeStruct((2, npair, m, LANES), BF16), jax.ShapeDtypeStruct((2, npair, m, LANES), BF16),
        jax.ShapeDtypeStruct((2, npair, m, LANES), F32),
        jax.ShapeDtypeStruct((m, D_MODEL), BF16), jax.ShapeDtypeStruct((m, D_MODEL), BF16))
    out_specs = (pairs(), pairs(), pairs(), pairs2(), pairs2(), pairs2(), row(), row())
    return pl.pallas_call(
        kern, grid=(m // tm,), in_specs=in_specs, out_specs=out_specs, out_shape=out_shape,
        compiler_params=_cparams(("parallel",)), name="rwkv_pre",
    )(*args)


def _bd(x, first):
    z = jnp.zeros_like(x)
    return jnp.concatenate([jnp.where(first, x, z), jnp.where(first, z, x)], axis=0)


def _compact(full, first):
    return jnp.where(first, full[:CHUNK], full[CHUNK:])


def _wkv_masks(sgn):
    c = CHUNK
    row = lax.broadcasted_iota(jnp.int32, (c, c), 0)
    col = lax.broadcasted_iota(jnp.int32, (c, c), 1)
    tri = jnp.where((row - col) * sgn >= 0, 1.0, 0.0).astype(BF16)
    prow = lax.broadcasted_iota(jnp.int32, (c, LANES), 0)
    pcol = lax.broadcasted_iota(jnp.int32, (c, LANES), 1) & (c - 1)
    dlt = (prow - pcol) * sgn
    strict = dlt > 0
    same = lambda sh: (prow >> sh) == (pcol >> sh)
    levels = []
    sh = 2
    while (1 << sh) < c:
        levels.append(jnp.where(strict & same(sh + 1) & jnp.logical_not(same(sh)), 1.0, 0.0))
        sh += 1
    in4 = same(2)
    return dict(tri=tri, strict=strict, incl=dlt >= 0, eye=jnp.where(prow == pcol, 1.0, 0.0), levels=levels,
                tri4=jnp.where(strict & in4, 1.0, 0.0),
                off2=jnp.where(in4 & (dlt == 2), 1.0, 0.0), off3=jnp.where(in4 & (dlt == 3), 1.0, 0.0),
                row1=sgn % c, row2=(2 * sgn) % c, lane1=(-sgn) % LANES, lane2=(-2 * sgn) % LANES)


def _inverse4(a_ab, mk):
    a = a_ab * mk['tri4']
    up1 = pltpu.roll(a, mk['row1'], 0)
    up2 = pltpu.roll(a, mk['row2'], 0)
    left1 = pltpu.roll(a, mk['lane1'], 1)
    left2 = pltpu.roll(a, mk['lane2'], 1)
    sq = left1 * up1 * mk['off2'] + (left1 * up2 + left2 * up1) * mk['off3']
    cube = pltpu.roll(sq, mk['lane1'], 1) * up2 * mk['off3']
    return mk['eye'] + a + sq + cube


def _wkv_local(chains, first):
    c = CHUNK
    for s in chains:
        hl = _dot(s['mk']['tri'], jnp.concatenate(_split(s['lw']), axis=1))
        s['cum'] = hl[:, :LANES] + hl[:, LANES:]
    yield
    for s in chains:
        cum, lw = s['cum'], s['lw']
        total = jnp.sum(lw, axis=0, keepdims=True)
        p_inv = jnp.exp(-cum)
        p_hat = jnp.exp(total - cum)
        s['at'] = (-s['kn'] * jnp.exp(cum - lw)).astype(BF16)
        s['rt'] = (s['r'] * jnp.exp(cum)).astype(BF16)
        s['bt'] = (s['ba'] * p_inv).astype(BF16)
        s['kt'] = (s['kd'] * p_inv).astype(BF16)
        s['bh'] = (s['ba'] * p_hat).astype(BF16)
        s['kh'] = (s['kd'] * p_hat).astype(BF16)
        s['v_bf'] = s['v'].astype(BF16)
        s['p_end'] = jnp.exp(total)
        s['lhs'] = jnp.concatenate([s['at'], s['rt']], axis=0)
    yield
    for s in chains:
        s['sbk'] = _dot_nt(s['lhs'], jnp.concatenate([_bd(s['bt'], first), _bd(s['kt'], first)], axis=0))
    yield
    for s in chains:
        mk = s['mk']
        sb, sk = s['sbk'][:, :LANES], s['sbk'][:, LANES:]
        s['a_ab'] = jnp.where(mk['strict'], sb[:c], 0.0)
        s['a_rb'] = jnp.where(mk['incl'], sb[c:], 0.0).astype(BF16)
        s['a_kk'] = jnp.concatenate([jnp.where(mk['strict'], sk[:c], 0.0),
                                     jnp.where(mk['incl'], sk[c:], 0.0)], axis=0).astype(BF16)
        s['inv'] = _inverse4(s['a_ab'], mk)
    yield
    for s in chains:
        s['av'] = _dot(s['a_kk'], _bd(s['v_bf'], first))
        s['y0'] = _compact(_dot_tn(s['v_bf'], s['kh']), first)
    yield
    for lvl in range(len(chains[0]['mk']['levels'])):
        for s in chains:
            s['inv_bf'] = s['inv'].astype(BF16)
            s['inner'] = _dot((s['a_ab'] * s['mk']['levels'][lvl]).astype(BF16), _bd(s['inv_bf'], first))
        yield
        for s in chains:
            s['inv'] = s['inv'] + _dot(s['inv_bf'], _bd(s['inner'].astype(BF16), first))
        yield
    for s in chains:
        rhs = jnp.concatenate([_bd(s['at'], first), _bd(s['av'][:c].astype(BF16), first)], axis=1)
        wu = _dot(s['inv'].astype(BF16), rhs)
        s['w1'] = wu[:, :LANES].astype(BF16)
        s['u0'] = wu[:, LANES:]
        s['o0'] = s['av'][c:]


def _wkv_seq(rows, ys, out_refs, first):
    c = CHUNK
    for row in rows:
        wrs = [_dot_nt(jnp.concatenate([s['w1'], s['rt']], axis=0), _bd(y.astype(BF16), first))
               for s, y in zip(row, ys)]
        yield
        us = [(wr[:c] + s['u0']).astype(BF16) for s, wr in zip(row, wrs)]
        for s, wr, u, (o_ref, p) in zip(row, wrs, us, out_refs):
            o_ref[p, s['sl'], :] = (wr[c:] + _dot(s['a_rb'], _bd(u, first)) + s['o0']).astype(o_ref.dtype)
        ys[:] = [s['p_end'] * y + _compact(_dot_tn(u, s['bh']), first) + s['y0'] for s, y, u in zip(row, ys, us)]
        yield


def _run_staggered(local_gens, make_seq, period):
    end = object()
    started, finished, tick = 0, 0, 0
    active, updates = [], []
    while finished < len(local_gens) or updates:
        if started < len(local_gens) and tick >= started * period:
            active.append(started)
            started += 1
        for g in list(active):
            if next(local_gens[g], end) is end:
                active.remove(g)
                finished += 1
                updates.append(make_seq(g))
        if updates and next(updates[0], end) is end:
            updates.pop(0)
        tick += 1


def _wkv_kernel(rf, vf, nf, kdf, baf, lwf, rb, vb, nb, kdb, bab, lwb, of_ref, ob_ref, state, *, ts, npairs):
    first = lax.broadcasted_iota(jnp.int32, (CHUNK, LANES), 1) < RWKV_HEAD

    @pl.when(pl.program_id(2) == 0)
    def _():
        state[...] = jnp.zeros_like(state)

    nchunk = ts // CHUNK
    names = ('r', 'v', 'kn', 'kd', 'ba', 'lw')
    dirs = ((_wkv_masks(1), (rf, vf, nf, kdf, baf, lwf), of_ref, list(range(nchunk))),
            (_wkv_masks(-1), (rb, vb, nb, kdb, bab, lwb), ob_ref, list(range(nchunk - 1, -1, -1))))
    scans = [(d, p) for d in range(2) for p in range(npairs)]
    out_refs = [(dirs[d][2], p) for d, p in scans]
    steps = []
    for k in range(nchunk):
        row = []
        for d, p in scans:
            mk, ins, _, order = dirs[d]
            sl = pl.ds(order[k] * CHUNK, CHUNK)
            chain = {n: ref[p, sl, :].astype(F32) for n, ref in zip(names, ins)}
            chain.update(mk=mk, sl=sl)
            row.append(chain)
        steps.append(row)
    ys = [state[d, p] for d, p in scans]
    groups = [steps[k:k + WKV_GROUP] for k in range(0, nchunk, WKV_GROUP)]
    _run_staggered([_wkv_local([s for row in rows for s in row], first) for rows in groups],
                   lambda g: _wkv_seq(groups[g], ys, out_refs, first), WKV_STAGGER)
    for (d, p), y in zip(scans, ys):
        state[d, p] = y


def _wkv(r, v, kn, kd, ba, lw, batch, seq):
    ts, npairs = TS_WKV, PAIRS_WKV
    nt = seq // ts
    npair = D_MODEL // LANES
    m = batch * seq
    fwd = lambda b, t: b * nt + t
    bwd = lambda b, t: b * nt + nt - 1 - t
    s3 = lambda at: pl.BlockSpec((npairs, ts, LANES), lambda b, g, t: (g, at(b, t), 0))
    s4 = lambda d, at: pl.BlockSpec((None, npairs, ts, LANES), lambda b, g, t: (d, g, at(b, t), 0))
    return pl.pallas_call(
        functools.partial(_wkv_kernel, ts=ts, npairs=npairs),
        grid=(batch, npair // npairs, nt),
        in_specs=[s3(fwd), s3(fwd), s3(fwd), s4(0, fwd), s4(0, fwd), s4(0, fwd),
                  s3(bwd), s3(bwd), s3(bwd), s4(1, bwd), s4(1, bwd), s4(1, bwd)],
        out_specs=(s3(fwd), s3(bwd)),
        out_shape=(jax.ShapeDtypeStruct((npair, m, LANES), BF16),) * 2,
        scratch_shapes=[pltpu.VMEM((2, npairs, CHUNK, LANES), F32)],
        compiler_params=_cparams(("parallel", "parallel", "arbitrary")),
        name="wkv",
    )(r, v, kn, kd, ba, lw, r, v, kn, kd, ba, lw)


def _rwkv_post_kernel(of_ref, ob_ref, bonus_ref, gate_ref, lw_ref, lb_ref, e_ref, wo_ref, x_ref, out_ref):
    y = jnp.concatenate([of_ref[p].astype(F32) + ob_ref[p].astype(F32) for p in range(of_ref.shape[0])], axis=1)
    mean = _head_sum(y, e_ref) * (1.0 / RWKV_HEAD)
    yc = y - mean
    var = _head_sum(yc * yc, e_ref) * (1.0 / RWKV_HEAD)
    yn = yc * lax.rsqrt(var + GN_EPS)
    y2 = yn * lw_ref[...] + lb_ref[...] + bonus_ref[...].astype(F32)
    out_ref[...] = x_ref[...] + _dot((y2 * gate_ref[...].astype(F32)).astype(BF16), wo_ref[...])


def _odd_tail_kernel(*refs, nsub):
    mixer_in, cross_in, out_ref, x_mid = refs[:9], refs[9:14], refs[14], refs[15]
    _rwkv_post_kernel(*mixer_in, x_mid)
    _cross_kernel(x_mid, *cross_in, out_ref, nsub=nsub)


def _odd_tail(x, o_f, o_b, bonus, gate, p, cross, seq):
    g_cross, kv, wq, wo, n_mem = cross
    m = x.shape[0]
    tm = TM
    row = lambda: pl.BlockSpec((tm, D_MODEL), lambda i: (i, 0))
    pairs = lambda: pl.BlockSpec((D_MODEL // LANES, tm, LANES), lambda i: (0, i, 0))
    return pl.pallas_call(
        functools.partial(_odd_tail_kernel, nsub=2),
        grid=(m // tm,),
        in_specs=[pairs(), pairs(), row(), row(), _full((1, D_MODEL)), _full((1, D_MODEL)),
                  _full(p['e'].shape), _full((D_MODEL, D_MODEL)), row()] + _cross_specs(seq, n_mem, tm),
        out_specs=row(),
        out_shape=jax.ShapeDtypeStruct((m, D_MODEL), F32),
        scratch_shapes=[pltpu.VMEM((tm, D_MODEL), F32)],
        compiler_params=_cparams(("parallel",)),
        name="odd_tail",
    )(o_f, o_b, bonus, gate, p['lnx_w'], p['lnx_b'], p['e'], p['wo'], x,
      g_cross.reshape(1, D_MODEL), kv, kv, wq, wo)


def _prep_odd(o, rw_mu, rw_wr, rw_wk, rw_wv, rw_wo, rw_w0, rw_w1, rw_w2, rw_a0, rw_a1, rw_a2, rw_g1, rw_g2,
              rw_kk, rw_ka, rw_rk, rw_lnx_w, rw_lnx_b):
    bf = lambda a: a.astype(BF16)

    def pad_dir(w):
        z = jnp.zeros_like(w[0])
        return jnp.stack([jnp.concatenate([w[0], z], axis=0), jnp.concatenate([z, w[1]], axis=0)])

    lane = jnp.arange(MXU_WIDTH) // RWKV_HEAD
    return dict(
        mu=rw_mu[o], wr=bf(rw_wr[o]), wk=bf(rw_wk[o]), wv=bf(rw_wv[o]), wo=bf(rw_wo[o]),
        w0=rw_w0[o], w1=bf(jnp.concatenate([rw_w1[o, 0], rw_w1[o, 1]], axis=1)), w2=bf(pad_dir(rw_w2[o])),
        a0=rw_a0[o], a1=bf(jnp.concatenate([rw_a1[o, 0], rw_a1[o, 1]], axis=1)), a2=bf(pad_dir(rw_a2[o])),
        g1=bf(rw_g1[o]), g2=bf(rw_g2[o]),
        kk=rw_kk[o].reshape(1, D_MODEL), ka=rw_ka[o].reshape(1, D_MODEL), rk=rw_rk[o].reshape(1, D_MODEL),
        lnx_w=rw_lnx_w[o].reshape(1, D_MODEL), lnx_b=rw_lnx_b[o].reshape(1, D_MODEL),
        e=(lane[:, None] == lane[None, :]).astype(BF16))


def _even_layer(x, g, w_in, w_out, conv_w, cos, sin, cross, batch, seq):
    z1, z4, z16, zc = _even_proj(x, g, w_in, cos, sin, batch, seq)
    m = x.shape[0]
    o1, l1 = _dilated_branch(z1.reshape(3, batch, seq, A_WIDTH), seq)
    o4, l4 = _dilated_branch(z4.reshape(3, batch * 4, seq // 4, A_WIDTH), seq // 4)
    o16, l16 = _dilated_branch(z16.reshape(3, batch * 16, seq // 16, A_WIDTH), seq // 16)
    o1, l1 = o1.reshape(m, A_WIDTH), l1.reshape(m, A_WIDTH)
    d4 = lambda a: a.reshape(batch, 4, seq // 4, A_WIDTH)
    d16 = lambda a: a.reshape(batch, 16, seq // 16, A_WIDTH)
    return _even_tail(x, o1, l1, d4(o4), d4(l4), d16(o16), d16(l16), zc, conv_w, w_out, cross, batch, seq)


def _odd_layer(x, g, p, cross, batch, seq):
    r, v, kn, kd, ba, lw, gate, bonus = _rwkv_pre(x, g, p, batch, seq)
    o_f, o_b = _wkv(r, v, kn, kd, ba, lw, batch, seq)
    return _odd_tail(x, o_f, o_b, bonus, gate, p, cross, seq)


def _trunk(x, mem, w):
    batch, seq, _ = x.shape
    n_mem = mem.shape[1]
    x = x.reshape(batch * seq, D_MODEL)
    mem = mem.reshape(batch * n_mem, D_MODEL)
    cos, sin = _rope_tables(seq)
    for l in range(DEPTH):
        kv = _norm_matmul(mem, w['norm_mem'][l], w['ca_wkv'][l], TM, 2 * D_MODEL, BF16)
        cross = (w['norm_cross'][l], kv, w['ca_wq'][l], w['ca_wo'][l], n_mem)
        if l % 2 == 0:
            e = l // 2
            x = _even_layer(x, w['norm_mix'][l], w['ab_w_in'][e], w['ab_w_out'][e], w['ab_conv'][e],
                            cos, sin, cross, batch, seq)
        else:
            x = _odd_layer(x, w['norm_mix'][l], w['odd'][l // 2], cross, batch, seq)
        x = _ffn(x, w['norm_ffn'][l], w['ffn_wgu'][l], w['ffn_wdown'][l], w['norm_final'], l == DEPTH - 1)
    return x.reshape(batch, seq, D_MODEL)


def kernel(x_prompt, x_sample, mem_prompt, mem_sample, norm_mix, norm_cross, norm_mem, norm_ffn, norm_final,
           ab_w_in, ab_w_out, ab_conv, rw_mu, rw_wr, rw_wk, rw_wv, rw_wo, rw_w0, rw_w1, rw_w2, rw_a0, rw_a1,
           rw_a2, rw_g1, rw_g2, rw_kk, rw_ka, rw_rk, rw_lnx_w, rw_lnx_b, ca_wq, ca_wkv, ca_wo, ffn_wgu,
           ffn_wdown):
    bf = lambda a: a.astype(BF16)
    w = dict(norm_mix=norm_mix, norm_cross=norm_cross, norm_mem=norm_mem, norm_ffn=norm_ffn,
             norm_final=norm_final, ab_w_in=bf(ab_w_in), ab_w_out=bf(ab_w_out), ab_conv=ab_conv,
             ca_wq=bf(ca_wq), ca_wkv=bf(ca_wkv), ca_wo=bf(ca_wo), ffn_wgu=bf(ffn_wgu), ffn_wdown=bf(ffn_wdown),
             odd=[_prep_odd(o, rw_mu, rw_wr, rw_wk, rw_wv, rw_wo, rw_w0, rw_w1, rw_w2, rw_a0, rw_a1, rw_a2,
                            rw_g1, rw_g2, rw_kk, rw_ka, rw_rk, rw_lnx_w, rw_lnx_b)
                  for o in range(rw_mu.shape[0])])
    return _trunk(x_prompt, mem_prompt, w), _trunk(x_sample, mem_sample, w)
```

```python
import functools

import jax
import jax.numpy as jnp
from jax import lax
from jax.experimental import pallas as pl
from jax.experimental.pallas import tpu as pltpu

F32 = jnp.float32
BF16 = jnp.bfloat16

D_MODEL = 1024
DEPTH = 4
HEAD_DIM = 64
A_WIDTH = 512
B_WIDTH = 512
BAND = 64
ROPE_THETA = 500000.0
ROPE_DIM = 16
RWKV_HEAD = 64
GN_EPS = 64e-5
DECAY_SCALE = 0.6065306597126334
CA_HEADS = 4
CA_HEAD_DIM = 256
D_FF = 2816
RMS_EPS = 1e-6
NEG_INF = -1e30
CHUNK = 64
LANES = 128
MXU_WIDTH = 256

VMEM_LIMIT = 48 * 1024 * 1024

TM = 512
TM_RW = 256
TB_DIL = 1024
TQ_DIL = 128
DIL_LAG = 1
TF_FFN = 256
TS_WKV = 512
PAIRS_WKV = 4
WKV_GROUP = 2
WKV_STAGGER = 10


def _cparams(sem):
    return pltpu.CompilerParams(dimension_semantics=sem, vmem_limit_bytes=VMEM_LIMIT)


def _full(shape):
    n = len(shape)
    return pl.BlockSpec(shape, lambda *_: (0,) * n)


def _rms(x, g):
    ms = jnp.mean(x * x, axis=-1, keepdims=True)
    return x * lax.rsqrt(ms + RMS_EPS) * g


def _dot(a, b):
    return jnp.dot(a, b, preferred_element_type=F32)


def _dot_nt(a, b):
    return lax.dot_general(a, b, (((1,), (1,)), ((), ())), preferred_element_type=F32)


def _dot_tn(a, b):
    return lax.dot_general(a, b, (((0,), (0,)), ((), ())), preferred_element_type=F32)


def _split(x):
    hi = x.astype(BF16)
    lo = (x - hi.astype(F32)).astype(BF16)
    return hi, lo


def _norm_matmul_kernel(x_ref, g_ref, w_ref, o_ref, xn_ref):
    @pl.when(pl.program_id(1) == 0)
    def _():
        xn_ref[...] = _rms(x_ref[...], g_ref[...]).astype(BF16)

    o_ref[...] = _dot(xn_ref[...], w_ref[...]).astype(o_ref.dtype)


def _norm_matmul(x, g, w, tm, tn, out_dtype):
    m, k = x.shape
    n = w.shape[1]
    return pl.pallas_call(
        _norm_matmul_kernel,
        grid=(m // tm, n // tn),
        in_specs=[pl.BlockSpec((tm, k), lambda i, j: (i, 0)),
                  pl.BlockSpec((1, k), lambda i, j: (0, 0)),
                  pl.BlockSpec((k, tn), lambda i, j: (0, j))],
        out_specs=pl.BlockSpec((tm, tn), lambda i, j: (i, j)),
        out_shape=jax.ShapeDtypeStruct((m, n), out_dtype),
        scratch_shapes=[pltpu.VMEM((tm, k), BF16)],
        compiler_params=_cparams(("parallel", "arbitrary")),
        name="norm_matmul",
    )(x, g.reshape(1, k), w)


def _even_proj_kernel(x_ref, g_ref, w_ref, cos_ref, sin_ref, p4_ref, p16_ref,
                      o1_ref, o4_ref, o16_ref, oc_ref, *, tm):
    xn = _rms(x_ref[...], g_ref[...]).astype(BF16)
    reps = A_WIDTH // LANES
    cos = jnp.concatenate([cos_ref[...]] * reps, axis=1)
    sin = jnp.concatenate([sin_ref[...]] * reps, axis=1)
    lane = lax.broadcasted_iota(jnp.int32, (tm, A_WIDTH), 1) & (HEAD_DIM - 1)
    half = ROPE_DIM // 2
    nparts = 3
    acc = None
    for j in range(nparts + 1):
        nxt = _dot(xn, w_ref[:, j * A_WIDTH:(j + 1) * A_WIDTH]) if j < nparts else None
        if j == nparts:
            oc_ref[...] = _dot(xn, w_ref[:, 3 * A_WIDTH:]).astype(BF16)
        if j > 0:
            a, part = acc, j - 1
            if part < 2:
                partner = jnp.where(lane < half, pltpu.roll(a, A_WIDTH - half, 1), pltpu.roll(a, half, 1))
                a = a * cos + partner * sin
                if part == 0:
                    a = a * (HEAD_DIM ** -0.5)
            a = a.astype(BF16)
            o1_ref[part] = a
            for d, p_ref, o_ref in ((4, p4_ref, o4_ref), (16, p16_ref, o16_ref)):
                perm = _dot(p_ref[...], a).astype(BF16)
                for r in range(d):
                    o_ref[part, r] = perm[r * (tm // d):(r + 1) * (tm // d)]
        acc = nxt


def _dedilate_perm(tm, d):
    i = jnp.arange(tm)
    src = (i % (tm // d)) * d + i // (tm // d)
    return (src[:, None] == jnp.arange(tm)[None, :]).astype(BF16)


def _even_proj(x, g, w_in, cos, sin, batch, seq):
    m = x.shape[0]
    tm = TM
    nts = seq // tm
    kern = functools.partial(_even_proj_kernel, tm=tm)
    out_shape = (jax.ShapeDtypeStruct((3, m, A_WIDTH), BF16),
                 jax.ShapeDtypeStruct((3, batch, 4, seq // 4, A_WIDTH), BF16),
                 jax.ShapeDtypeStruct((3, batch, 16, seq // 16, A_WIDTH), BF16),
                 jax.ShapeDtypeStruct((m, 3 * B_WIDTH), BF16))
    return pl.pallas_call(
        kern,
        grid=(m // tm,),
        in_specs=[pl.BlockSpec((tm, D_MODEL), lambda i: (i, 0)),
                  _full((1, D_MODEL)),
                  _full(w_in.shape),
                  pl.BlockSpec((tm, LANES), lambda i: (i % nts, 0)),
                  pl.BlockSpec((tm, LANES), lambda i: (i % nts, 0)),
                  _full((tm, tm)), _full((tm, tm))],
        out_specs=(pl.BlockSpec((3, tm, A_WIDTH), lambda i: (0, i, 0)),
                   pl.BlockSpec((3, None, 4, tm // 4, A_WIDTH), lambda i: (0, i // nts, 0, i % nts, 0)),
                   pl.BlockSpec((3, None, 16, tm // 16, A_WIDTH), lambda i: (0, i // nts, 0, i % nts, 0)),
                   pl.BlockSpec((tm, 3 * B_WIDTH), lambda i: (i, 0))),
        out_shape=out_shape,
        compiler_params=_cparams(("parallel",)),
        name="even_proj",
    )(x, g.reshape(1, D_MODEL), w_in, cos, sin, _dedilate_perm(tm, 4), _dedilate_perm(tm, 16))


def _rope_tables(seq):
    half = ROPE_DIM // 2
    inv = ROPE_THETA ** (-2.0 * jnp.arange(half, dtype=F32) / ROPE_DIM)
    ang = jnp.arange(seq, dtype=F32)[:, None] * inv[None, :]
    cos, sin = jnp.cos(ang), jnp.sin(ang)
    rest = HEAD_DIM - ROPE_DIM
    cos_h = jnp.concatenate([cos, cos, jnp.ones((seq, rest), F32)], axis=1)
    sin_h = jnp.concatenate([-sin, sin, jnp.zeros((seq, rest), F32)], axis=1)
    reps = LANES // HEAD_DIM
    return jnp.tile(cos_h, (1, reps)), jnp.tile(sin_h, (1, reps))


def _dil_kernel(q_ref, kp_ref, kc_ref, kn_ref, vp_ref, vc_ref, vn_ref, o_ref, lse_ref, *, gs, tb, tq, length):
    qi = pl.program_id(1)
    span = tq + 2 * BAND
    ks = [jnp.concatenate([kp_ref[g], kc_ref[g], kn_ref[g]], axis=0) for g in range(gs)]
    vs = [jnp.concatenate([vp_ref[g], vc_ref[g], vn_ref[g]], axis=0) for g in range(gs)]
    row = lax.broadcasted_iota(jnp.int32, (tq, span), 0)
    col = lax.broadcasted_iota(jnp.int32, (tq, span), 1)
    rel = col - BAND - row
    band = (rel <= BAND) & (rel >= -BAND)
    first = lax.broadcasted_iota(jnp.int32, (tq, LANES), 1) < HEAD_DIM
    units = [(g, s_idx, p, hh) for g in range(gs) for s_idx in range(tb // tq)
             for p in range(A_WIDTH // LANES) for hh in range(2)]
    lag = DIL_LAG
    scores, maxes, probs, results = {}, {}, {}, {}
    for n in range(len(units) + 3 * lag):
        if n < len(units):
            g, s_idx, p, hh = units[n]
            sl = slice(p * LANES, (p + 1) * LANES)
            qp = q_ref[g, s_idx * tq:(s_idx + 1) * tq, sl]
            sel = first if hh == 0 else jnp.logical_not(first)
            key_pos = qi * tb + s_idx * tq - BAND + col
            valid = band & (key_pos >= 0) & (key_pos < length)
            s = _dot_nt(jnp.where(sel, qp, jnp.zeros_like(qp)), ks[g][s_idx * tq:s_idx * tq + span, sl])
            scores[n] = jnp.where(valid, s, NEG_INF)
        m = n - lag
        if 0 <= m < len(units):
            maxes[m] = jnp.max(scores[m], axis=-1, keepdims=True)
        m = n - 2 * lag
        if 0 <= m < len(units):
            mx = maxes.pop(m)
            e = jnp.exp(scores.pop(m) - mx)
            den = jnp.sum(e, axis=-1, keepdims=True)
            probs[m] = (e.astype(BF16), den, mx)
        m = n - 3 * lag
        if 0 <= m < len(units):
            g, s_idx, p, hh = units[m]
            sl = slice(p * LANES, (p + 1) * LANES)
            e, den, mx = probs.pop(m)
            results[hh] = (_dot(e, vs[g][s_idx * tq:s_idx * tq + span, sl]) * (1.0 / den), mx + jnp.log(den))
            if hh == 1:
                rows = slice(s_idx * tq, (s_idx + 1) * tq)
                o_ref[g, rows, sl] = jnp.where(first, results[0][0], results[1][0]).astype(BF16)
                lse_ref[g, rows, sl] = jnp.where(first, results[0][1], results[1][1])


def _dilated_branch(zd, length):
    g = zd.shape[1]
    tb = min(TB_DIL, length)
    tq = min(TQ_DIL, length)
    gs = TB_DIL // tb
    nb = length // BAND
    r = tb // BAND
    kern = functools.partial(_dil_kernel, gs=gs, tb=tb, tq=tq, length=length)

    def cur(which):
        return pl.BlockSpec((None, gs, tb, A_WIDTH), lambda b, i: (which, b, i, 0))

    def prev(which):
        return pl.BlockSpec((None, gs, BAND, A_WIDTH), lambda b, i: (which, b, jnp.maximum(i * r - 1, 0), 0))

    def nxt(which):
        return pl.BlockSpec((None, gs, BAND, A_WIDTH), lambda b, i: (which, b, jnp.minimum((i + 1) * r, nb - 1), 0))

    return pl.pallas_call(
        kern,
        grid=(g // gs, length // tb),
        in_specs=[cur(0), prev(1), cur(1), nxt(1), prev(2), cur(2), nxt(2)],
        out_specs=(pl.BlockSpec((gs, tb, A_WIDTH), lambda b, i: (b, i, 0)),
                   pl.BlockSpec((gs, tb, A_WIDTH), lambda b, i: (b, i, 0))),
        out_shape=(jax.ShapeDtypeStruct((g, length, A_WIDTH), BF16),
                   jax.ShapeDtypeStruct((g, length, A_WIDTH), F32)),
        compiler_params=_cparams(("parallel", "parallel")),
        name="dilated_attn",
    )(zd, zd, zd, zd, zd, zd, zd)


def _even_out_kernel(o1_ref, l1_ref, o4_ref, l4_ref, o16_ref, l16_ref, bg_ref, cg_ref, h_ref,
                     cgp_ref, hp_ref, cgn_ref, hn_ref, cw_ref, wa_ref, wb_ref, x_ref, out_ref,
                     s4o, s4l, s16o, s16l, *, tm, nts):
    it = pl.program_id(0) % nts
    nslab = A_WIDTH // LANES
    for c in range(nslab):
        sl = slice(c * LANES, (c + 1) * LANES)
        for r in range(4):
            s4o[c, pl.ds(r, tm // 4, stride=4), :] = o4_ref[r, :, sl].astype(F32)
            s4l[c, pl.ds(r, tm // 4, stride=4), :] = l4_ref[r, :, sl]
        for r in range(16):
            s16o[c, pl.ds(r, tm // 16, stride=16), :] = o16_ref[r, :, sl].astype(F32)
            s16l[c, pl.ds(r, tm // 16, stride=16), :] = l16_ref[r, :, sl]
    wide = lambda ref: jnp.concatenate([ref[c] for c in range(nslab)], axis=1)
    l1, l4, l16 = l1_ref[...], wide(s4l), wide(s16l)
    mx = jnp.maximum(jnp.maximum(l1, l4), l16)
    e1, e4, e16 = jnp.exp(l1 - mx), jnp.exp(l4 - mx), jnp.exp(l16 - mx)
    ya = (e1 * o1_ref[...].astype(F32) + e4 * wide(s4o) + e16 * wide(s16o)) * (1.0 / (e1 + e4 + e16))

    u = cg_ref[...].astype(F32) * h_ref[...].astype(F32)
    last = cgp_ref.shape[0] - 1
    u_before = (cgp_ref[...].astype(F32) * hp_ref[...].astype(F32))[last:last + 1]
    u_after = (cgn_ref[...].astype(F32) * hn_ref[...].astype(F32))[0:1]
    u_before = u_before * (it > 0).astype(F32)
    u_after = u_after * (it < nts - 1).astype(F32)
    row = lax.broadcasted_iota(jnp.int32, u.shape, 0)
    u_prev = jnp.where(row == 0, u_before, pltpu.roll(u, 1, 0))
    u_next = jnp.where(row == tm - 1, u_after, pltpu.roll(u, tm - 1, 0))
    cw = cw_ref[...]
    yb = bg_ref[...].astype(F32) * (cw[0:1] * u_prev + cw[1:2] * u + cw[2:3] * u_next)

    out_ref[...] = x_ref[...] + _dot(ya.astype(BF16), wa_ref[...]) + _dot(yb.astype(BF16), wb_ref[...])


def _even_tail_kernel(*refs, tm, nts, nsub):
    mixer_in, cross_in, out_ref, scratch = refs[:17], refs[17:22], refs[22], refs[23:]
    x_mid = scratch[4]
    _even_out_kernel(*mixer_in, x_mid, *scratch[:4], tm=tm, nts=nts)
    _cross_kernel(x_mid, *cross_in, out_ref, nsub=nsub)


def _cross_specs(seq, n_mem, tm):
    nts = seq // tm
    return [_full((1, D_MODEL)),
            pl.BlockSpec((n_mem, D_MODEL), lambda i: (i // nts, 0)),
            pl.BlockSpec((n_mem, D_MODEL), lambda i: (i // nts, 1)),
            _full((D_MODEL, D_MODEL)),
            _full((D_MODEL, D_MODEL))]


def _even_tail(x, o1, l1, o4, l4, o16, l16, zc, conv_w, w_out, cross, batch, seq):
    g_cross, kv, wq, wo, n_mem = cross
    m = x.shape[0]
    tm = TM
    nts = seq // tm
    hb = 16
    nhb = m // hb
    kern = functools.partial(_even_tail_kernel, tm=tm, nts=nts, nsub=2)
    nat = lambda: pl.BlockSpec((tm, A_WIDTH), lambda i: (i, 0))
    dil = lambda d: pl.BlockSpec((None, d, tm // d, A_WIDTH), lambda i: (i // nts, 0, i % nts, 0))
    col = lambda c: pl.BlockSpec((tm, B_WIDTH), lambda i: (i, c))
    before = lambda c: pl.BlockSpec((hb, B_WIDTH), lambda i: (jnp.maximum(i * (tm // hb) - 1, 0), c))
    after = lambda c: pl.BlockSpec((hb, B_WIDTH), lambda i: (jnp.minimum((i + 1) * (tm // hb), nhb - 1), c))
    return pl.pallas_call(
        kern,
        grid=(m // tm,),
        in_specs=[nat(), nat(), dil(4), dil(4), dil(16), dil(16),
                  col(0), col(1), col(2), before(1), before(2), after(1), after(2),
                  _full((3, B_WIDTH)),
                  pl.BlockSpec((A_WIDTH, D_MODEL), lambda i: (0, 0)),
                  pl.BlockSpec((B_WIDTH, D_MODEL), lambda i: (1, 0)),
                  pl.BlockSpec((tm, D_MODEL), lambda i: (i, 0))] + _cross_specs(seq, n_mem, tm),
        out_specs=pl.BlockSpec((tm, D_MODEL), lambda i: (i, 0)),
        out_shape=jax.ShapeDtypeStruct((m, D_MODEL), F32),
        scratch_shapes=[pltpu.VMEM((A_WIDTH // LANES, tm, LANES), F32)] * 4 + [pltpu.VMEM((tm, D_MODEL), F32)],
        compiler_params=_cparams(("parallel",)),
        name="even_tail",
    )(o1, l1, o4, l4, o16, l16, zc, zc, zc, zc, zc, zc, zc, conv_w, w_out, w_out, x,
      g_cross.reshape(1, D_MODEL), kv, kv, wq, wo)


def _cross_kernel(x_ref, g_ref, k_ref, v_ref, wq_ref, wo_ref, out_ref, *, nsub):
    rows = x_ref.shape[0] // nsub
    g = g_ref[...]
    heads = [slice(h * CA_HEAD_DIM, (h + 1) * CA_HEAD_DIM) for h in range(CA_HEADS)]
    k = k_ref[...]
    v = v_ref[...]
    qs = []
    for i in range(nsub):
        xn = _rms(x_ref[i * rows:(i + 1) * rows, :], g).astype(BF16)
        qs.append((_dot(xn, wq_ref[...]) * (CA_HEAD_DIM ** -0.5)).astype(BF16))
    units = [(i, h) for i in range(nsub) for h in range(CA_HEADS)]
    scores, probs, outs = {}, {}, {}
    for step in range(len(units) + 2):
        if step < len(units):
            i, h = units[step]
            scores[step] = _dot_nt(qs[i][:, heads[h]], k[:, heads[h]])
        if 0 <= step - 1 < len(units):
            s = scores.pop(step - 1)
            e = jnp.exp(s - jnp.max(s, axis=-1, keepdims=True))
            probs[step - 1] = (e * (1.0 / jnp.sum(e, axis=-1, keepdims=True))).astype(BF16)
        if 0 <= step - 2 < len(units):
            i, h = units[step - 2]
            outs[i, h] = _dot(probs.pop(step - 2), v[:, heads[h]]).astype(BF16)
            if h == CA_HEADS - 1:
                o = jnp.concatenate([outs.pop((i, hh)) for hh in range(CA_HEADS)], axis=1)
                out_ref[i * rows:(i + 1) * rows, :] = x_ref[i * rows:(i + 1) * rows, :] + _dot(o, wo_ref[...])


def _ffn_kernel(x_ref, g_ref, wgu_ref, wd_ref, gf_ref, out_ref, *, final_norm, tf, nsub):
    rows = x_ref.shape[0] // nsub
    g = g_ref[...]
    accs = [x_ref[i * rows:(i + 1) * rows, :] for i in range(nsub)]
    xns = [_rms(x, g).astype(BF16) for x in accs]
    nf = D_FF // tf
    acts = None
    for f in range(nf + 1):
        new_acts = []
        if f < nf:
            for i in range(nsub):
                gate = _dot(xns[i], wgu_ref[:, f * tf:(f + 1) * tf])
                up = _dot(xns[i], wgu_ref[:, D_FF + f * tf:D_FF + (f + 1) * tf])
                new_acts.append((gate * _sigmoid(gate) * up).astype(BF16))
        if f > 0:
            for i in range(nsub):
                accs[i] = accs[i] + _dot(acts[i], wd_ref[(f - 1) * tf:f * tf, :])
        acts = new_acts
    for i in range(nsub):
        y = _rms(accs[i], gf_ref[...]) if final_norm else accs[i]
        out_ref[i * rows:(i + 1) * rows, :] = y


def _ffn(x, g, w_gu, w_down, g_final, final_norm):
    m = x.shape[0]
    tm = TM
    kern = functools.partial(_ffn_kernel, final_norm=final_norm, tf=TF_FFN, nsub=2)
    resident = lambda shape: pl.BlockSpec(shape, lambda i: (0, 0), pipeline_mode=pl.Buffered(1))
    return pl.pallas_call(
        kern,
        grid=(m // tm,),
        in_specs=[pl.BlockSpec((tm, D_MODEL), lambda i: (i, 0)),
                  _full((1, D_MODEL)),
                  resident(w_gu.shape),
                  resident(w_down.shape),
                  _full((1, D_MODEL))],
        out_specs=pl.BlockSpec((tm, D_MODEL), lambda i: (i, 0)),
        out_shape=jax.ShapeDtypeStruct((m, D_MODEL), F32),
        compiler_params=_cparams(("parallel",)),
        name="swiglu",
    )(x, g.reshape(1, D_MODEL), w_gu, w_down, g_final.reshape(1, D_MODEL))


def _head_sum(x, e_ref):
    x = x.astype(BF16)
    e = e_ref[...]
    width = e.shape[0]
    return jnp.concatenate([_dot(x[:, c * width:(c + 1) * width], e) for c in range(x.shape[1] // width)],
                           axis=1)


def _sigmoid(x):
    return 0.5 * jnp.tanh(0.5 * x) + 0.5


def _store_pairs(ref, val, *lead):
    for p in range(val.shape[1] // LANES):
        ref[(*lead, p)] = val[:, p * LANES:(p + 1) * LANES].astype(ref.dtype)


def _rwkv_pre_kernel(x_ref, xb_ref, xa_ref, g_ref, mu_ref, wr_ref, wk_ref, wv_ref, w1_ref, w2_ref, w0_ref,
                     a1_ref, a2_ref, a0_ref, g1_ref, g2_ref, kk_ref, ka_ref, rk_ref, e_ref,
                     sm_ref, r_out, v_out, kn_out, kd_out, ba_out, lw_out, gate_out, bonus_out, *, nts):
    it = pl.program_id(0) % nts
    g = g_ref[...]
    xn_bf = _rms(x_ref[...], g).astype(BF16)
    before = (_rms(xb_ref[...], g) * (it > 0).astype(F32)).astype(BF16)
    after = (_rms(xa_ref[...], g) * (it < nts - 1).astype(F32)).astype(BF16)
    xx_bf = _dot(sm_ref[...], jnp.concatenate([before, xn_bf, after], axis=0)).astype(BF16)
    mu = mu_ref[...].astype(BF16)
    mix = lambda i: xn_bf + xx_bf * mu[i:i + 1]

    tw = jnp.tanh(_dot(mix(1), w1_ref[...])).astype(BF16)
    ta = _dot(mix(4), a1_ref[...]).astype(BF16)
    tg = _sigmoid(_dot(mix(5), g1_ref[...])).astype(BF16)
    k_lin = _dot(mix(2), wk_ref[...])
    wls = [w0_ref[d:d + 1] + _dot(tw, w2_ref[d]) for d in range(2)]
    r = _dot(mix(0), wr_ref[...])
    for d in range(2):
        _store_pairs(lw_out, (-0.5 * DECAY_SCALE) * jnp.tanh(0.5 * wls[d]) - 0.5 * DECAY_SCALE, d)
    kn = k_lin * kk_ref[...]
    kn = kn * lax.rsqrt(jnp.maximum(_head_sum(kn * kn, e_ref), 1e-24))
    _store_pairs(kn_out, kn)
    als = [a0_ref[d:d + 1] + _dot(ta, a2_ref[d]) for d in range(2)]
    v = _dot(mix(3), wv_ref[...])
    hk = k_lin * (0.5 * ka_ref[...])
    k_base = k_lin - hk
    hn = 0.5 * kn
    kd_sum = jnp.zeros_like(k_lin)
    for d in range(2):
        t = jnp.tanh(0.5 * als[d])
        kd = k_base + hk * t
        kd_sum = kd_sum + kd
        _store_pairs(kd_out, kd, d)
        _store_pairs(ba_out, hn + hn * t, d)
    gate_out[...] = _dot(tg, g2_ref[...]).astype(BF16)
    _store_pairs(r_out, r)
    _store_pairs(v_out, v)
    bonus_out[...] = (_head_sum(r * kd_sum * rk_ref[...], e_ref) * v).astype(BF16)


def _rwkv_pre(x, g, p, batch, seq):
    m = x.shape[0]
    tm = TM_RW
    nts = seq // tm
    hb = 16
    nhb = m // hb
    kern = functools.partial(_rwkv_pre_kernel, nts=nts)
    row = lambda: pl.BlockSpec((tm, D_MODEL), lambda i: (i, 0))
    npair = D_MODEL // LANES
    pairs = lambda: pl.BlockSpec((npair, tm, LANES), lambda i: (0, i, 0))
    pairs2 = lambda: pl.BlockSpec((2, npair, tm, LANES), lambda i: (0, 0, i, 0))
    t = jnp.arange(tm)[:, None]
    c = jnp.arange(tm + 2 * hb)[None, :] - hb
    shift = (jnp.where(c == t, -1.0, 0.0) + jnp.where(jnp.abs(c - t) == 1, 0.5, 0.0)).astype(BF16)
    args = (x, x, x, g.reshape(1, D_MODEL), p['mu'], p['wr'], p['wk'], p['wv'], p['w1'], p['w2'], p['w0'],
            p['a1'], p['a2'], p['a0'], p['g1'], p['g2'], p['kk'], p['ka'], p['rk'], p['e'], shift)
    in_specs = [row(),
                pl.BlockSpec((hb, D_MODEL), lambda i: (jnp.maximum(i * (tm // hb) - 1, 0), 0)),
                pl.BlockSpec((hb, D_MODEL), lambda i: (jnp.minimum((i + 1) * (tm // hb), nhb - 1), 0))]
    in_specs += [_full(a.shape) for a in args[3:]]
    out_shape = (jax.ShapeDtypeStruct((npair, m, LANES), BF16),) * 3 + (
        jax.ShapeDtypeStruct((2, npair, m, LANES), BF16), jax.ShapeDtypeStruct((2, npair, m, LANES), BF16),
        jax.ShapeDtypeStruct((2, npair, m, LANES), F32),
        jax.ShapeDtypeStruct((m, D_MODEL), BF16), jax.ShapeDtypeStruct((m, D_MODEL), BF16))
    out_specs = (pairs(), pairs(), pairs(), pairs2(), pairs2(), pairs2(), row(), row())
    return pl.pallas_call(
        kern, grid=(m // tm,), in_specs=in_specs, out_specs=out_specs, out_shape=out_shape,
        compiler_params=_cparams(("parallel",)), name="rwkv_pre",
    )(*args)


def _bd(x, first):
    z = jnp.zeros_like(x)
    return jnp.concatenate([jnp.where(first, x, z), jnp.where(first, z, x)], axis=0)


def _compact(full, first):
    return jnp.where(first, full[:CHUNK], full[CHUNK:])


def _wkv_masks(sgn):
    c = CHUNK
    row = lax.broadcasted_iota(jnp.int32, (c, c), 0)
    col = lax.broadcasted_iota(jnp.int32, (c, c), 1)
    tri = jnp.where((row - col) * sgn >= 0, 1.0, 0.0).astype(BF16)
    prow = lax.broadcasted_iota(jnp.int32, (c, LANES), 0)
    pcol = lax.broadcasted_iota(jnp.int32, (c, LANES), 1) & (c - 1)
    dlt = (prow - pcol) * sgn
    strict = dlt > 0
    same = lambda sh: (prow >> sh) == (pcol >> sh)
    levels = []
    sh = 2
    while (1 << sh) < c:
        levels.append(jnp.where(strict & same(sh + 1) & jnp.logical_not(same(sh)), 1.0, 0.0))
        sh += 1
    in4 = same(2)
    return dict(tri=tri, strict=strict, incl=dlt >= 0, eye=jnp.where(prow == pcol, 1.0, 0.0), levels=levels,
                tri4=jnp.where(strict & in4, 1.0, 0.0),
                off2=jnp.where(in4 & (dlt == 2), 1.0, 0.0), off3=jnp.where(in4 & (dlt == 3), 1.0, 0.0),
                row1=sgn % c, row2=(2 * sgn) % c, lane1=(-sgn) % LANES, lane2=(-2 * sgn) % LANES)


def _inverse4(a_ab, mk):
    a = a_ab * mk['tri4']
    up1 = pltpu.roll(a, mk['row1'], 0)
    up2 = pltpu.roll(a, mk['row2'], 0)
    left1 = pltpu.roll(a, mk['lane1'], 1)
    left2 = pltpu.roll(a, mk['lane2'], 1)
    sq = left1 * up1 * mk['off2'] + (left1 * up2 + left2 * up1) * mk['off3']
    cube = pltpu.roll(sq, mk['lane1'], 1) * up2 * mk['off3']
    return mk['eye'] + a + sq + cube


def _wkv_local(chains, first):
    c = CHUNK
    for s in chains:
        hl = _dot(s['mk']['tri'], jnp.concatenate(_split(s['lw']), axis=1))
        s['cum'] = hl[:, :LANES] + hl[:, LANES:]
    yield
    for s in chains:
        cum, lw = s['cum'], s['lw']
        total = jnp.sum(lw, axis=0, keepdims=True)
        p_inv = jnp.exp(-cum)
        p_hat = jnp.exp(total - cum)
        s['at'] = (-s['kn'] * jnp.exp(cum - lw)).astype(BF16)
        s['rt'] = (s['r'] * jnp.exp(cum)).astype(BF16)
        s['bt'] = (s['ba'] * p_inv).astype(BF16)
        s['kt'] = (s['kd'] * p_inv).astype(BF16)
        s['bh'] = (s['ba'] * p_hat).astype(BF16)
        s['kh'] = (s['kd'] * p_hat).astype(BF16)
        s['v_bf'] = s['v'].astype(BF16)
        s['p_end'] = jnp.exp(total)
        s['lhs'] = jnp.concatenate([s['at'], s['rt']], axis=0)
    yield
    for s in chains:
        s['sbk'] = _dot_nt(s['lhs'], jnp.concatenate([_bd(s['bt'], first), _bd(s['kt'], first)], axis=0))
    yield
    for s in chains:
        mk = s['mk']
        sb, sk = s['sbk'][:, :LANES], s['sbk'][:, LANES:]
        s['a_ab'] = jnp.where(mk['strict'], sb[:c], 0.0)
        s['a_rb'] = jnp.where(mk['incl'], sb[c:], 0.0).astype(BF16)
        s['a_kk'] = jnp.concatenate([jnp.where(mk['strict'], sk[:c], 0.0),
                                     jnp.where(mk['incl'], sk[c:], 0.0)], axis=0).astype(BF16)
        s['inv'] = _inverse4(s['a_ab'], mk)
    yield
    for s in chains:
        s['av'] = _dot(s['a_kk'], _bd(s['v_bf'], first))
        s['y0'] = _compact(_dot_tn(s['v_bf'], s['kh']), first)
    yield
    for lvl in range(len(chains[0]['mk']['levels'])):
        for s in chains:
            s['inv_bf'] = s['inv'].astype(BF16)
            s['inner'] = _dot((s['a_ab'] * s['mk']['levels'][lvl]).astype(BF16), _bd(s['inv_bf'], first))
        yield
        for s in chains:
            s['inv'] = s['inv'] + _dot(s['inv_bf'], _bd(s['inner'].astype(BF16), first))
        yield
    for s in chains:
        rhs = jnp.concatenate([_bd(s['at'], first), _bd(s['av'][:c].astype(BF16), first)], axis=1)
        wu = _dot(s['inv'].astype(BF16), rhs)
        s['w1'] = wu[:, :LANES].astype(BF16)
        s['u0'] = wu[:, LANES:]
        s['o0'] = s['av'][c:]


def _wkv_seq(rows, ys, out_refs, first):
    c = CHUNK
    for row in rows:
        wrs = [_dot_nt(jnp.concatenate([s['w1'], s['rt']], axis=0), _bd(y.astype(BF16), first))
               for s, y in zip(row, ys)]
        yield
        us = [(wr[:c] + s['u0']).astype(BF16) for s, wr in zip(row, wrs)]
        for s, wr, u, (o_ref, p) in zip(row, wrs, us, out_refs):
            o_ref[p, s['sl'], :] = (wr[c:] + _dot(s['a_rb'], _bd(u, first)) + s['o0']).astype(o_ref.dtype)
        ys[:] = [s['p_end'] * y + _compact(_dot_tn(u, s['bh']), first) + s['y0'] for s, y, u in zip(row, ys, us)]
        yield


def _run_staggered(local_gens, make_seq, period):
    end = object()
    started, finished, tick = 0, 0, 0
    active, updates = [], []
    while finished < len(local_gens) or updates:
        if started < len(local_gens) and tick >= started * period:
            active.append(started)
            started += 1
        for g in list(active):
            if next(local_gens[g], end) is end:
                active.remove(g)
                finished += 1
                updates.append(make_seq(g))
        if updates and next(updates[0], end) is end:
            updates.pop(0)
        tick += 1


def _wkv_kernel(rf, vf, nf, kdf, baf, lwf, rb, vb, nb, kdb, bab, lwb, of_ref, ob_ref, state, *, ts, npairs):
    first = lax.broadcasted_iota(jnp.int32, (CHUNK, LANES), 1) < RWKV_HEAD

    @pl.when(pl.program_id(2) == 0)
    def _():
        state[...] = jnp.zeros_like(state)

    nchunk = ts // CHUNK
    names = ('r', 'v', 'kn', 'kd', 'ba', 'lw')
    dirs = ((_wkv_masks(1), (rf, vf, nf, kdf, baf, lwf), of_ref, list(range(nchunk))),
            (_wkv_masks(-1), (rb, vb, nb, kdb, bab, lwb), ob_ref, list(range(nchunk - 1, -1, -1))))
    scans = [(d, p) for d in range(2) for p in range(npairs)]
    out_refs = [(dirs[d][2], p) for d, p in scans]
    steps = []
    for k in range(nchunk):
        row = []
        for d, p in scans:
            mk, ins, _, order = dirs[d]
            sl = pl.ds(order[k] * CHUNK, CHUNK)
            chain = {n: ref[p, sl, :].astype(F32) for n, ref in zip(names, ins)}
            chain.update(mk=mk, sl=sl)
            row.append(chain)
        steps.append(row)
    ys = [state[d, p] for d, p in scans]
    groups = [steps[k:k + WKV_GROUP] for k in range(0, nchunk, WKV_GROUP)]
    _run_staggered([_wkv_local([s for row in rows for s in row], first) for rows in groups],
                   lambda g: _wkv_seq(groups[g], ys, out_refs, first), WKV_STAGGER)
    for (d, p), y in zip(scans, ys):
        state[d, p] = y


def _wkv(r, v, kn, kd, ba, lw, batch, seq):
    ts, npairs = TS_WKV, PAIRS_WKV
    nt = seq // ts
    npair = D_MODEL // LANES
    m = batch * seq
    fwd = lambda b, t: b * nt + t
    bwd = lambda b, t: b * nt + nt - 1 - t
    s3 = lambda at: pl.BlockSpec((npairs, ts, LANES), lambda b, g, t: (g, at(b, t), 0))
    s4 = lambda d, at: pl.BlockSpec((None, npairs, ts, LANES), lambda b, g, t: (d, g, at(b, t), 0))
    return pl.pallas_call(
        functools.partial(_wkv_kernel, ts=ts, npairs=npairs),
        grid=(batch, npair // npairs, nt),
        in_specs=[s3(fwd), s3(fwd), s3(fwd), s4(0, fwd), s4(0, fwd), s4(0, fwd),
                  s3(bwd), s3(bwd), s3(bwd), s4(1, bwd), s4(1, bwd), s4(1, bwd)],
        out_specs=(s3(fwd), s3(bwd)),
        out_shape=(jax.ShapeDtypeStruct((npair, m, LANES), BF16),) * 2,
        scratch_shapes=[pltpu.VMEM((2, npairs, CHUNK, LANES), F32)],
        compiler_params=_cparams(("parallel", "parallel", "arbitrary")),
        name="wkv",
    )(r, v, kn, kd, ba, lw, r, v, kn, kd, ba, lw)


def _rwkv_post_kernel(of_ref, ob_ref, bonus_ref, gate_ref, lw_ref, lb_ref, e_ref, wo_ref, x_ref, out_ref):
    y = jnp.concatenate([of_ref[p].astype(F32) + ob_ref[p].astype(F32) for p in range(of_ref.shape[0])], axis=1)
    mean = _head_sum(y, e_ref) * (1.0 / RWKV_HEAD)
    yc = y - mean
    var = _head_sum(yc * yc, e_ref) * (1.0 / RWKV_HEAD)
    yn = yc * lax.rsqrt(var + GN_EPS)
    y2 = yn * lw_ref[...] + lb_ref[...] + bonus_ref[...].astype(F32)
    out_ref[...] = x_ref[...] + _dot((y2 * gate_ref[...].astype(F32)).astype(BF16), wo_ref[...])


def _odd_tail_kernel(*refs, nsub):
    mixer_in, cross_in, out_ref, x_mid = refs[:9], refs[9:14], refs[14], refs[15]
    _rwkv_post_kernel(*mixer_in, x_mid)
    _cross_kernel(x_mid, *cross_in, out_ref, nsub=nsub)


def _odd_tail(x, o_f, o_b, bonus, gate, p, cross, seq):
    g_cross, kv, wq, wo, n_mem = cross
    m = x.shape[0]
    tm = TM
    row = lambda: pl.BlockSpec((tm, D_MODEL), lambda i: (i, 0))
    pairs = lambda: pl.BlockSpec((D_MODEL // LANES, tm, LANES), lambda i: (0, i, 0))
    return pl.pallas_call(
        functools.partial(_odd_tail_kernel, nsub=2),
        grid=(m // tm,),
        in_specs=[pairs(), pairs(), row(), row(), _full((1, D_MODEL)), _full((1, D_MODEL)),
                  _full(p['e'].shape), _full((D_MODEL, D_MODEL)), row()] + _cross_specs(seq, n_mem, tm),
        out_specs=row(),
        out_shape=jax.ShapeDtypeStruct((m, D_MODEL), F32),
        scratch_shapes=[pltpu.VMEM((tm, D_MODEL), F32)],
        compiler_params=_cparams(("parallel",)),
        name="odd_tail",
    )(o_f, o_b, bonus, gate, p['lnx_w'], p['lnx_b'], p['e'], p['wo'], x,
      g_cross.reshape(1, D_MODEL), kv, kv, wq, wo)


def _prep_odd(o, rw_mu, rw_wr, rw_wk, rw_wv, rw_wo, rw_w0, rw_w1, rw_w2, rw_a0, rw_a1, rw_a2, rw_g1, rw_g2,
              rw_kk, rw_ka, rw_rk, rw_lnx_w, rw_lnx_b):
    bf = lambda a: a.astype(BF16)

    def pad_dir(w):
        z = jnp.zeros_like(w[0])
        return jnp.stack([jnp.concatenate([w[0], z], axis=0), jnp.concatenate([z, w[1]], axis=0)])

    lane = jnp.arange(MXU_WIDTH) // RWKV_HEAD
    return dict(
        mu=rw_mu[o], wr=bf(rw_wr[o]), wk=bf(rw_wk[o]), wv=bf(rw_wv[o]), wo=bf(rw_wo[o]),
        w0=rw_w0[o], w1=bf(jnp.concatenate([rw_w1[o, 0], rw_w1[o, 1]], axis=1)), w2=bf(pad_dir(rw_w2[o])),
        a0=rw_a0[o], a1=bf(jnp.concatenate([rw_a1[o, 0], rw_a1[o, 1]], axis=1)), a2=bf(pad_dir(rw_a2[o])),
        g1=bf(rw_g1[o]), g2=bf(rw_g2[o]),
        kk=rw_kk[o].reshape(1, D_MODEL), ka=rw_ka[o].reshape(1, D_MODEL), rk=rw_rk[o].reshape(1, D_MODEL),
        lnx_w=rw_lnx_w[o].reshape(1, D_MODEL), lnx_b=rw_lnx_b[o].reshape(1, D_MODEL),
        e=(lane[:, None] == lane[None, :]).astype(BF16))


def _even_layer(x, g, w_in, w_out, conv_w, cos, sin, cross, batch, seq):
    z1, z4, z16, zc = _even_proj(x, g, w_in, cos, sin, batch, seq)
    m = x.shape[0]
    o1, l1 = _dilated_branch(z1.reshape(3, batch, seq, A_WIDTH), seq)
    o4, l4 = _dilated_branch(z4.reshape(3, batch * 4, seq // 4, A_WIDTH), seq // 4)
    o16, l16 = _dilated_branch(z16.reshape(3, batch * 16, seq // 16, A_WIDTH), seq // 16)
    o1, l1 = o1.reshape(m, A_WIDTH), l1.reshape(m, A_WIDTH)
    d4 = lambda a: a.reshape(batch, 4, seq // 4, A_WIDTH)
    d16 = lambda a: a.reshape(batch, 16, seq // 16, A_WIDTH)
    return _even_tail(x, o1, l1, d4(o4), d4(l4), d16(o16), d16(l16), zc, conv_w, w_out, cross, batch, seq)


def _odd_layer(x, g, p, cross, batch, seq):
    r, v, kn, kd, ba, lw, gate, bonus = _rwkv_pre(x, g, p, batch, seq)
    o_f, o_b = _wkv(r, v, kn, kd, ba, lw, batch, seq)
    return _odd_tail(x, o_f, o_b, bonus, gate, p, cross, seq)


def _trunk(x, mem, w):
    batch, seq, _ = x.shape
    n_mem = mem.shape[1]
    x = x.reshape(batch * seq, D_MODEL)
    mem = mem.reshape(batch * n_mem, D_MODEL)
    cos, sin = _rope_tables(seq)
    for l in range(DEPTH):
        kv = _norm_matmul(mem, w['norm_mem'][l], w['ca_wkv'][l], TM, 2 * D_MODEL, BF16)
        cross = (w['norm_cross'][l], kv, w['ca_wq'][l], w['ca_wo'][l], n_mem)
        if l % 2 == 0:
            e = l // 2
            x = _even_layer(x, w['norm_mix'][l], w['ab_w_in'][e], w['ab_w_out'][e], w['ab_conv'][e],
                            cos, sin, cross, batch, seq)
        else:
            x = _odd_layer(x, w['norm_mix'][l], w['odd'][l // 2], cross, batch, seq)
        x = _ffn(x, w['norm_ffn'][l], w['ffn_wgu'][l], w['ffn_wdown'][l], w['norm_final'], l == DEPTH - 1)
    return x.reshape(batch, seq, D_MODEL)


def kernel(x_prompt, x_sample, mem_prompt, mem_sample, norm_mix, norm_cross, norm_mem, norm_ffn, norm_final,
           ab_w_in, ab_w_out, ab_conv, rw_mu, rw_wr, rw_wk, rw_wv, rw_wo, rw_w0, rw_w1, rw_w2, rw_a0, rw_a1,
           rw_a2, rw_g1, rw_g2, rw_kk, rw_ka, rw_rk, rw_lnx_w, rw_lnx_b, ca_wq, ca_wkv, ca_wo, ffn_wgu,
           ffn_wdown):
    bf = lambda a: a.astype(BF16)
    w = dict(norm_mix=norm_mix, norm_cross=norm_cross, norm_mem=norm_mem, norm_ffn=norm_ffn,
             norm_final=norm_final, ab_w_in=bf(ab_w_in), ab_w_out=bf(ab_w_out), ab_conv=ab_conv,
             ca_wq=bf(ca_wq), ca_wkv=bf(ca_wkv), ca_wo=bf(ca_wo), ffn_wgu=bf(ffn_wgu), ffn_wdown=bf(ffn_wdown),
             odd=[_prep_odd(o, rw_mu, rw_wr, rw_wk, rw_wv, rw_wo, rw_w0, rw_w1, rw_w2, rw_a0, rw_a1, rw_a2,
                            rw_g1, rw_g2, rw_kk, rw_ka, rw_rk, rw_lnx_w, rw_lnx_b)
                  for o in range(rw_mu.shape[0])])
    return _trunk(x_prompt, mem_prompt, w), _trunk(x_sample, mem_sample, w)
```

```python
import functools

import jax
import jax.numpy as jnp
from jax import lax
from jax.experimental import pallas as pl
from jax.experimental.pallas import tpu as pltpu

F32 = jnp.float32
BF16 = jnp.bfloat16

D_MODEL = 1024
DEPTH = 4
HEAD_DIM = 64
A_WIDTH = 512
B_WIDTH = 512
BAND = 64
ROPE_THETA = 500000.0
ROPE_DIM = 16
RWKV_HEAD = 64
GN_EPS = 64e-5
DECAY_SCALE = 0.6065306597126334
CA_HEADS = 4
CA_HEAD_DIM = 256
D_FF = 2816
RMS_EPS = 1e-6
NEG_INF = -1e30
CHUNK = 64
LANES = 128
MXU_WIDTH = 256

VMEM_LIMIT = 48 * 1024 * 1024

TM = 512
TM_RW = 256
TB_DIL = 1024
TQ_DIL = 128
DIL_LAG = 1
TF_FFN = 256
TS_WKV = 1024
PAIRS_WKV = 4
WKV_GROUP = 2
WKV_STAGGER = 10


def _cparams(sem):
    return pltpu.CompilerParams(dimension_semantics=sem, vmem_limit_bytes=VMEM_LIMIT)


def _full(shape):
    n = len(shape)
    return pl.BlockSpec(shape, lambda *_: (0,) * n)


def _rms(x, g):
    ms = jnp.mean(x * x, axis=-1, keepdims=True)
    return x * lax.rsqrt(ms + RMS_EPS) * g


def _dot(a, b):
    return jnp.dot(a, b, preferred_element_type=F32)


def _dot_nt(a, b):
    return lax.dot_general(a, b, (((1,), (1,)), ((), ())), preferred_element_type=F32)


def _dot_tn(a, b):
    return lax.dot_general(a, b, (((0,), (0,)), ((), ())), preferred_element_type=F32)


def _split(x):
    hi = x.astype(BF16)
    lo = (x - hi.astype(F32)).astype(BF16)
    return hi, lo


def _norm_matmul_kernel(x_ref, g_ref, w_ref, o_ref, xn_ref):
    @pl.when(pl.program_id(1) == 0)
    def _():
        xn_ref[...] = _rms(x_ref[...], g_ref[...]).astype(BF16)

    o_ref[...] = _dot(xn_ref[...], w_ref[...]).astype(o_ref.dtype)


def _norm_matmul(x, g, w, tm, tn, out_dtype):
    m, k = x.shape
    n = w.shape[1]
    return pl.pallas_call(
        _norm_matmul_kernel,
        grid=(m // tm, n // tn),
        in_specs=[pl.BlockSpec((tm, k), lambda i, j: (i, 0)),
                  pl.BlockSpec((1, k), lambda i, j: (0, 0)),
                  pl.BlockSpec((k, tn), lambda i, j: (0, j))],
        out_specs=pl.BlockSpec((tm, tn), lambda i, j: (i, j)),
        out_shape=jax.ShapeDtypeStruct((m, n), out_dtype),
        scratch_shapes=[pltpu.VMEM((tm, k), BF16)],
        compiler_params=_cparams(("parallel", "arbitrary")),
        name="norm_matmul",
    )(x, g.reshape(1, k), w)


def _even_proj_kernel(x_ref, g_ref, w_ref, cos_ref, sin_ref, p4_ref, p16_ref,
                      o1_ref, o4_ref, o16_ref, oc_ref, *, tm):
    xn = _rms(x_ref[...], g_ref[...]).astype(BF16)
    reps = A_WIDTH // LANES
    cos = jnp.concatenate([cos_ref[...]] * reps, axis=1)
    sin = jnp.concatenate([sin_ref[...]] * reps, axis=1)
    lane = lax.broadcasted_iota(jnp.int32, (tm, A_WIDTH), 1) & (HEAD_DIM - 1)
    half = ROPE_DIM // 2
    nparts = 3
    acc = None
    for j in range(nparts + 1):
        nxt = _dot(xn, w_ref[:, j * A_WIDTH:(j + 1) * A_WIDTH]) if j < nparts else None
        if j == nparts:
            oc_ref[...] = _dot(xn, w_ref[:, 3 * A_WIDTH:]).astype(BF16)
        if j > 0:
            a, part = acc, j - 1
            if part < 2:
                partner = jnp.where(lane < half, pltpu.roll(a, A_WIDTH - half, 1), pltpu.roll(a, half, 1))
                a = a * cos + partner * sin
                if part == 0:
                    a = a * (HEAD_DIM ** -0.5)
            a = a.astype(BF16)
            o1_ref[part] = a
            for d, p_ref, o_ref in ((4, p4_ref, o4_ref), (16, p16_ref, o16_ref)):
                perm = _dot(p_ref[...], a).astype(BF16)
                for r in range(d):
                    o_ref[part, r] = perm[r * (tm // d):(r + 1) * (tm // d)]
        acc = nxt


def _dedilate_perm(tm, d):
    i = jnp.arange(tm)
    src = (i % (tm // d)) * d + i // (tm // d)
    return (src[:, None] == jnp.arange(tm)[None, :]).astype(BF16)


def _even_proj(x, g, w_in, cos, sin, batch, seq):
    m = x.shape[0]
    tm = TM
    nts = seq // tm
    kern = functools.partial(_even_proj_kernel, tm=tm)
    out_shape = (jax.ShapeDtypeStruct((3, m, A_WIDTH), BF16),
                 jax.ShapeDtypeStruct((3, batch, 4, seq // 4, A_WIDTH), BF16),
                 jax.ShapeDtypeStruct((3, batch, 16, seq // 16, A_WIDTH), BF16),
                 jax.ShapeDtypeStruct((m, 3 * B_WIDTH), BF16))
    return pl.pallas_call(
        kern,
        grid=(m // tm,),
        in_specs=[pl.BlockSpec((tm, D_MODEL), lambda i: (i, 0)),
                  _full((1, D_MODEL)),
                  _full(w_in.shape),
                  pl.BlockSpec((tm, LANES), lambda i: (i % nts, 0)),
                  pl.BlockSpec((tm, LANES), lambda i: (i % nts, 0)),
                  _full((tm, tm)), _full((tm, tm))],
        out_specs=(pl.BlockSpec((3, tm, A_WIDTH), lambda i: (0, i, 0)),
                   pl.BlockSpec((3, None, 4, tm // 4, A_WIDTH), lambda i: (0, i // nts, 0, i % nts, 0)),
                   pl.BlockSpec((3, None, 16, tm // 16, A_WIDTH), lambda i: (0, i // nts, 0, i % nts, 0)),
                   pl.BlockSpec((tm, 3 * B_WIDTH), lambda i: (i, 0))),
        out_shape=out_shape,
        compiler_params=_cparams(("parallel",)),
        name="even_proj",
    )(x, g.reshape(1, D_MODEL), w_in, cos, sin, _dedilate_perm(tm, 4), _dedilate_perm(tm, 16))


def _rope_tables(seq):
    half = ROPE_DIM // 2
    inv = ROPE_THETA ** (-2.0 * jnp.arange(half, dtype=F32) / ROPE_DIM)
    ang = jnp.arange(seq, dtype=F32)[:, None] * inv[None, :]
    cos, sin = jnp.cos(ang), jnp.sin(ang)
    rest = HEAD_DIM - ROPE_DIM
    cos_h = jnp.concatenate([cos, cos, jnp.ones((seq, rest), F32)], axis=1)
    sin_h = jnp.concatenate([-sin, sin, jnp.zeros((seq, rest), F32)], axis=1)
    reps = LANES // HEAD_DIM
    return jnp.tile(cos_h, (1, reps)), jnp.tile(sin_h, (1, reps))


def _dil_kernel(q_ref, kp_ref, kc_ref, kn_ref, vp_ref, vc_ref, vn_ref, o_ref, lse_ref, *, gs, tb, tq, length):
    qi = pl.program_id(1)
    span = tq + 2 * BAND
    ks = [jnp.concatenate([kp_ref[g], kc_ref[g], kn_ref[g]], axis=0) for g in range(gs)]
    vs = [jnp.concatenate([vp_ref[g], vc_ref[g], vn_ref[g]], axis=0) for g in range(gs)]
    row = lax.broadcasted_iota(jnp.int32, (tq, span), 0)
    col = lax.broadcasted_iota(jnp.int32, (tq, span), 1)
    rel = col - BAND - row
    band = (rel <= BAND) & (rel >= -BAND)
    first = lax.broadcasted_iota(jnp.int32, (tq, LANES), 1) < HEAD_DIM
    units = [(g, s_idx, p, hh) for g in range(gs) for s_idx in range(tb // tq)
             for p in range(A_WIDTH // LANES) for hh in range(2)]
    lag = DIL_LAG
    scores, maxes, probs, results = {}, {}, {}, {}
    for n in range(len(units) + 3 * lag):
        if n < len(units):
            g, s_idx, p, hh = units[n]
            sl = slice(p * LANES, (p + 1) * LANES)
            qp = q_ref[g, s_idx * tq:(s_idx + 1) * tq, sl]
            sel = first if hh == 0 else jnp.logical_not(first)
            key_pos = qi * tb + s_idx * tq - BAND + col
            valid = band & (key_pos >= 0) & (key_pos < length)
            s = _dot_nt(jnp.where(sel, qp, jnp.zeros_like(qp)), ks[g][s_idx * tq:s_idx * tq + span, sl])
            scores[n] = jnp.where(valid, s, NEG_INF)
        m = n - lag
        if 0 <= m < len(units):
            maxes[m] = jnp.max(scores[m], axis=-1, keepdims=True)
        m = n - 2 * lag
        if 0 <= m < len(units):
            mx = maxes.pop(m)
            e = jnp.exp(scores.pop(m) - mx)
            den = jnp.sum(e, axis=-1, keepdims=True)
            probs[m] = (e.astype(BF16), den, mx)
        m = n - 3 * lag
        if 0 <= m < len(units):
            g, s_idx, p, hh = units[m]
            sl = slice(p * LANES, (p + 1) * LANES)
            e, den, mx = probs.pop(m)
            results[hh] = (_dot(e, vs[g][s_idx * tq:s_idx * tq + span, sl]) * (1.0 / den), mx + jnp.log(den))
            if hh == 1:
                rows = slice(s_idx * tq, (s_idx + 1) * tq)
                o_ref[g, rows, sl] = jnp.where(first, results[0][0], results[1][0]).astype(BF16)
                lse_ref[g, rows, sl] = jnp.where(first, results[0][1], results[1][1])


def _dilated_branch(zd, length):
    g = zd.shape[1]
    tb = min(TB_DIL, length)
    tq = min(TQ_DIL, length)
    gs = TB_DIL // tb
    nb = length // BAND
    r = tb // BAND
    kern = functools.partial(_dil_kernel, gs=gs, tb=tb, tq=tq, length=length)

    def cur(which):
        return pl.BlockSpec((None, gs, tb, A_WIDTH), lambda b, i: (which, b, i, 0))

    def prev(which):
        return pl.BlockSpec((None, gs, BAND, A_WIDTH), lambda b, i: (which, b, jnp.maximum(i * r - 1, 0), 0))

    def nxt(which):
        return pl.BlockSpec((None, gs, BAND, A_WIDTH), lambda b, i: (which, b, jnp.minimum((i + 1) * r, nb - 1), 0))

    return pl.pallas_call(
        kern,
        grid=(g // gs, length // tb),
        in_specs=[cur(0), prev(1), cur(1), nxt(1), prev(2), cur(2), nxt(2)],
        out_specs=(pl.BlockSpec((gs, tb, A_WIDTH), lambda b, i: (b, i, 0)),
                   pl.BlockSpec((gs, tb, A_WIDTH), lambda b, i: (b, i, 0))),
        out_shape=(jax.ShapeDtypeStruct((g, length, A_WIDTH), BF16),
                   jax.ShapeDtypeStruct((g, length, A_WIDTH), F32)),
        compiler_params=_cparams(("parallel", "parallel")),
        name="dilated_attn",
    )(zd, zd, zd, zd, zd, zd, zd)


def _even_out_kernel(o1_ref, l1_ref, o4_ref, l4_ref, o16_ref, l16_ref, bg_ref, cg_ref, h_ref,
                     cgp_ref, hp_ref, cgn_ref, hn_ref, cw_ref, wa_ref, wb_ref, x_ref, out_ref,
                     s4o, s4l, s16o, s16l, *, tm, nts):
    it = pl.program_id(0) % nts
    nslab = A_WIDTH // LANES
    for c in range(nslab):
        sl = slice(c * LANES, (c + 1) * LANES)
        for r in range(4):
            s4o[c, pl.ds(r, tm // 4, stride=4), :] = o4_ref[r, :, sl].astype(F32)
            s4l[c, pl.ds(r, tm // 4, stride=4), :] = l4_ref[r, :, sl]
        for r in range(16):
            s16o[c, pl.ds(r, tm // 16, stride=16), :] = o16_ref[r, :, sl].astype(F32)
            s16l[c, pl.ds(r, tm // 16, stride=16), :] = l16_ref[r, :, sl]
    wide = lambda ref: jnp.concatenate([ref[c] for c in range(nslab)], axis=1)
    l1, l4, l16 = l1_ref[...], wide(s4l), wide(s16l)
    mx = jnp.maximum(jnp.maximum(l1, l4), l16)
    e1, e4, e16 = jnp.exp(l1 - mx), jnp.exp(l4 - mx), jnp.exp(l16 - mx)
    ya = (e1 * o1_ref[...].astype(F32) + e4 * wide(s4o) + e16 * wide(s16o)) * (1.0 / (e1 + e4 + e16))

    u = cg_ref[...].astype(F32) * h_ref[...].astype(F32)
    last = cgp_ref.shape[0] - 1
    u_before = (cgp_ref[...].astype(F32) * hp_ref[...].astype(F32))[last:last + 1]
    u_after = (cgn_ref[...].astype(F32) * hn_ref[...].astype(F32))[0:1]
    u_before = u_before * (it > 0).astype(F32)
    u_after = u_after * (it < nts - 1).astype(F32)
    row = lax.broadcasted_iota(jnp.int32, u.shape, 0)
    u_prev = jnp.where(row == 0, u_before, pltpu.roll(u, 1, 0))
    u_next = jnp.where(row == tm - 1, u_after, pltpu.roll(u, tm - 1, 0))
    cw = cw_ref[...]
    yb = bg_ref[...].astype(F32) * (cw[0:1] * u_prev + cw[1:2] * u + cw[2:3] * u_next)

    out_ref[...] = x_ref[...] + _dot(ya.astype(BF16), wa_ref[...]) + _dot(yb.astype(BF16), wb_ref[...])


def _even_tail_kernel(*refs, tm, nts, nsub):
    mixer_in, cross_in, out_ref, scratch = refs[:17], refs[17:22], refs[22], refs[23:]
    x_mid = scratch[4]
    _even_out_kernel(*mixer_in, x_mid, *scratch[:4], tm=tm, nts=nts)
    _cross_kernel(x_mid, *cross_in, out_ref, nsub=nsub)


def _cross_specs(seq, n_mem, tm):
    nts = seq // tm
    return [_full((1, D_MODEL)),
            pl.BlockSpec((n_mem, D_MODEL), lambda i: (i // nts, 0)),
            pl.BlockSpec((n_mem, D_MODEL), lambda i: (i // nts, 1)),
            _full((D_MODEL, D_MODEL)),
            _full((D_MODEL, D_MODEL))]


def _even_tail(x, o1, l1, o4, l4, o16, l16, zc, conv_w, w_out, cross, batch, seq):
    g_cross, kv, wq, wo, n_mem = cross
    m = x.shape[0]
    tm = TM
    nts = seq // tm
    hb = 16
    nhb = m // hb
    kern = functools.partial(_even_tail_kernel, tm=tm, nts=nts, nsub=2)
    nat = lambda: pl.BlockSpec((tm, A_WIDTH), lambda i: (i, 0))
    dil = lambda d: pl.BlockSpec((None, d, tm // d, A_WIDTH), lambda i: (i // nts, 0, i % nts, 0))
    col = lambda c: pl.BlockSpec((tm, B_WIDTH), lambda i: (i, c))
    before = lambda c: pl.BlockSpec((hb, B_WIDTH), lambda i: (jnp.maximum(i * (tm // hb) - 1, 0), c))
    after = lambda c: pl.BlockSpec((hb, B_WIDTH), lambda i: (jnp.minimum((i + 1) * (tm // hb), nhb - 1), c))
    return pl.pallas_call(
        kern,
        grid=(m // tm,),
        in_specs=[nat(), nat(), dil(4), dil(4), dil(16), dil(16),
                  col(0), col(1), col(2), before(1), before(2), after(1), after(2),
                  _full((3, B_WIDTH)),
                  pl.BlockSpec((A_WIDTH, D_MODEL), lambda i: (0, 0)),
                  pl.BlockSpec((B_WIDTH, D_MODEL), lambda i: (1, 0)),
                  pl.BlockSpec((tm, D_MODEL), lambda i: (i, 0))] + _cross_specs(seq, n_mem, tm),
        out_specs=pl.BlockSpec((tm, D_MODEL), lambda i: (i, 0)),
        out_shape=jax.ShapeDtypeStruct((m, D_MODEL), F32),
        scratch_shapes=[pltpu.VMEM((A_WIDTH // LANES, tm, LANES), F32)] * 4 + [pltpu.VMEM((tm, D_MODEL), F32)],
        compiler_params=_cparams(("parallel",)),
        name="even_tail",
    )(o1, l1, o4, l4, o16, l16, zc, zc, zc, zc, zc, zc, zc, conv_w, w_out, w_out, x,
      g_cross.reshape(1, D_MODEL), kv, kv, wq, wo)


def _cross_kernel(x_ref, g_ref, k_ref, v_ref, wq_ref, wo_ref, out_ref, *, nsub):
    rows = x_ref.shape[0] // nsub
    g = g_ref[...]
    heads = [slice(h * CA_HEAD_DIM, (h + 1) * CA_HEAD_DIM) for h in range(CA_HEADS)]
    k = k_ref[...]
    v = v_ref[...]
    qs = []
    for i in range(nsub):
        xn = _rms(x_ref[i * rows:(i + 1) * rows, :], g).astype(BF16)
        qs.append((_dot(xn, wq_ref[...]) * (CA_HEAD_DIM ** -0.5)).astype(BF16))
    units = [(i, h) for i in range(nsub) for h in range(CA_HEADS)]
    scores, probs, outs = {}, {}, {}
    for step in range(len(units) + 2):
        if step < len(units):
            i, h = units[step]
            scores[step] = _dot_nt(qs[i][:, heads[h]], k[:, heads[h]])
        if 0 <= step - 1 < len(units):
            s = scores.pop(step - 1)
            e = jnp.exp(s - jnp.max(s, axis=-1, keepdims=True))
            probs[step - 1] = (e * (1.0 / jnp.sum(e, axis=-1, keepdims=True))).astype(BF16)
        if 0 <= step - 2 < len(units):
            i, h = units[step - 2]
            outs[i, h] = _dot(probs.pop(step - 2), v[:, heads[h]]).astype(BF16)
            if h == CA_HEADS - 1:
                o = jnp.concatenate([outs.pop((i, hh)) for hh in range(CA_HEADS)], axis=1)
                out_ref[i * rows:(i + 1) * rows, :] = x_ref[i * rows:(i + 1) * rows, :] + _dot(o, wo_ref[...])


def _ffn_kernel(x_ref, g_ref, wgu_ref, wd_ref, gf_ref, out_ref, *, final_norm, tf, nsub):
    rows = x_ref.shape[0] // nsub
    g = g_ref[...]
    accs = [x_ref[i * rows:(i + 1) * rows, :] for i in range(nsub)]
    xns = [_rms(x, g).astype(BF16) for x in accs]
    nf = D_FF // tf
    acts = None
    for f in range(nf + 1):
        new_acts = []
        if f < nf:
            for i in range(nsub):
                gate = _dot(xns[i], wgu_ref[:, f * tf:(f + 1) * tf])
                up = _dot(xns[i], wgu_ref[:, D_FF + f * tf:D_FF + (f + 1) * tf])
                new_acts.append((gate * _sigmoid(gate) * up).astype(BF16))
        if f > 0:
            for i in range(nsub):
                accs[i] = accs[i] + _dot(acts[i], wd_ref[(f - 1) * tf:f * tf, :])
        acts = new_acts
    for i in range(nsub):
        y = _rms(accs[i], gf_ref[...]) if final_norm else accs[i]
        out_ref[i * rows:(i + 1) * rows, :] = y


def _ffn(x, g, w_gu, w_down, g_final, final_norm):
    m = x.shape[0]
    tm = TM
    kern = functools.partial(_ffn_kernel, final_norm=final_norm, tf=TF_FFN, nsub=2)
    resident = lambda shape: pl.BlockSpec(shape, lambda i: (0, 0), pipeline_mode=pl.Buffered(1))
    return pl.pallas_call(
        kern,
        grid=(m // tm,),
        in_specs=[pl.BlockSpec((tm, D_MODEL), lambda i: (i, 0)),
                  _full((1, D_MODEL)),
                  resident(w_gu.shape),
                  resident(w_down.shape),
                  _full((1, D_MODEL))],
        out_specs=pl.BlockSpec((tm, D_MODEL), lambda i: (i, 0)),
        out_shape=jax.ShapeDtypeStruct((m, D_MODEL), F32),
        compiler_params=_cparams(("parallel",)),
        name="swiglu",
    )(x, g.reshape(1, D_MODEL), w_gu, w_down, g_final.reshape(1, D_MODEL))


def _head_sum(x, e_ref):
    x = x.astype(BF16)
    e = e_ref[...]
    width = e.shape[0]
    return jnp.concatenate([_dot(x[:, c * width:(c + 1) * width], e) for c in range(x.shape[1] // width)],
                           axis=1)


def _sigmoid(x):
    return 0.5 * jnp.tanh(0.5 * x) + 0.5


def _store_pairs(ref, val, *lead):
    for p in range(val.shape[1] // LANES):
        ref[(*lead, p)] = val[:, p * LANES:(p + 1) * LANES].astype(ref.dtype)


def _rwkv_pre_kernel(x_ref, xb_ref, xa_ref, g_ref, mu_ref, wr_ref, wk_ref, wv_ref, w1_ref, w2_ref, w0_ref,
                     a1_ref, a2_ref, a0_ref, g1_ref, g2_ref, kk_ref, ka_ref, rk_ref, e_ref,
                     sm_ref, r_out, v_out, kn_out, kd_out, ba_out, lw_out, gate_out, bonus_out, *, nts):
    it = pl.program_id(0) % nts
    g = g_ref[...]
    xn_bf = _rms(x_ref[...], g).astype(BF16)
    before = (_rms(xb_ref[...], g) * (it > 0).astype(F32)).astype(BF16)
    after = (_rms(xa_ref[...], g) * (it < nts - 1).astype(F32)).astype(BF16)
    xx_bf = _dot(sm_ref[...], jnp.concatenate([before, xn_bf, after], axis=0)).astype(BF16)
    mu = mu_ref[...].astype(BF16)
    mix = lambda i: xn_bf + xx_bf * mu[i:i + 1]

    tw = jnp.tanh(_dot(mix(1), w1_ref[...])).astype(BF16)
    ta = _dot(mix(4), a1_ref[...]).astype(BF16)
    tg = _sigmoid(_dot(mix(5), g1_ref[...])).astype(BF16)
    k_lin = _dot(mix(2), wk_ref[...])
    wls = [w0_ref[d:d + 1] + _dot(tw, w2_ref[d]) for d in range(2)]
    r = _dot(mix(0), wr_ref[...])
    for d in range(2):
        _store_pairs(lw_out, (-0.5 * DECAY_SCALE) * jnp.tanh(0.5 * wls[d]) - 0.5 * DECAY_SCALE, d)
    kn = k_lin * kk_ref[...]
    kn = kn * lax.rsqrt(jnp.maximum(_head_sum(kn * kn, e_ref), 1e-24))
    _store_pairs(kn_out, kn)
    als = [a0_ref[d:d + 1] + _dot(ta, a2_ref[d]) for d in range(2)]
    v = _dot(mix(3), wv_ref[...])
    hk = k_lin * (0.5 * ka_ref[...])
    k_base = k_lin - hk
    hn = 0.5 * kn
    kd_sum = jnp.zeros_like(k_lin)
    for d in range(2):
        t = jnp.tanh(0.5 * als[d])
        kd = k_base + hk * t
        kd_sum = kd_sum + kd
        _store_pairs(kd_out, kd, d)
        _store_pairs(ba_out, hn + hn * t, d)
    gate_out[...] = _dot(tg, g2_ref[...]).astype(BF16)
    _store_pairs(r_out, r)
    _store_pairs(v_out, v)
    bonus_out[...] = (_head_sum(r * kd_sum * rk_ref[...], e_ref) * v).astype(BF16)


def _rwkv_pre(x, g, p, batch, seq):
    m = x.shape[0]
    tm = TM_RW
    nts = seq // tm
    hb = 16
    nhb = m // hb
    kern = functools.partial(_rwkv_pre_kernel, nts=nts)
    row = lambda: pl.BlockSpec((tm, D_MODEL), lambda i: (i, 0))
    npair = D_MODEL // LANES
    pairs = lambda: pl.BlockSpec((npair, tm, LANES), lambda i: (0, i, 0))
    pairs2 = lambda: pl.BlockSpec((2, npair, tm, LANES), lambda i: (0, 0, i, 0))
    t = jnp.arange(tm)[:, None]
    c = jnp.arange(tm + 2 * hb)[None, :] - hb
    shift = (jnp.where(c == t, -1.0, 0.0) + jnp.where(jnp.abs(c - t) == 1, 0.5, 0.0)).astype(BF16)
    args = (x, x, x, g.reshape(1, D_MODEL), p['mu'], p['wr'], p['wk'], p['wv'], p['w1'], p['w2'], p['w0'],
            p['a1'], p['a2'], p['a0'], p['g1'], p['g2'], p['kk'], p['ka'], p['rk'], p['e'], shift)
    in_specs = [row(),
                pl.BlockSpec((hb, D_MODEL), lambda i: (jnp.maximum(i * (tm // hb) - 1, 0), 0)),
                pl.BlockSpec((hb, D_MODEL), lambda i: (jnp.minimum((i + 1) * (tm // hb), nhb - 1), 0))]
    in_specs += [_full(a.shape) for a in args[3:]]
    out_shape = (jax.ShapeDtypeStruct((npair, m, LANES), BF16),) * 3 + (
        jax.ShapeDtypeStruct((2, npair, m, LANES), BF16), jax.ShapeDtypeStruct((2, npair, m, LANES), BF16),
        jax.ShapeDtypeStruct((2, npair, m, LANES), F32),
        jax.ShapeDtypeStruct((m, D_MODEL), BF16), jax.ShapeDtypeStruct((m, D_MODEL), BF16))
    out_specs = (pairs(), pairs(), pairs(), pairs2(), pairs2(), pairs2(), row(), row())
    return pl.pallas_call(
        kern, grid=(m // tm,), in_specs=in_specs, out_specs=out_specs, out_shape=out_shape,
        compiler_params=_cparams(("parallel",)), name="rwkv_pre",
    )(*args)


def _bd(x, first):
    z = jnp.zeros_like(x)
    return jnp.concatenate([jnp.where(first, x, z), jnp.where(first, z, x)], axis=0)


def _compact(full, first):
    return jnp.where(first, full[:CHUNK], full[CHUNK:])


def _wkv_masks(sgn):
    c = CHUNK
    row = lax.broadcasted_iota(jnp.int32, (c, c), 0)
    col = lax.broadcasted_iota(jnp.int32, (c, c), 1)
    tri = jnp.where((row - col) * sgn >= 0, 1.0, 0.0).astype(BF16)
    prow = lax.broadcasted_iota(jnp.int32, (c, LANES), 0)
    pcol = lax.broadcasted_iota(jnp.int32, (c, LANES), 1) & (c - 1)
    dlt = (prow - pcol) * sgn
    strict = dlt > 0
    same = lambda sh: (prow >> sh) == (pcol >> sh)
    levels = []
    sh = 2
    while (1 << sh) < c:
        levels.append(jnp.where(strict & same(sh + 1) & jnp.logical_not(same(sh)), 1.0, 0.0))
        sh += 1
    in4 = same(2)
    return dict(tri=tri, strict=strict, incl=dlt >= 0, eye=jnp.where(prow == pcol, 1.0, 0.0), levels=levels,
                tri4=jnp.where(strict & in4, 1.0, 0.0),
                off2=jnp.where(in4 & (dlt == 2), 1.0, 0.0), off3=jnp.where(in4 & (dlt == 3), 1.0, 0.0),
                row1=sgn % c, row2=(2 * sgn) % c, lane1=(-sgn) % LANES, lane2=(-2 * sgn) % LANES)


def _inverse4(a_ab, mk):
    a = a_ab * mk['tri4']
    up1 = pltpu.roll(a, mk['row1'], 0)
    up2 = pltpu.roll(a, mk['row2'], 0)
    left1 = pltpu.roll(a, mk['lane1'], 1)
    left2 = pltpu.roll(a, mk['lane2'], 1)
    sq = left1 * up1 * mk['off2'] + (left1 * up2 + left2 * up1) * mk['off3']
    cube = pltpu.roll(sq, mk['lane1'], 1) * up2 * mk['off3']
    return mk['eye'] + a + sq + cube


def _wkv_local(chains, first):
    c = CHUNK
    for s in chains:
        hl = _dot(s['mk']['tri'], jnp.concatenate(_split(s['lw']), axis=1))
        s['cum'] = hl[:, :LANES] + hl[:, LANES:]
    yield
    for s in chains:
        cum, lw = s['cum'], s['lw']
        total = jnp.sum(lw, axis=0, keepdims=True)
        p_inv = jnp.exp(-cum)
        p_hat = jnp.exp(total - cum)
        s['at'] = (-s['kn'] * jnp.exp(cum - lw)).astype(BF16)
        s['rt'] = (s['r'] * jnp.exp(cum)).astype(BF16)
        s['bt'] = (s['ba'] * p_inv).astype(BF16)
        s['kt'] = (s['kd'] * p_inv).astype(BF16)
        s['bh'] = (s['ba'] * p_hat).astype(BF16)
        s['kh'] = (s['kd'] * p_hat).astype(BF16)
        s['v_bf'] = s['v'].astype(BF16)
        s['p_end'] = jnp.exp(total)
        s['lhs'] = jnp.concatenate([s['at'], s['rt']], axis=0)
    yield
    for s in chains:
        s['sbk'] = _dot_nt(s['lhs'], jnp.concatenate([_bd(s['bt'], first), _bd(s['kt'], first)], axis=0))
    yield
    for s in chains:
        mk = s['mk']
        sb, sk = s['sbk'][:, :LANES], s['sbk'][:, LANES:]
        s['a_ab'] = jnp.where(mk['strict'], sb[:c], 0.0)
        s['a_rb'] = jnp.where(mk['incl'], sb[c:], 0.0).astype(BF16)
        s['a_kk'] = jnp.concatenate([jnp.where(mk['strict'], sk[:c], 0.0),
                                     jnp.where(mk['incl'], sk[c:], 0.0)], axis=0).astype(BF16)
        s['inv'] = _inverse4(s['a_ab'], mk)
    yield
    for s in chains:
        s['av'] = _dot(s['a_kk'], _bd(s['v_bf'], first))
        s['y0'] = _compact(_dot_tn(s['v_bf'], s['kh']), first)
    yield
    for lvl in range(len(chains[0]['mk']['levels'])):
        for s in chains:
            s['inv_bf'] = s['inv'].astype(BF16)
            s['inner'] = _dot((s['a_ab'] * s['mk']['levels'][lvl]).astype(BF16), _bd(s['inv_bf'], first))
        yield
        for s in chains:
            s['inv'] = s['inv'] + _dot(s['inv_bf'], _bd(s['inner'].astype(BF16), first))
        yield
    for s in chains:
        rhs = jnp.concatenate([_bd(s['at'], first), _bd(s['av'][:c].astype(BF16), first)], axis=1)
        wu = _dot(s['inv'].astype(BF16), rhs)
        s['w1'] = wu[:, :LANES].astype(BF16)
        s['u0'] = wu[:, LANES:]
        s['o0'] = s['av'][c:]


def _wkv_seq(rows, ys, out_refs, first):
    c = CHUNK
    for row in rows:
        wrs = [_dot_nt(jnp.concatenate([s['w1'], s['rt']], axis=0), _bd(y.astype(BF16), first))
               for s, y in zip(row, ys)]
        yield
        us = [(wr[:c] + s['u0']).astype(BF16) for s, wr in zip(row, wrs)]
        for s, wr, u, (o_ref, p) in zip(row, wrs, us, out_refs):
            o_ref[p, s['sl'], :] = (wr[c:] + _dot(s['a_rb'], _bd(u, first)) + s['o0']).astype(o_ref.dtype)
        ys[:] = [s['p_end'] * y + _compact(_dot_tn(u, s['bh']), first) + s['y0'] for s, y, u in zip(row, ys, us)]
        yield


def _run_staggered(local_gens, make_seq, period):
    end = object()
    started, finished, tick = 0, 0, 0
    active, updates = [], []
    while finished < len(local_gens) or updates:
        if started < len(local_gens) and tick >= started * period:
            active.append(started)
            started += 1
        for g in list(active):
            if next(local_gens[g], end) is end:
                active.remove(g)
                finished += 1
                updates.append(make_seq(g))
        if updates and next(updates[0], end) is end:
            updates.pop(0)
        tick += 1


def _wkv_kernel(rf, vf, nf, kdf, baf, lwf, rb, vb, nb, kdb, bab, lwb, of_ref, ob_ref, state, *, ts, npairs):
    first = lax.broadcasted_iota(jnp.int32, (CHUNK, LANES), 1) < RWKV_HEAD

    @pl.when(pl.program_id(2) == 0)
    def _():
        state[...] = jnp.zeros_like(state)

    nchunk = ts // CHUNK
    names = ('r', 'v', 'kn', 'kd', 'ba', 'lw')
    dirs = ((_wkv_masks(1), (rf, vf, nf, kdf, baf, lwf), of_ref, list(range(nchunk))),
            (_wkv_masks(-1), (rb, vb, nb, kdb, bab, lwb), ob_ref, list(range(nchunk - 1, -1, -1))))
    scans = [(d, p) for d in range(2) for p in range(npairs)]
    out_refs = [(dirs[d][2], p) for d, p in scans]
    steps = []
    for k in range(nchunk):
        row = []
        for d, p in scans:
            mk, ins, _, order = dirs[d]
            sl = pl.ds(order[k] * CHUNK, CHUNK)
            chain = {n: ref[p, sl, :].astype(F32) for n, ref in zip(names, ins)}
            chain.update(mk=mk, sl=sl)
            row.append(chain)
        steps.append(row)
    ys = [state[d, p] for d, p in scans]
    groups = [steps[k:k + WKV_GROUP] for k in range(0, nchunk, WKV_GROUP)]
    _run_staggered([_wkv_local([s for row in rows for s in row], first) for rows in groups],
                   lambda g: _wkv_seq(groups[g], ys, out_refs, first), WKV_STAGGER)
    for (d, p), y in zip(scans, ys):
        state[d, p] = y


def _wkv(r, v, kn, kd, ba, lw, batch, seq):
    ts, npairs = TS_WKV, PAIRS_WKV
    nt = seq // ts
    npair = D_MODEL // LANES
    m = batch * seq
    fwd = lambda b, t: b * nt + t
    bwd = lambda b, t: b * nt + nt - 1 - t
    s3 = lambda at: pl.BlockSpec((npairs, ts, LANES), lambda b, g, t: (g, at(b, t), 0))
    s4 = lambda d, at: pl.BlockSpec((None, npairs, ts, LANES), lambda b, g, t: (d, g, at(b, t), 0))
    return pl.pallas_call(
        functools.partial(_wkv_kernel, ts=ts, npairs=npairs),
        grid=(batch, npair // npairs, nt),
        in_specs=[s3(fwd), s3(fwd), s3(fwd), s4(0, fwd), s4(0, fwd), s4(0, fwd),
                  s3(bwd), s3(bwd), s3(bwd), s4(1, bwd), s4(1, bwd), s4(1, bwd)],
        out_specs=(s3(fwd), s3(bwd)),
        out_shape=(jax.ShapeDtypeStruct((npair, m, LANES), BF16),) * 2,
        scratch_shapes=[pltpu.VMEM((2, npairs, CHUNK, LANES), F32)],
        compiler_params=_cparams(("parallel", "parallel", "arbitrary")),
        name="wkv",
    )(r, v, kn, kd, ba, lw, r, v, kn, kd, ba, lw)


def _rwkv_post_kernel(of_ref, ob_ref, bonus_ref, gate_ref, lw_ref, lb_ref, e_ref, wo_ref, x_ref, out_ref):
    y = jnp.concatenate([of_ref[p].astype(F32) + ob_ref[p].astype(F32) for p in range(of_ref.shape[0])], axis=1)
    mean = _head_sum(y, e_ref) * (1.0 / RWKV_HEAD)
    yc = y - mean
    var = _head_sum(yc * yc, e_ref) * (1.0 / RWKV_HEAD)
    yn = yc * lax.rsqrt(var + GN_EPS)
    y2 = yn * lw_ref[...] + lb_ref[...] + bonus_ref[...].astype(F32)
    out_ref[...] = x_ref[...] + _dot((y2 * gate_ref[...].astype(F32)).astype(BF16), wo_ref[...])


def _odd_tail_kernel(*refs, nsub):
    mixer_in, cross_in, out_ref, x_mid = refs[:9], refs[9:14], refs[14], refs[15]
    _rwkv_post_kernel(*mixer_in, x_mid)
    _cross_kernel(x_mid, *cross_in, out_ref, nsub=nsub)


def _odd_tail(x, o_f, o_b, bonus, gate, p, cross, seq):
    g_cross, kv, wq, wo, n_mem = cross
    m = x.shape[0]
    tm = TM
    row = lambda: pl.BlockSpec((tm, D_MODEL), lambda i: (i, 0))
    pairs = lambda: pl.BlockSpec((D_MODEL // LANES, tm, LANES), lambda i: (0, i, 0))
    return pl.pallas_call(
        functools.partial(_odd_tail_kernel, nsub=2),
        grid=(m // tm,),
        in_specs=[pairs(), pairs(), row(), row(), _full((1, D_MODEL)), _full((1, D_MODEL)),
                  _full(p['e'].shape), _full((D_MODEL, D_MODEL)), row()] + _cross_specs(seq, n_mem, tm),
        out_specs=row(),
        out_shape=jax.ShapeDtypeStruct((m, D_MODEL), F32),
        scratch_shapes=[pltpu.VMEM((tm, D_MODEL), F32)],
        compiler_params=_cparams(("parallel",)),
        name="odd_tail",
    )(o_f, o_b, bonus, gate, p['lnx_w'], p['lnx_b'], p['e'], p['wo'], x,
      g_cross.reshape(1, D_MODEL), kv, kv, wq, wo)


def _prep_odd(o, rw_mu, rw_wr, rw_wk, rw_wv, rw_wo, rw_w0, rw_w1, rw_w2, rw_a0, rw_a1, rw_a2, rw_g1, rw_g2,
              rw_kk, rw_ka, rw_rk, rw_lnx_w, rw_lnx_b):
    bf = lambda a: a.astype(BF16)

    def pad_dir(w):
        z = jnp.zeros_like(w[0])
        return jnp.stack([jnp.concatenate([w[0], z], axis=0), jnp.concatenate([z, w[1]], axis=0)])

    lane = jnp.arange(MXU_WIDTH) // RWKV_HEAD
    return dict(
        mu=rw_mu[o], wr=bf(rw_wr[o]), wk=bf(rw_wk[o]), wv=bf(rw_wv[o]), wo=bf(rw_wo[o]),
        w0=rw_w0[o], w1=bf(jnp.concatenate([rw_w1[o, 0], rw_w1[o, 1]], axis=1)), w2=bf(pad_dir(rw_w2[o])),
        a0=rw_a0[o], a1=bf(jnp.concatenate([rw_a1[o, 0], rw_a1[o, 1]], axis=1)), a2=bf(pad_dir(rw_a2[o])),
        g1=bf(rw_g1[o]), g2=bf(rw_g2[o]),
        kk=rw_kk[o].reshape(1, D_MODEL), ka=rw_ka[o].reshape(1, D_MODEL), rk=rw_rk[o].reshape(1, D_MODEL),
        lnx_w=rw_lnx_w[o].reshape(1, D_MODEL), lnx_b=rw_lnx_b[o].reshape(1, D_MODEL),
        e=(lane[:, None] == lane[None, :]).astype(BF16))


def _even_layer(x, g, w_in, w_out, conv_w, cos, sin, cross, batch, seq):
    z1, z4, z16, zc = _even_proj(x, g, w_in, cos, sin, batch, seq)
    m = x.shape[0]
    o1, l1 = _dilated_branch(z1.reshape(3, batch, seq, A_WIDTH), seq)
    o4, l4 = _dilated_branch(z4.reshape(3, batch * 4, seq // 4, A_WIDTH), seq // 4)
    o16, l16 = _dilated_branch(z16.reshape(3, batch * 16, seq // 16, A_WIDTH), seq // 16)
    o1, l1 = o1.reshape(m, A_WIDTH), l1.reshape(m, A_WIDTH)
    d4 = lambda a: a.reshape(batch, 4, seq // 4, A_WIDTH)
    d16 = lambda a: a.reshape(batch, 16, seq // 16, A_WIDTH)
    return _even_tail(x, o1, l1, d4(o4), d4(l4), d16(o16), d16(l16), zc, conv_w, w_out, cross, batch, seq)


def _odd_layer(x, g, p, cross, batch, seq):
    r, v, kn, kd, ba, lw, gate, bonus = _rwkv_pre(x, g, p, batch, seq)
    o_f, o_b = _wkv(r, v, kn, kd, ba, lw, batch, seq)
    return _odd_tail(x, o_f, o_b, bonus, gate, p, cross, seq)


def _trunk(x, mem, w):
    batch, seq, _ = x.shape
    n_mem = mem.shape[1]
    x = x.reshape(batch * seq, D_MODEL)
    mem = mem.reshape(batch * n_mem, D_MODEL)
    cos, sin = _rope_tables(seq)
    for l in range(DEPTH):
        kv = _norm_matmul(mem, w['norm_mem'][l], w['ca_wkv'][l], TM, 2 * D_MODEL, BF16)
        cross = (w['norm_cross'][l], kv, w['ca_wq'][l], w['ca_wo'][l], n_mem)
        if l % 2 == 0:
            e = l // 2
            x = _even_layer(x, w['norm_mix'][l], w['ab_w_in'][e], w['ab_w_out'][e], w['ab_conv'][e],
                            cos, sin, cross, batch, seq)
        else:
            x = _odd_layer(x, w['norm_mix'][l], w['odd'][l // 2], cross, batch, seq)
        x = _ffn(x, w['norm_ffn'][l], w['ffn_wgu'][l], w['ffn_wdown'][l], w['norm_final'], l == DEPTH - 1)
    return x.reshape(batch, seq, D_MODEL)


def kernel(x_prompt, x_sample, mem_prompt, mem_sample, norm_mix, norm_cross, norm_mem, norm_ffn, norm_final,
           ab_w_in, ab_w_out, ab_conv, rw_mu, rw_wr, rw_wk, rw_wv, rw_wo, rw_w0, rw_w1, rw_w2, rw_a0, rw_a1,
           rw_a2, rw_g1, rw_g2, rw_kk, rw_ka, rw_rk, rw_lnx_w, rw_lnx_b, ca_wq, ca_wkv, ca_wo, ffn_wgu,
           ffn_wdown):
    bf = lambda a: a.astype(BF16)
    w = dict(norm_mix=norm_mix, norm_cross=norm_cross, norm_mem=norm_mem, norm_ffn=norm_ffn,
             norm_final=norm_final, ab_w_in=bf(ab_w_in), ab_w_out=bf(ab_w_out), ab_conv=ab_conv,
             ca_wq=bf(ca_wq), ca_wkv=bf(ca_wkv), ca_wo=bf(ca_wo), ffn_wgu=bf(ffn_wgu), ffn_wdown=bf(ffn_wdown),
             odd=[_prep_odd(o, rw_mu, rw_wr, rw_wk, rw_wv, rw_wo, rw_w0, rw_w1, rw_w2, rw_a0, rw_a1, rw_a2,
                            rw_g1, rw_g2, rw_kk, rw_ka, rw_rk, rw_lnx_w, rw_lnx_b)
                  for o in range(rw_mu.shape[0])])
    return _trunk(x_prompt, mem_prompt, w), _trunk(x_sample, mem_sample, w)
```

```python
import functools

import jax
import jax.numpy as jnp
from jax import lax
from jax.experimental import pallas as pl
from jax.experimental.pallas import tpu as pltpu

F32 = jnp.float32
BF16 = jnp.bfloat16

D_MODEL = 1024
DEPTH = 4
HEAD_DIM = 64
A_WIDTH = 512
B_WIDTH = 512
BAND = 64
ROPE_THETA = 500000.0
ROPE_DIM = 16
RWKV_HEAD = 64
GN_EPS = 64e-5
DECAY_SCALE = 0.6065306597126334
CA_HEADS = 4
CA_HEAD_DIM = 256
D_FF = 2816
RMS_EPS = 1e-6
NEG_INF = -1e30
CHUNK = 64
LANES = 128
MXU_WIDTH = 256

VMEM_LIMIT = 48 * 1024 * 1024

TM = 512
TM_RW = 256
TB_DIL = 1024
TQ_DIL = 128
DIL_LAG = 2
TF_FFN = 256
TS_WKV = 1024
PAIRS_WKV = 4
WKV_GROUP = 2
WKV_STAGGER = 10


def _cparams(sem):
    return pltpu.CompilerParams(dimension_semantics=sem, vmem_limit_bytes=VMEM_LIMIT)


def _full(shape):
    n = len(shape)
    return pl.BlockSpec(shape, lambda *_: (0,) * n)


def _rms(x, g):
    ms = jnp.mean(x * x, axis=-1, keepdims=True)
    return x * lax.rsqrt(ms + RMS_EPS) * g


def _dot(a, b):
    return jnp.dot(a, b, preferred_element_type=F32)


def _dot_nt(a, b):
    return lax.dot_general(a, b, (((1,), (1,)), ((), ())), preferred_element_type=F32)


def _dot_tn(a, b):
    return lax.dot_general(a, b, (((0,), (0,)), ((), ())), preferred_element_type=F32)


def _split(x):
    hi = x.astype(BF16)
    lo = (x - hi.astype(F32)).astype(BF16)
    return hi, lo


def _norm_matmul_kernel(x_ref, g_ref, w_ref, o_ref, xn_ref):
    @pl.when(pl.program_id(1) == 0)
    def _():
        xn_ref[...] = _rms(x_ref[...], g_ref[...]).astype(BF16)

    o_ref[...] = _dot(xn_ref[...], w_ref[...]).astype(o_ref.dtype)


def _norm_matmul(x, g, w, tm, tn, out_dtype):
    m, k = x.shape
    n = w.shape[1]
    return pl.pallas_call(
        _norm_matmul_kernel,
        grid=(m // tm, n // tn),
        in_specs=[pl.BlockSpec((tm, k), lambda i, j: (i, 0)),
                  pl.BlockSpec((1, k), lambda i, j: (0, 0)),
                  pl.BlockSpec((k, tn), lambda i, j: (0, j))],
        out_specs=pl.BlockSpec((tm, tn), lambda i, j: (i, j)),
        out_shape=jax.ShapeDtypeStruct((m, n), out_dtype),
        scratch_shapes=[pltpu.VMEM((tm, k), BF16)],
        compiler_params=_cparams(("parallel", "arbitrary")),
        name="norm_matmul",
    )(x, g.reshape(1, k), w)


def _even_proj_kernel(x_ref, g_ref, w_ref, cos_ref, sin_ref, p4_ref, p16_ref,
                      o1_ref, o4_ref, o16_ref, oc_ref, *, tm):
    xn = _rms(x_ref[...], g_ref[...]).astype(BF16)
    reps = A_WIDTH // LANES
    cos = jnp.concatenate([cos_ref[...]] * reps, axis=1)
    sin = jnp.concatenate([sin_ref[...]] * reps, axis=1)
    lane = lax.broadcasted_iota(jnp.int32, (tm, A_WIDTH), 1) & (HEAD_DIM - 1)
    half = ROPE_DIM // 2
    nparts = 3
    acc = None
    for j in range(nparts + 1):
        nxt = _dot(xn, w_ref[:, j * A_WIDTH:(j + 1) * A_WIDTH]) if j < nparts else None
        if j == nparts:
            oc_ref[...] = _dot(xn, w_ref[:, 3 * A_WIDTH:]).astype(BF16)
        if j > 0:
            a, part = acc, j - 1
            if part < 2:
                partner = jnp.where(lane < half, pltpu.roll(a, A_WIDTH - half, 1), pltpu.roll(a, half, 1))
                a = a * cos + partner * sin
                if part == 0:
                    a = a * (HEAD_DIM ** -0.5)
            a = a.astype(BF16)
            o1_ref[part] = a
            for d, p_ref, o_ref in ((4, p4_ref, o4_ref), (16, p16_ref, o16_ref)):
                perm = _dot(p_ref[...], a).astype(BF16)
                for r in range(d):
                    o_ref[part, r] = perm[r * (tm // d):(r + 1) * (tm // d)]
        acc = nxt


def _dedilate_perm(tm, d):
    i = jnp.arange(tm)
    src = (i % (tm // d)) * d + i // (tm // d)
    return (src[:, None] == jnp.arange(tm)[None, :]).astype(BF16)


def _even_proj(x, g, w_in, cos, sin, batch, seq):
    m = x.shape[0]
    tm = TM
    nts = seq // tm
    kern = functools.partial(_even_proj_kernel, tm=tm)
    out_shape = (jax.ShapeDtypeStruct((3, m, A_WIDTH), BF16),
                 jax.ShapeDtypeStruct((3, batch, 4, seq // 4, A_WIDTH), BF16),
                 jax.ShapeDtypeStruct((3, batch, 16, seq // 16, A_WIDTH), BF16),
                 jax.ShapeDtypeStruct((m, 3 * B_WIDTH), BF16))
    return pl.pallas_call(
        kern,
        grid=(m // tm,),
        in_specs=[pl.BlockSpec((tm, D_MODEL), lambda i: (i, 0)),
                  _full((1, D_MODEL)),
                  _full(w_in.shape),
                  pl.BlockSpec((tm, LANES), lambda i: (i % nts, 0)),
                  pl.BlockSpec((tm, LANES), lambda i: (i % nts, 0)),
                  _full((tm, tm)), _full((tm, tm))],
        out_specs=(pl.BlockSpec((3, tm, A_WIDTH), lambda i: (0, i, 0)),
                   pl.BlockSpec((3, None, 4, tm // 4, A_WIDTH), lambda i: (0, i // nts, 0, i % nts, 0)),
                   pl.BlockSpec((3, None, 16, tm // 16, A_WIDTH), lambda i: (0, i // nts, 0, i % nts, 0)),
                   pl.BlockSpec((tm, 3 * B_WIDTH), lambda i: (i, 0))),
        out_shape=out_shape,
        compiler_params=_cparams(("parallel",)),
        name="even_proj",
    )(x, g.reshape(1, D_MODEL), w_in, cos, sin, _dedilate_perm(tm, 4), _dedilate_perm(tm, 16))


def _rope_tables(seq):
    half = ROPE_DIM // 2
    inv = ROPE_THETA ** (-2.0 * jnp.arange(half, dtype=F32) / ROPE_DIM)
    ang = jnp.arange(seq, dtype=F32)[:, None] * inv[None, :]
    cos, sin = jnp.cos(ang), jnp.sin(ang)
    rest = HEAD_DIM - ROPE_DIM
    cos_h = jnp.concatenate([cos, cos, jnp.ones((seq, rest), F32)], axis=1)
    sin_h = jnp.concatenate([-sin, sin, jnp.zeros((seq, rest), F32)], axis=1)
    reps = LANES // HEAD_DIM
    return jnp.tile(cos_h, (1, reps)), jnp.tile(sin_h, (1, reps))


def _dil_kernel(q_ref, kp_ref, kc_ref, kn_ref, vp_ref, vc_ref, vn_ref, o_ref, lse_ref, *, gs, tb, tq, length):
    qi = pl.program_id(1)
    span = tq + 2 * BAND
    ks = [jnp.concatenate([kp_ref[g], kc_ref[g], kn_ref[g]], axis=0) for g in range(gs)]
    vs = [jnp.concatenate([vp_ref[g], vc_ref[g], vn_ref[g]], axis=0) for g in range(gs)]
    row = lax.broadcasted_iota(jnp.int32, (tq, span), 0)
    col = lax.broadcasted_iota(jnp.int32, (tq, span), 1)
    rel = col - BAND - row
    band = (rel <= BAND) & (rel >= -BAND)
    first = lax.broadcasted_iota(jnp.int32, (tq, LANES), 1) < HEAD_DIM
    units = [(g, s_idx, p, hh) for g in range(gs) for s_idx in range(tb // tq)
             for p in range(A_WIDTH // LANES) for hh in range(2)]
    lag = DIL_LAG
    scores, maxes, probs, results = {}, {}, {}, {}
    for n in range(len(units) + 3 * lag):
        if n < len(units):
            g, s_idx, p, hh = units[n]
            sl = slice(p * LANES, (p + 1) * LANES)
            qp = q_ref[g, s_idx * tq:(s_idx + 1) * tq, sl]
            sel = first if hh == 0 else jnp.logical_not(first)
            key_pos = qi * tb + s_idx * tq - BAND + col
            valid = band & (key_pos >= 0) & (key_pos < length)
            s = _dot_nt(jnp.where(sel, qp, jnp.zeros_like(qp)), ks[g][s_idx * tq:s_idx * tq + span, sl])
            scores[n] = jnp.where(valid, s, NEG_INF)
        m = n - lag
        if 0 <= m < len(units):
            maxes[m] = jnp.max(scores[m], axis=-1, keepdims=True)
        m = n - 2 * lag
        if 0 <= m < len(units):
            mx = maxes.pop(m)
            e = jnp.exp(scores.pop(m) - mx)
            den = jnp.sum(e, axis=-1, keepdims=True)
            probs[m] = (e.astype(BF16), den, mx)
        m = n - 3 * lag
        if 0 <= m < len(units):
            g, s_idx, p, hh = units[m]
            sl = slice(p * LANES, (p + 1) * LANES)
            e, den, mx = probs.pop(m)
            results[hh] = (_dot(e, vs[g][s_idx * tq:s_idx * tq + span, sl]) * (1.0 / den), mx + jnp.log(den))
            if hh == 1:
                rows = slice(s_idx * tq, (s_idx + 1) * tq)
                o_ref[g, rows, sl] = jnp.where(first, results[0][0], results[1][0]).astype(BF16)
                lse_ref[g, rows, sl] = jnp.where(first, results[0][1], results[1][1])


def _dilated_branch(zd, length):
    g = zd.shape[1]
    tb = min(TB_DIL, length)
    tq = min(TQ_DIL, length)
    gs = TB_DIL // tb
    nb = length // BAND
    r = tb // BAND
    kern = functools.partial(_dil_kernel, gs=gs, tb=tb, tq=tq, length=length)

    def cur(which):
        return pl.BlockSpec((None, gs, tb, A_WIDTH), lambda b, i: (which, b, i, 0))

    def prev(which):
        return pl.BlockSpec((None, gs, BAND, A_WIDTH), lambda b, i: (which, b, jnp.maximum(i * r - 1, 0), 0))

    def nxt(which):
        return pl.BlockSpec((None, gs, BAND, A_WIDTH), lambda b, i: (which, b, jnp.minimum((i + 1) * r, nb - 1), 0))

    return pl.pallas_call(
        kern,
        grid=(g // gs, length // tb),
        in_specs=[cur(0), prev(1), cur(1), nxt(1), prev(2), cur(2), nxt(2)],
        out_specs=(pl.BlockSpec((gs, tb, A_WIDTH), lambda b, i: (b, i, 0)),
                   pl.BlockSpec((gs, tb, A_WIDTH), lambda b, i: (b, i, 0))),
        out_shape=(jax.ShapeDtypeStruct((g, length, A_WIDTH), BF16),
                   jax.ShapeDtypeStruct((g, length, A_WIDTH), F32)),
        compiler_params=_cparams(("parallel", "parallel")),
        name="dilated_attn",
    )(zd, zd, zd, zd, zd, zd, zd)


def _even_out_kernel(o1_ref, l1_ref, o4_ref, l4_ref, o16_ref, l16_ref, bg_ref, cg_ref, h_ref,
                     cgp_ref, hp_ref, cgn_ref, hn_ref, cw_ref, wa_ref, wb_ref, x_ref, out_ref,
                     s4o, s4l, s16o, s16l, *, tm, nts):
    it = pl.program_id(0) % nts
    nslab = A_WIDTH // LANES
    for c in range(nslab):
        sl = slice(c * LANES, (c + 1) * LANES)
        for r in range(4):
            s4o[c, pl.ds(r, tm // 4, stride=4), :] = o4_ref[r, :, sl].astype(F32)
            s4l[c, pl.ds(r, tm // 4, stride=4), :] = l4_ref[r, :, sl]
        for r in range(16):
            s16o[c, pl.ds(r, tm // 16, stride=16), :] = o16_ref[r, :, sl].astype(F32)
            s16l[c, pl.ds(r, tm // 16, stride=16), :] = l16_ref[r, :, sl]
    wide = lambda ref: jnp.concatenate([ref[c] for c in range(nslab)], axis=1)
    l1, l4, l16 = l1_ref[...], wide(s4l), wide(s16l)
    mx = jnp.maximum(jnp.maximum(l1, l4), l16)
    e1, e4, e16 = jnp.exp(l1 - mx), jnp.exp(l4 - mx), jnp.exp(l16 - mx)
    ya = (e1 * o1_ref[...].astype(F32) + e4 * wide(s4o) + e16 * wide(s16o)) * (1.0 / (e1 + e4 + e16))

    u = cg_ref[...].astype(F32) * h_ref[...].astype(F32)
    last = cgp_ref.shape[0] - 1
    u_before = (cgp_ref[...].astype(F32) * hp_ref[...].astype(F32))[last:last + 1]
    u_after = (cgn_ref[...].astype(F32) * hn_ref[...].astype(F32))[0:1]
    u_before = u_before * (it > 0).astype(F32)
    u_after = u_after * (it < nts - 1).astype(F32)
    row = lax.broadcasted_iota(jnp.int32, u.shape, 0)
    u_prev = jnp.where(row == 0, u_before, pltpu.roll(u, 1, 0))
    u_next = jnp.where(row == tm - 1, u_after, pltpu.roll(u, tm - 1, 0))
    cw = cw_ref[...]
    yb = bg_ref[...].astype(F32) * (cw[0:1] * u_prev + cw[1:2] * u + cw[2:3] * u_next)

    out_ref[...] = x_ref[...] + _dot(ya.astype(BF16), wa_ref[...]) + _dot(yb.astype(BF16), wb_ref[...])


def _even_tail_kernel(*refs, tm, nts, nsub):
    mixer_in, cross_in, out_ref, scratch = refs[:17], refs[17:22], refs[22], refs[23:]
    x_mid = scratch[4]
    _even_out_kernel(*mixer_in, x_mid, *scratch[:4], tm=tm, nts=nts)
    _cross_kernel(x_mid, *cross_in, out_ref, nsub=nsub)


def _cross_specs(seq, n_mem, tm):
    nts = seq // tm
    return [_full((1, D_MODEL)),
            pl.BlockSpec((n_mem, D_MODEL), lambda i: (i // nts, 0)),
            pl.BlockSpec((n_mem, D_MODEL), lambda i: (i // nts, 1)),
            _full((D_MODEL, D_MODEL)),
            _full((D_MODEL, D_MODEL))]


def _even_tail(x, o1, l1, o4, l4, o16, l16, zc, conv_w, w_out, cross, batch, seq):
    g_cross, kv, wq, wo, n_mem = cross
    m = x.shape[0]
    tm = TM
    nts = seq // tm
    hb = 16
    nhb = m // hb
    kern = functools.partial(_even_tail_kernel, tm=tm, nts=nts, nsub=2)
    nat = lambda: pl.BlockSpec((tm, A_WIDTH), lambda i: (i, 0))
    dil = lambda d: pl.BlockSpec((None, d, tm // d, A_WIDTH), lambda i: (i // nts, 0, i % nts, 0))
    col = lambda c: pl.BlockSpec((tm, B_WIDTH), lambda i: (i, c))
    before = lambda c: pl.BlockSpec((hb, B_WIDTH), lambda i: (jnp.maximum(i * (tm // hb) - 1, 0), c))
    after = lambda c: pl.BlockSpec((hb, B_WIDTH), lambda i: (jnp.minimum((i + 1) * (tm // hb), nhb - 1), c))
    return pl.pallas_call(
        kern,
        grid=(m // tm,),
        in_specs=[nat(), nat(), dil(4), dil(4), dil(16), dil(16),
                  col(0), col(1), col(2), before(1), before(2), after(1), after(2),
                  _full((3, B_WIDTH)),
                  pl.BlockSpec((A_WIDTH, D_MODEL), lambda i: (0, 0)),
                  pl.BlockSpec((B_WIDTH, D_MODEL), lambda i: (1, 0)),
                  pl.BlockSpec((tm, D_MODEL), lambda i: (i, 0))] + _cross_specs(seq, n_mem, tm),
        out_specs=pl.BlockSpec((tm, D_MODEL), lambda i: (i, 0)),
        out_shape=jax.ShapeDtypeStruct((m, D_MODEL), F32),
        scratch_shapes=[pltpu.VMEM((A_WIDTH // LANES, tm, LANES), F32)] * 4 + [pltpu.VMEM((tm, D_MODEL), F32)],
        compiler_params=_cparams(("parallel",)),
        name="even_tail",
    )(o1, l1, o4, l4, o16, l16, zc, zc, zc, zc, zc, zc, zc, conv_w, w_out, w_out, x,
      g_cross.reshape(1, D_MODEL), kv, kv, wq, wo)


def _cross_kernel(x_ref, g_ref, k_ref, v_ref, wq_ref, wo_ref, out_ref, *, nsub):
    rows = x_ref.shape[0] // nsub
    g = g_ref[...]
    heads = [slice(h * CA_HEAD_DIM, (h + 1) * CA_HEAD_DIM) for h in range(CA_HEADS)]
    k = k_ref[...]
    v = v_ref[...]
    qs = []
    for i in range(nsub):
        xn = _rms(x_ref[i * rows:(i + 1) * rows, :], g).astype(BF16)
        qs.append((_dot(xn, wq_ref[...]) * (CA_HEAD_DIM ** -0.5)).astype(BF16))
    units = [(i, h) for i in range(nsub) for h in range(CA_HEADS)]
    scores, probs, outs = {}, {}, {}
    for step in range(len(units) + 2):
        if step < len(units):
            i, h = units[step]
            scores[step] = _dot_nt(qs[i][:, heads[h]], k[:, heads[h]])
        if 0 <= step - 1 < len(units):
            s = scores.pop(step - 1)
            e = jnp.exp(s - jnp.max(s, axis=-1, keepdims=True))
            probs[step - 1] = (e * (1.0 / jnp.sum(e, axis=-1, keepdims=True))).astype(BF16)
        if 0 <= step - 2 < len(units):
            i, h = units[step - 2]
            outs[i, h] = _dot(probs.pop(step - 2), v[:, heads[h]]).astype(BF16)
            if h == CA_HEADS - 1:
                o = jnp.concatenate([outs.pop((i, hh)) for hh in range(CA_HEADS)], axis=1)
                out_ref[i * rows:(i + 1) * rows, :] = x_ref[i * rows:(i + 1) * rows, :] + _dot(o, wo_ref[...])


def _ffn_kernel(x_ref, g_ref, wgu_ref, wd_ref, gf_ref, out_ref, *, final_norm, tf, nsub):
    rows = x_ref.shape[0] // nsub
    g = g_ref[...]
    accs = [x_ref[i * rows:(i + 1) * rows, :] for i in range(nsub)]
    xns = [_rms(x, g).astype(BF16) for x in accs]
    nf = D_FF // tf
    acts = None
    for f in range(nf + 1):
        new_acts = []
        if f < nf:
            for i in range(nsub):
                gate = _dot(xns[i], wgu_ref[:, f * tf:(f + 1) * tf])
                up = _dot(xns[i], wgu_ref[:, D_FF + f * tf:D_FF + (f + 1) * tf])
                new_acts.append((gate * _sigmoid(gate) * up).astype(BF16))
        if f > 0:
            for i in range(nsub):
                accs[i] = accs[i] + _dot(acts[i], wd_ref[(f - 1) * tf:f * tf, :])
        acts = new_acts
    for i in range(nsub):
        y = _rms(accs[i], gf_ref[...]) if final_norm else accs[i]
        out_ref[i * rows:(i + 1) * rows, :] = y


def _ffn(x, g, w_gu, w_down, g_final, final_norm):
    m = x.shape[0]
    tm = TM
    kern = functools.partial(_ffn_kernel, final_norm=final_norm, tf=TF_FFN, nsub=2)
    resident = lambda shape: pl.BlockSpec(shape, lambda i: (0, 0), pipeline_mode=pl.Buffered(1))
    return pl.pallas_call(
        kern,
        grid=(m // tm,),
        in_specs=[pl.BlockSpec((tm, D_MODEL), lambda i: (i, 0)),
                  _full((1, D_MODEL)),
                  resident(w_gu.shape),
                  resident(w_down.shape),
                  _full((1, D_MODEL))],
        out_specs=pl.BlockSpec((tm, D_MODEL), lambda i: (i, 0)),
        out_shape=jax.ShapeDtypeStruct((m, D_MODEL), F32),
        compiler_params=_cparams(("parallel",)),
        name="swiglu",
    )(x, g.reshape(1, D_MODEL), w_gu, w_down, g_final.reshape(1, D_MODEL))


def _head_sum(x, e_ref):
    x = x.astype(BF16)
    e = e_ref[...]
    width = e.shape[0]
    return jnp.concatenate([_dot(x[:, c * width:(c + 1) * width], e) for c in range(x.shape[1] // width)],
                           axis=1)


def _sigmoid(x):
    return 0.5 * jnp.tanh(0.5 * x) + 0.5


def _store_pairs(ref, val, *lead):
    for p in range(val.shape[1] // LANES):
        ref[(*lead, p)] = val[:, p * LANES:(p + 1) * LANES].astype(ref.dtype)


def _rwkv_pre_kernel(x_ref, xb_ref, xa_ref, g_ref, mu_ref, wr_ref, wk_ref, wv_ref, w1_ref, w2_ref, w0_ref,
                     a1_ref, a2_ref, a0_ref, g1_ref, g2_ref, kk_ref, ka_ref, rk_ref, e_ref,
                     sm_ref, r_out, v_out, kn_out, kd_out, ba_out, lw_out, gate_out, bonus_out, *, nts):
    it = pl.program_id(0) % nts
    g = g_ref[...]
    xn_bf = _rms(x_ref[...], g).astype(BF16)
    before = (_rms(xb_ref[...], g) * (it > 0).astype(F32)).astype(BF16)
    after = (_rms(xa_ref[...], g) * (it < nts - 1).astype(F32)).astype(BF16)
    xx_bf = _dot(sm_ref[...], jnp.concatenate([before, xn_bf, after], axis=0)).astype(BF16)
    mu = mu_ref[...].astype(BF16)
    mix = lambda i: xn_bf + xx_bf * mu[i:i + 1]

    tw = jnp.tanh(_dot(mix(1), w1_ref[...])).astype(BF16)
    ta = _dot(mix(4), a1_ref[...]).astype(BF16)
    tg = _sigmoid(_dot(mix(5), g1_ref[...])).astype(BF16)
    k_lin = _dot(mix(2), wk_ref[...])
    wls = [w0_ref[d:d + 1] + _dot(tw, w2_ref[d]) for d in range(2)]
    r = _dot(mix(0), wr_ref[...])
    for d in range(2):
        _store_pairs(lw_out, (-0.5 * DECAY_SCALE) * jnp.tanh(0.5 * wls[d]) - 0.5 * DECAY_SCALE, d)
    kn = k_lin * kk_ref[...]
    kn = kn * lax.rsqrt(jnp.maximum(_head_sum(kn * kn, e_ref), 1e-24))
    _store_pairs(kn_out, kn)
    als = [a0_ref[d:d + 1] + _dot(ta, a2_ref[d]) for d in range(2)]
    v = _dot(mix(3), wv_ref[...])
    hk = k_lin * (0.5 * ka_ref[...])
    k_base = k_lin - hk
    hn = 0.5 * kn
    kd_sum = jnp.zeros_like(k_lin)
    for d in range(2):
        t = jnp.tanh(0.5 * als[d])
        kd = k_base + hk * t
        kd_sum = kd_sum + kd
        _store_pairs(kd_out, kd, d)
        _store_pairs(ba_out, hn + hn * t, d)
    gate_out[...] = _dot(tg, g2_ref[...]).astype(BF16)
    _store_pairs(r_out, r)
    _store_pairs(v_out, v)
    bonus_out[...] = (_head_sum(r * kd_sum * rk_ref[...], e_ref) * v).astype(BF16)


def _rwkv_pre(x, g, p, batch, seq):
    m = x.shape[0]
    tm = TM_RW
    nts = seq // tm
    hb = 16
    nhb = m // hb
    kern = functools.partial(_rwkv_pre_kernel, nts=nts)
    row = lambda: pl.BlockSpec((tm, D_MODEL), lambda i: (i, 0))
    npair = D_MODEL // LANES
    pairs = lambda: pl.BlockSpec((npair, tm, LANES), lambda i: (0, i, 0))
    pairs2 = lambda: pl.BlockSpec((2, npair, tm, LANES), lambda i: (0, 0, i, 0))
    t = jnp.arange(tm)[:, None]
    c = jnp.arange(tm + 2 * hb)[None, :] - hb
    shift = (jnp.where(c == t, -1.0, 0.0) + jnp.where(jnp.abs(c - t) == 1, 0.5, 0.0)).astype(BF16)
    args = (x, x, x, g.reshape(1, D_MODEL), p['mu'], p['wr'], p['wk'], p['wv'], p['w1'], p['w2'], p['w0'],
            p['a1'], p['a2'], p['a0'], p['g1'], p['g2'], p['kk'], p['ka'], p['rk'], p['e'], shift)
    in_specs = [row(),
                pl.BlockSpec((hb, D_MODEL), lambda i: (jnp.maximum(i * (tm // hb) - 1, 0), 0)),
                pl.BlockSpec((hb, D_MODEL), lambda i: (jnp.minimum((i + 1) * (tm // hb), nhb - 1), 0))]
    in_specs += [_full(a.shape) for a in args[3:]]
    out_shape = (jax.ShapeDtypeStruct((npair, m, LANES), BF16),) * 3 + (
        jax.ShapeDtypeStruct((2, npair, m, LANES), BF16), jax.ShapeDtypeStruct((2, npair, m, LANES), BF16),
        jax.ShapeDtypeStruct((2, npair, m, LANES), F32),
        jax.ShapeDtypeStruct((m, D_MODEL), BF16), jax.ShapeDtypeStruct((m, D_MODEL), BF16))
    out_specs = (pairs(), pairs(), pairs(), pairs2(), pairs2(), pairs2(), row(), row())
    return pl.pallas_call(
        kern, grid=(m // tm,), in_specs=in_specs, out_specs=out_specs, out_shape=out_shape,
        compiler_params=_cparams(("parallel",)), name="rwkv_pre",
    )(*args)


def _bd(x, first):
    z = jnp.zeros_like(x)
    return jnp.concatenate([jnp.where(first, x, z), jnp.where(first, z, x)], axis=0)


def _compact(full, first):
    return jnp.where(first, full[:CHUNK], full[CHUNK:])


def _wkv_masks(sgn):
    c = CHUNK
    row = lax.broadcasted_iota(jnp.int32, (c, c), 0)
    col = lax.broadcasted_iota(jnp.int32, (c, c), 1)
    tri = jnp.where((row - col) * sgn >= 0, 1.0, 0.0).astype(BF16)
    prow = lax.broadcasted_iota(jnp.int32, (c, LANES), 0)
    pcol = lax.broadcasted_iota(jnp.int32, (c, LANES), 1) & (c - 1)
    dlt = (prow - pcol) * sgn
    strict = dlt > 0
    same = lambda sh: (prow >> sh) == (pcol >> sh)
    levels = []
    sh = 2
    while (1 << sh) < c:
        levels.append(jnp.where(strict & same(sh + 1) & jnp.logical_not(same(sh)), 1.0, 0.0))
        sh += 1
    in4 = same(2)
    return dict(tri=tri, strict=strict, incl=dlt >= 0, eye=jnp.where(prow == pcol, 1.0, 0.0), levels=levels,
                tri4=jnp.where(strict & in4, 1.0, 0.0),
                off2=jnp.where(in4 & (dlt == 2), 1.0, 0.0), off3=jnp.where(in4 & (dlt == 3), 1.0, 0.0),
                row1=sgn % c, row2=(2 * sgn) % c, lane1=(-sgn) % LANES, lane2=(-2 * sgn) % LANES)


def _inverse4(a_ab, mk):
    a = a_ab * mk['tri4']
    up1 = pltpu.roll(a, mk['row1'], 0)
    up2 = pltpu.roll(a, mk['row2'], 0)
    left1 = pltpu.roll(a, mk['lane1'], 1)
    left2 = pltpu.roll(a, mk['lane2'], 1)
    sq = left1 * up1 * mk['off2'] + (left1 * up2 + left2 * up1) * mk['off3']
    cube = pltpu.roll(sq, mk['lane1'], 1) * up2 * mk['off3']
    return mk['eye'] + a + sq + cube


def _wkv_local(chains, first):
    c = CHUNK
    for s in chains:
        hl = _dot(s['mk']['tri'], jnp.concatenate(_split(s['lw']), axis=1))
        s['cum'] = hl[:, :LANES] + hl[:, LANES:]
    yield
    for s in chains:
        cum, lw = s['cum'], s['lw']
        total = jnp.sum(lw, axis=0, keepdims=True)
        p_inv = jnp.exp(-cum)
        p_hat = jnp.exp(total - cum)
        s['at'] = (-s['kn'] * jnp.exp(cum - lw)).astype(BF16)
        s['rt'] = (s['r'] * jnp.exp(cum)).astype(BF16)
        s['bt'] = (s['ba'] * p_inv).astype(BF16)
        s['kt'] = (s['kd'] * p_inv).astype(BF16)
        s['bh'] = (s['ba'] * p_hat).astype(BF16)
        s['kh'] = (s['kd'] * p_hat).astype(BF16)
        s['v_bf'] = s['v'].astype(BF16)
        s['p_end'] = jnp.exp(total)
        s['lhs'] = jnp.concatenate([s['at'], s['rt']], axis=0)
    yield
    for s in chains:
        s['sbk'] = _dot_nt(s['lhs'], jnp.concatenate([_bd(s['bt'], first), _bd(s['kt'], first)], axis=0))
    yield
    for s in chains:
        mk = s['mk']
        sb, sk = s['sbk'][:, :LANES], s['sbk'][:, LANES:]
        s['a_ab'] = jnp.where(mk['strict'], sb[:c], 0.0)
        s['a_rb'] = jnp.where(mk['incl'], sb[c:], 0.0).astype(BF16)
        s['a_kk'] = jnp.concatenate([jnp.where(mk['strict'], sk[:c], 0.0),
                                     jnp.where(mk['incl'], sk[c:], 0.0)], axis=0).astype(BF16)
        s['inv'] = _inverse4(s['a_ab'], mk)
    yield
    for s in chains:
        s['av'] = _dot(s['a_kk'], _bd(s['v_bf'], first))
        s['y0'] = _compact(_dot_tn(s['v_bf'], s['kh']), first)
    yield
    for lvl in range(len(chains[0]['mk']['levels'])):
        for s in chains:
            s['inv_bf'] = s['inv'].astype(BF16)
            s['inner'] = _dot((s['a_ab'] * s['mk']['levels'][lvl]).astype(BF16), _bd(s['inv_bf'], first))
        yield
        for s in chains:
            s['inv'] = s['inv'] + _dot(s['inv_bf'], _bd(s['inner'].astype(BF16), first))
        yield
    for s in chains:
        rhs = jnp.concatenate([_bd(s['at'], first), _bd(s['av'][:c].astype(BF16), first)], axis=1)
        wu = _dot(s['inv'].astype(BF16), rhs)
        s['w1'] = wu[:, :LANES].astype(BF16)
        s['u0'] = wu[:, LANES:]
        s['o0'] = s['av'][c:]


def _wkv_seq(rows, ys, out_refs, first):
    c = CHUNK
    for row in rows:
        wrs = [_dot_nt(jnp.concatenate([s['w1'], s['rt']], axis=0), _bd(y.astype(BF16), first))
               for s, y in zip(row, ys)]
        yield
        us = [(wr[:c] + s['u0']).astype(BF16) for s, wr in zip(row, wrs)]
        for s, wr, u, (o_ref, p) in zip(row, wrs, us, out_refs):
            o_ref[p, s['sl'], :] = (wr[c:] + _dot(s['a_rb'], _bd(u, first)) + s['o0']).astype(o_ref.dtype)
        ys[:] = [s['p_end'] * y + _compact(_dot_tn(u, s['bh']), first) + s['y0'] for s, y, u in zip(row, ys, us)]
        yield


def _run_staggered(local_gens, make_seq, period):
    end = object()
    started, finished, tick = 0, 0, 0
    active, updates = [], []
    while finished < len(local_gens) or updates:
        if started < len(local_gens) and tick >= started * period:
            active.append(started)
            started += 1
        for g in list(active):
            if next(local_gens[g], end) is end:
                active.remove(g)
                finished += 1
                updates.append(make_seq(g))
        if updates and next(updates[0], end) is end:
            updates.pop(0)
        tick += 1


def _wkv_kernel(rf, vf, nf, kdf, baf, lwf, rb, vb, nb, kdb, bab, lwb, of_ref, ob_ref, state, *, ts, npairs):
    first = lax.broadcasted_iota(jnp.int32, (CHUNK, LANES), 1) < RWKV_HEAD

    @pl.when(pl.program_id(2) == 0)
    def _():
        state[...] = jnp.zeros_like(state)

    nchunk = ts // CHUNK
    names = ('r', 'v', 'kn', 'kd', 'ba', 'lw')
    dirs = ((_wkv_masks(1), (rf, vf, nf, kdf, baf, lwf), of_ref, list(range(nchunk))),
            (_wkv_masks(-1), (rb, vb, nb, kdb, bab, lwb), ob_ref, list(range(nchunk - 1, -1, -1))))
    scans = [(d, p) for d in range(2) for p in range(npairs)]
    out_refs = [(dirs[d][2], p) for d, p in scans]
    steps = []
    for k in range(nchunk):
        row = []
        for d, p in scans:
            mk, ins, _, order = dirs[d]
            sl = pl.ds(order[k] * CHUNK, CHUNK)
            chain = {n: ref[p, sl, :].astype(F32) for n, ref in zip(names, ins)}
            chain.update(mk=mk, sl=sl)
            row.append(chain)
        steps.append(row)
    ys = [state[d, p] for d, p in scans]
    groups = [steps[k:k + WKV_GROUP] for k in range(0, nchunk, WKV_GROUP)]
    _run_staggered([_wkv_local([s for row in rows for s in row], first) for rows in groups],
                   lambda g: _wkv_seq(groups[g], ys, out_refs, first), WKV_STAGGER)
    for (d, p), y in zip(scans, ys):
        state[d, p] = y


def _wkv(r, v, kn, kd, ba, lw, batch, seq):
    ts, npairs = TS_WKV, PAIRS_WKV
    nt = seq // ts
    npair = D_MODEL // LANES
    m = batch * seq
    fwd = lambda b, t: b * nt + t
    bwd = lambda b, t: b * nt + nt - 1 - t
    s3 = lambda at: pl.BlockSpec((npairs, ts, LANES), lambda b, g, t: (g, at(b, t), 0))
    s4 = lambda d, at: pl.BlockSpec((None, npairs, ts, LANES), lambda b, g, t: (d, g, at(b, t), 0))
    return pl.pallas_call(
        functools.partial(_wkv_kernel, ts=ts, npairs=npairs),
        grid=(batch, npair // npairs, nt),
        in_specs=[s3(fwd), s3(fwd), s3(fwd), s4(0, fwd), s4(0, fwd), s4(0, fwd),
                  s3(bwd), s3(bwd), s3(bwd), s4(1, bwd), s4(1, bwd), s4(1, bwd)],
        out_specs=(s3(fwd), s3(bwd)),
        out_shape=(jax.ShapeDtypeStruct((npair, m, LANES), BF16),) * 2,
        scratch_shapes=[pltpu.VMEM((2, npairs, CHUNK, LANES), F32)],
        compiler_params=_cparams(("parallel", "parallel", "arbitrary")),
        name="wkv",
    )(r, v, kn, kd, ba, lw, r, v, kn, kd, ba, lw)


def _rwkv_post_kernel(of_ref, ob_ref, bonus_ref, gate_ref, lw_ref, lb_ref, e_ref, wo_ref, x_ref, out_ref):
    y = jnp.concatenate([of_ref[p].astype(F32) + ob_ref[p].astype(F32) for p in range(of_ref.shape[0])], axis=1)
    mean = _head_sum(y, e_ref) * (1.0 / RWKV_HEAD)
    yc = y - mean
    var = _head_sum(yc * yc, e_ref) * (1.0 / RWKV_HEAD)
    yn = yc * lax.rsqrt(var + GN_EPS)
    y2 = yn * lw_ref[...] + lb_ref[...] + bonus_ref[...].astype(F32)
    out_ref[...] = x_ref[...] + _dot((y2 * gate_ref[...].astype(F32)).astype(BF16), wo_ref[...])


def _odd_tail_kernel(*refs, nsub):
    mixer_in, cross_in, out_ref, x_mid = refs[:9], refs[9:14], refs[14], refs[15]
    _rwkv_post_kernel(*mixer_in, x_mid)
    _cross_kernel(x_mid, *cross_in, out_ref, nsub=nsub)


def _odd_tail(x, o_f, o_b, bonus, gate, p, cross, seq):
    g_cross, kv, wq, wo, n_mem = cross
    m = x.shape[0]
    tm = TM
    row = lambda: pl.BlockSpec((tm, D_MODEL), lambda i: (i, 0))
    pairs = lambda: pl.BlockSpec((D_MODEL // LANES, tm, LANES), lambda i: (0, i, 0))
    return pl.pallas_call(
        functools.partial(_odd_tail_kernel, nsub=2),
        grid=(m // tm,),
        in_specs=[pairs(), pairs(), row(), row(), _full((1, D_MODEL)), _full((1, D_MODEL)),
                  _full(p['e'].shape), _full((D_MODEL, D_MODEL)), row()] + _cross_specs(seq, n_mem, tm),
        out_specs=row(),
        out_shape=jax.ShapeDtypeStruct((m, D_MODEL), F32),
        scratch_shapes=[pltpu.VMEM((tm, D_MODEL), F32)],
        compiler_params=_cparams(("parallel",)),
        name="odd_tail",
    )(o_f, o_b, bonus, gate, p['lnx_w'], p['lnx_b'], p['e'], p['wo'], x,
      g_cross.reshape(1, D_MODEL), kv, kv, wq, wo)


def _prep_odd(o, rw_mu, rw_wr, rw_wk, rw_wv, rw_wo, rw_w0, rw_w1, rw_w2, rw_a0, rw_a1, rw_a2, rw_g1, rw_g2,
              rw_kk, rw_ka, rw_rk, rw_lnx_w, rw_lnx_b):
    bf = lambda a: a.astype(BF16)

    def pad_dir(w):
        z = jnp.zeros_like(w[0])
        return jnp.stack([jnp.concatenate([w[0], z], axis=0), jnp.concatenate([z, w[1]], axis=0)])

    lane = jnp.arange(MXU_WIDTH) // RWKV_HEAD
    return dict(
        mu=rw_mu[o], wr=bf(rw_wr[o]), wk=bf(rw_wk[o]), wv=bf(rw_wv[o]), wo=bf(rw_wo[o]),
        w0=rw_w0[o], w1=bf(jnp.concatenate([rw_w1[o, 0], rw_w1[o, 1]], axis=1)), w2=bf(pad_dir(rw_w2[o])),
        a0=rw_a0[o], a1=bf(jnp.concatenate([rw_a1[o, 0], rw_a1[o, 1]], axis=1)), a2=bf(pad_dir(rw_a2[o])),
        g1=bf(rw_g1[o]), g2=bf(rw_g2[o]),
        kk=rw_kk[o].reshape(1, D_MODEL), ka=rw_ka[o].reshape(1, D_MODEL), rk=rw_rk[o].reshape(1, D_MODEL),
        lnx_w=rw_lnx_w[o].reshape(1, D_MODEL), lnx_b=rw_lnx_b[o].reshape(1, D_MODEL),
        e=(lane[:, None] == lane[None, :]).astype(BF16))


def _even_layer(x, g, w_in, w_out, conv_w, cos, sin, cross, batch, seq):
    z1, z4, z16, zc = _even_proj(x, g, w_in, cos, sin, batch, seq)
    m = x.shape[0]
    o1, l1 = _dilated_branch(z1.reshape(3, batch, seq, A_WIDTH), seq)
    o4, l4 = _dilated_branch(z4.reshape(3, batch * 4, seq // 4, A_WIDTH), seq // 4)
    o16, l16 = _dilated_branch(z16.reshape(3, batch * 16, seq // 16, A_WIDTH), seq // 16)
    o1, l1 = o1.reshape(m, A_WIDTH), l1.reshape(m, A_WIDTH)
    d4 = lambda a: a.reshape(batch, 4, seq // 4, A_WIDTH)
    d16 = lambda a: a.reshape(batch, 16, seq // 16, A_WIDTH)
    return _even_tail(x, o1, l1, d4(o4), d4(l4), d16(o16), d16(l16), zc, conv_w, w_out, cross, batch, seq)


def _odd_layer(x, g, p, cross, batch, seq):
    r, v, kn, kd, ba, lw, gate, bonus = _rwkv_pre(x, g, p, batch, seq)
    o_f, o_b = _wkv(r, v, kn, kd, ba, lw, batch, seq)
    return _odd_tail(x, o_f, o_b, bonus, gate, p, cross, seq)


def _trunk(x, mem, w):
    batch, seq, _ = x.shape
    n_mem = mem.shape[1]
    x = x.reshape(batch * seq, D_MODEL)
    mem = mem.reshape(batch * n_mem, D_MODEL)
    cos, sin = _rope_tables(seq)
    for l in range(DEPTH):
        kv = _norm_matmul(mem, w['norm_mem'][l], w['ca_wkv'][l], TM, 2 * D_MODEL, BF16)
        cross = (w['norm_cross'][l], kv, w['ca_wq'][l], w['ca_wo'][l], n_mem)
        if l % 2 == 0:
            e = l // 2
            x = _even_layer(x, w['norm_mix'][l], w['ab_w_in'][e], w['ab_w_out'][e], w['ab_conv'][e],
                            cos, sin, cross, batch, seq)
        else:
            x = _odd_layer(x, w['norm_mix'][l], w['odd'][l // 2], cross, batch, seq)
        x = _ffn(x, w['norm_ffn'][l], w['ffn_wgu'][l], w['ffn_wdown'][l], w['norm_final'], l == DEPTH - 1)
    return x.reshape(batch, seq, D_MODEL)


def kernel(x_prompt, x_sample, mem_prompt, mem_sample, norm_mix, norm_cross, norm_mem, norm_ffn, norm_final,
           ab_w_in, ab_w_out, ab_conv, rw_mu, rw_wr, rw_wk, rw_wv, rw_wo, rw_w0, rw_w1, rw_w2, rw_a0, rw_a1,
           rw_a2, rw_g1, rw_g2, rw_kk, rw_ka, rw_rk, rw_lnx_w, rw_lnx_b, ca_wq, ca_wkv, ca_wo, ffn_wgu,
           ffn_wdown):
    bf = lambda a: a.astype(BF16)
    w = dict(norm_mix=norm_mix, norm_cross=norm_cross, norm_mem=norm_mem, norm_ffn=norm_ffn,
             norm_final=norm_final, ab_w_in=bf(ab_w_in), ab_w_out=bf(ab_w_out), ab_conv=ab_conv,
             ca_wq=bf(ca_wq), ca_wkv=bf(ca_wkv), ca_wo=bf(ca_wo), ffn_wgu=bf(ffn_wgu), ffn_wdown=bf(ffn_wdown),
             odd=[_prep_odd(o, rw_mu, rw_wr, rw_wk, rw_wv, rw_wo, rw_w0, rw_w1, rw_w2, rw_a0, rw_a1, rw_a2,
                            rw_g1, rw_g2, rw_kk, rw_ka, rw_rk, rw_lnx_w, rw_lnx_b)
                  for o in range(rw_mu.shape[0])])
    return _trunk(x_prompt, mem_prompt, w), _trunk(x_sample, mem_sample, w)
```
